```python
import jax, jax.numpy as jnp
from jax import lax
import numpy as np

D_MODEL = 1024
BATCH = 4
SEQ = 4096
DEPTH = 2

HEAD_DIM = 64
CONV_CH = D_MODEL // 4
FOX_HEADS = (D_MODEL // 4) // HEAD_DIM
DIL_HEADS = (D_MODEL // 2) // HEAD_DIM
FOX_W = FOX_HEADS * HEAD_DIM
DIL_W = DIL_HEADS * HEAD_DIM
CONV_K = 31
DIL_CONFIGS = ((128, 1), (512, 4), (2048, 16))
BLK = 128
ROPE_THETA = 10000.0
D_FF = 2816
ALPHA = (2 * DEPTH) ** 0.25
BETA = (8 * DEPTH) ** -0.25
LN_EPS = 1e-5
FORGET_BIAS_CENTER = 3.0
IN_WIDTHS = (CONV_CH, CONV_CH, FOX_W, FOX_W, FOX_W, FOX_HEADS, DIL_W, DIL_W, DIL_W)
IN_COLS = sum(IN_WIDTHS)
SPLIT_POINTS = tuple(int(v) for v in np.cumsum(IN_WIDTHS)[:-1])

kernel_name = "hymba_style_conv_fox_dilated_deepnorm"


def layer_norm(x, g, b):
    xf = x.astype(jnp.float32)
    mu = jnp.mean(xf, axis=-1, keepdims=True)
    var = jnp.mean(jnp.square(xf - mu), axis=-1, keepdims=True)
    return ((xf - mu) * lax.rsqrt(var + LN_EPS)).astype(x.dtype) * g + b


def swiglu_ffn(x, w_in, w_out):
    gate, up = jnp.split(x @ w_in, 2, axis=-1)
    return (jax.nn.silu(gate) * up) @ w_out


def rope(t, positions):
    inv = 1.0 / (ROPE_THETA ** (jnp.arange(0, HEAD_DIM, 2, dtype=jnp.float32) / HEAD_DIM))
    ang = positions.astype(jnp.float32)[:, None] * inv[None, :]
    ang = jnp.concatenate([ang, ang], axis=-1)
    cos = jnp.cos(ang).astype(t.dtype)
    sin = jnp.sin(ang).astype(t.dtype)
    t1, t2 = jnp.split(t, 2, axis=-1)
    return t * cos + jnp.concatenate([-t2, t1], axis=-1) * sin


def conv_module(val, gate, conv_w, conv_b, g, b):
    u = val * jax.nn.sigmoid(gate)
    u = lax.conv_general_dilated(
        u, conv_w[:, None, :].astype(u.dtype), window_strides=(1,),
        padding=[(CONV_K - 1, 0)], dimension_numbers=('NWC', 'WIO', 'NWC'),
        feature_group_count=CONV_CH) + conv_b
    return jax.nn.silu(layer_norm(u, g, b))


def forgetting_attention(q, k, v, f_logit, f_bias):
    bsz, nh, s_len, dh = q.shape
    log_f = jax.nn.log_sigmoid((f_logit + f_bias).astype(jnp.float32))
    cum = jnp.cumsum(log_f, axis=1).transpose(0, 2, 1)
    nq = s_len // BLK
    qb = q.reshape(bsz, nh, nq, BLK, dh).transpose(2, 0, 1, 3, 4)
    cb = cum.reshape(bsz, nh, nq, BLK).transpose(2, 0, 1, 3)
    kpos = jnp.arange(s_len)
    scale = HEAD_DIM ** -0.5

    def block(args):
        qi, ci, start = args
        s = jnp.einsum('bhqe,bhke->bhqk', qi, k).astype(jnp.float32) * scale
        s = s + (ci[..., :, None] - cum[..., None, :])
        qpos = start + jnp.arange(BLK)
        s = jnp.where(kpos[None, :] <= qpos[:, None], s, -jnp.inf)
        p = jax.nn.softmax(s, axis=-1)
        return jnp.einsum('bhqk,bhke->bhqe', p.astype(v.dtype), v)

    out = lax.map(block, (qb, cb, jnp.arange(nq) * BLK))
    return out.transpose(1, 2, 0, 3, 4).reshape(bsz, nh, s_len, dh)


def dilated_branch(q, k, v, window, dilation):
    bsz, nh, s_len, dh = q.shape
    span = window // dilation
    seg = dilation * BLK
    s_pad = -(-s_len // seg) * seg
    m_len = s_pad // dilation
    nb = m_len // BLK

    def to_sub(t):
        t = jnp.pad(t, ((0, 0), (0, 0), (0, s_pad - s_len), (0, 0)))
        t = t.reshape(bsz, nh, m_len, dilation, dh).transpose(0, 1, 3, 2, 4)
        return t.reshape(bsz, nh, dilation, nb, BLK, dh)

    def with_prev(t):
        prev = jnp.pad(t, ((0, 0), (0, 0), (0, 0), (1, 0), (0, 0), (0, 0)))[:, :, :, :-1]
        return jnp.concatenate([prev, t], axis=4)

    qs = to_sub(q)
    kw = with_prev(to_sub(k))
    vw = with_prev(to_sub(v))
    s = jnp.einsum('bhrnqe,bhrnke->bhrnqk', qs, kw).astype(jnp.float32)
    qm = jnp.arange(nb)[:, None, None] * BLK + jnp.arange(BLK)[None, :, None]
    km = jnp.arange(nb)[:, None, None] * BLK + jnp.arange(2 * BLK)[None, None, :] - BLK
    dist = qm - km
    valid = (km >= 0) & (dist >= 0) & (dist <= span)
    s = jnp.where(valid, s, -jnp.inf)
    mx = jnp.max(s, axis=-1, keepdims=True)
    p = jnp.exp(s - mx)
    den = jnp.sum(p, axis=-1, keepdims=True)
    o = jnp.einsum('bhrnqk,bhrnke->bhrnqe', (p / den).astype(v.dtype), vw)
    lse = (mx + jnp.log(den))[..., 0]
    o = o.reshape(bsz, nh, dilation, m_len, dh).transpose(0, 1, 3, 2, 4)
    o = o.reshape(bsz, nh, s_pad, dh)[:, :, :s_len]
    lse = lse.reshape(bsz, nh, dilation, m_len).transpose(0, 1, 3, 2)
    lse = lse.reshape(bsz, nh, s_pad)[:, :, :s_len]
    return o, lse


def dilated_attention(q, k, v):
    branches = [dilated_branch(q, k, v, w, d) for w, d in DIL_CONFIGS]
    outs = jnp.stack([o for o, _ in branches], axis=0)
    lses = jnp.stack([l for _, l in branches], axis=0)
    wts = jax.nn.softmax(lses, axis=0)
    return jnp.einsum('nbhs,nbhse->bhse', wts.astype(outs.dtype), outs)


def hybrid_mixer(x, w_in, w_o, f_bias, conv_w, conv_b, conv_g, conv_beta, positions):
    bsz, s_len, _ = x.shape
    proj = x @ w_in
    a_val, a_gate, bq, bk, bv, bf, cq, ck, cv = jnp.split(proj, SPLIT_POINTS, axis=-1)

    def heads(t, nh):
        return t.reshape(bsz, s_len, nh, HEAD_DIM).transpose(0, 2, 1, 3)

    def merge(t):
        return t.transpose(0, 2, 1, 3).reshape(bsz, s_len, -1)

    y_a = conv_module(a_val, a_gate, conv_w, conv_b, conv_g, conv_beta)
    y_b = forgetting_attention(heads(bq, FOX_HEADS), heads(bk, FOX_HEADS),
                               heads(bv, FOX_HEADS), bf, f_bias)
    qc = rope(heads(cq, DIL_HEADS), positions) * (HEAD_DIM ** -0.5)
    kc = rope(heads(ck, DIL_HEADS), positions)
    y_c = dilated_attention(qc, kc, heads(cv, DIL_HEADS))
    y = jnp.concatenate([y_a, merge(y_b), merge(y_c)], axis=-1)
    return y @ w_o


def setup_inputs(seed: int = 0) -> dict:
    key = jax.random.key(seed)
    ks = jax.random.split(key, 12)
    f32 = jnp.float32
    x = jax.random.normal(ks[0], (BATCH, SEQ, D_MODEL), f32)
    w_in = jax.random.normal(ks[1], (DEPTH, D_MODEL, IN_COLS), f32) * D_MODEL ** -0.5
    w_o = jax.random.normal(ks[2], (DEPTH, D_MODEL, D_MODEL), f32) * (D_MODEL ** -0.5) * BETA
    forget_bias = FORGET_BIAS_CENTER + 0.5 * jax.random.normal(ks[3], (DEPTH, FOX_HEADS), f32)
    conv_w = jax.random.normal(ks[4], (DEPTH, CONV_K, CONV_CH), f32) * CONV_K ** -0.5
    conv_b = 0.02 * jax.random.normal(ks[5], (DEPTH, CONV_CH), f32)
    conv_ln_g = 1.0 + 0.02 * jax.random.normal(ks[6], (DEPTH, CONV_CH), f32)
    conv_ln_b = 0.02 * jax.random.normal(ks[7], (DEPTH, CONV_CH), f32)
    ffn_w_in = jax.random.normal(ks[8], (DEPTH, 2, D_MODEL, 2 * D_FF), f32) * D_MODEL ** -0.5
    ffn_w_out = jax.random.normal(ks[9], (DEPTH, 2, D_FF, D_MODEL), f32) * (D_FF ** -0.5) * BETA
    ln_g = 1.0 + 0.02 * jax.random.normal(ks[10], (DEPTH, 3, D_MODEL), f32)
    ln_b = 0.02 * jax.random.normal(ks[11], (DEPTH, 3, D_MODEL), f32)
    return {"x": x, "w_in": w_in, "w_o": w_o, "forget_bias": forget_bias,
            "conv_w": conv_w, "conv_b": conv_b, "conv_ln_g": conv_ln_g,
            "conv_ln_b": conv_ln_b, "ffn_w_in": ffn_w_in, "ffn_w_out": ffn_w_out,
            "ln_g": ln_g, "ln_b": ln_b}


def reference(x, w_in, w_o, forget_bias, conv_w, conv_b, conv_ln_g, conv_ln_b,
              ffn_w_in, ffn_w_out, ln_g, ln_b):
    positions = jnp.arange(x.shape[1])
    h = x
    for l in range(DEPTH):
        h = layer_norm(ALPHA * h + 0.5 * swiglu_ffn(h, ffn_w_in[l, 0], ffn_w_out[l, 0]),
                       ln_g[l, 0], ln_b[l, 0])
        h = layer_norm(ALPHA * h + hybrid_mixer(h, w_in[l], w_o[l], forget_bias[l], conv_w[l],
                                                conv_b[l], conv_ln_g[l], conv_ln_b[l], positions),
                       ln_g[l, 1], ln_b[l, 1])
        h = layer_norm(ALPHA * h + 0.5 * swiglu_ffn(h, ffn_w_in[l, 1], ffn_w_out[l, 1]),
                       ln_g[l, 2], ln_b[l, 2])
    return h
```

```python
import functools

import jax
import jax.numpy as jnp
from jax import lax
from jax.experimental import pallas as pl
from jax.experimental.pallas import tpu as pltpu

HEAD_DIM = 64
CONV_K = 31
DIL_CONFIGS = ((128, 1), (512, 4), (2048, 16))
ROPE_THETA = 10000.0
LN_EPS = 1e-5
LANES = 128
HEADS_PER_SLAB = LANES // HEAD_DIM
NEG_BIG = -1e30
VMEM_LIMIT = 56 * 1024 * 1024

F32 = jnp.float32
BF16 = jnp.bfloat16


def _layer_norm(y, g, b):
    mu = jnp.mean(y, axis=-1, keepdims=True)
    yc = y - mu
    var = jnp.mean(yc * yc, axis=-1, keepdims=True)
    return yc * lax.rsqrt(var + LN_EPS) * g + b


def _params(*sem):
    return pltpu.CompilerParams(dimension_semantics=sem, vmem_limit_bytes=VMEM_LIMIT)


def _resident(shape):
    nd = len(shape)
    return pl.BlockSpec(shape, lambda *_: (0,) * nd, pipeline_mode=pl.Buffered(1))


def _ffn_kernel(x_ref, win_ref, wout_ref, g_ref, b_ref, o_ref, acc_ref, *, alpha, d_ff, chunk):
    x = x_ref[...]
    xb = x.astype(BF16)
    for c in range(d_ff // chunk):
        lo = c * chunk
        gate = jnp.dot(xb, win_ref[:, lo:lo + chunk], preferred_element_type=F32)
        up = jnp.dot(xb, win_ref[:, d_ff + lo:d_ff + lo + chunk], preferred_element_type=F32)
        hid = (gate * jax.nn.sigmoid(gate) * up).astype(BF16)
        part = jnp.dot(hid, wout_ref[lo:lo + chunk, :], preferred_element_type=F32)
        if c == 0:
            acc_ref[...] = part
        else:
            acc_ref[...] += part
    y = alpha * x + 0.5 * acc_ref[...]
    o_ref[...] = _layer_norm(y, g_ref[...], b_ref[...])


def _ffn_ln(x, w_in, w_out, g, b, *, alpha, tm=512, chunk=256):
    t, d = x.shape
    d_ff = w_out.shape[0]
    return pl.pallas_call(
        functools.partial(_ffn_kernel, alpha=alpha, d_ff=d_ff, chunk=chunk),
        grid=(t // tm,),
        in_specs=[pl.BlockSpec((tm, d), lambda i: (i, 0)),
                  _resident(w_in.shape), _resident(w_out.shape),
                  _resident(g.shape), _resident(b.shape)],
        out_specs=pl.BlockSpec((tm, d), lambda i: (i, 0)),
        out_shape=jax.ShapeDtypeStruct((t, d), F32),
        scratch_shapes=[pltpu.VMEM((tm, d), F32)],
        compiler_params=_params("parallel"),
        name="ffn_ln",
    )(x, w_in, w_out, g, b)


def _rope_slab(t, cos, sin_lo, sin_hi):
    nxt = pltpu.roll(t, LANES - HEAD_DIM // 2, axis=1)
    prv = pltpu.roll(t, HEAD_DIM // 2, axis=1)
    return t * cos + nxt * sin_lo + prv * sin_hi


def _proj_kernel(h_ref, w_ref, fb_ref, cos_ref, slo_ref, shi_ref,
                 u_ref, fq_ref, fk_ref, fv_ref, lf_ref, dq_ref, dk_ref, dv_ref, *, cw, fw, dw):
    hb = h_ref[...].astype(BF16)

    def cols(lo, width):
        return jnp.dot(hb, w_ref[:, lo:lo + width], preferred_element_type=F32)

    scale = HEAD_DIM ** -0.5
    val = cols(0, cw)
    gate = cols(cw, cw)
    u_ref[...] = val * jax.nn.sigmoid(gate)
    base = 2 * cw
    fq_ref[...] = (cols(base, fw) * scale).astype(BF16)
    fk_ref[...] = cols(base + fw, fw).astype(BF16)
    fv_ref[...] = cols(base + 2 * fw, fw).astype(BF16)
    lf_ref[...] = jax.nn.log_sigmoid(cols(base + 3 * fw, fw) + fb_ref[...])
    base = base + 4 * fw
    cos, slo, shi = cos_ref[...], slo_ref[...], shi_ref[...]
    for j in range(dw // LANES):
        lo = j * LANES
        dq_ref[:, lo:lo + LANES] = _rope_slab(cols(base + lo, LANES), cos, slo, shi) * scale
        dk_ref[:, lo:lo + LANES] = _rope_slab(cols(base + dw + lo, LANES), cos, slo, shi)
    dv_ref[...] = cols(base + 2 * dw, dw)


def _mixer_proj(h, w, fb, cos, slo, shi, *, seq, cw, fw, dw, tm=512):
    t, d = h.shape
    nsb = seq // tm
    row = lambda i: (i, 0)
    pos = lambda i: (i % nsb, 0)
    outs = [(cw, F32), (fw, BF16), (fw, BF16), (fw, BF16), (fw, F32), (dw, F32), (dw, F32), (dw, F32)]
    return pl.pallas_call(
        functools.partial(_proj_kernel, cw=cw, fw=fw, dw=dw),
        grid=(t // tm,),
        in_specs=[pl.BlockSpec((tm, d), row), _resident(w.shape), _resident(fb.shape),
                  pl.BlockSpec((tm, LANES), pos), pl.BlockSpec((tm, LANES), pos),
                  pl.BlockSpec((tm, LANES), pos)],
        out_specs=[pl.BlockSpec((tm, n), row) for n, _ in outs],
        out_shape=[jax.ShapeDtypeStruct((t, n), dt) for n, dt in outs],
        compiler_params=_params("parallel"),
        name="mixer_proj",
    )(h, w, fb, cos, slo, shi)


def _cumsum_kernel(lf_ref, col_ref, row_ref, *, blk, heads):
    r = lax.broadcasted_iota(jnp.int32, (blk, blk), 0)
    c = lax.broadcasted_iota(jnp.int32, (blk, blk), 1)
    tri = (c <= r).astype(BF16)
    width = lf_ref.shape[-1]

    def body(i, carry):
        t0 = pl.multiple_of(i * blk, blk)
        x = lf_ref[pl.ds(t0, blk), :]
        hi = x.astype(BF16)
        r1 = x - hi.astype(F32)
        mid = r1.astype(BF16)
        lo = (r1 - mid.astype(F32)).astype(BF16)
        cum = (jnp.dot(tri, hi, preferred_element_type=F32)
               + jnp.dot(tri, mid, preferred_element_type=F32)
               + jnp.dot(tri, lo, preferred_element_type=F32)) + carry
        col_ref[pl.ds(t0, blk), :] = cum
        cum_t = cum.T
        for h in range(heads):
            row_ref[h, :, pl.ds(t0, blk)] = cum_t[h * HEAD_DIM:h * HEAD_DIM + 1, :]
        return cum[blk - 1:blk, :]

    lax.fori_loop(0, lf_ref.shape[0] // blk, body, jnp.zeros((1, width), F32))


def _forget_cumsum(lf, *, heads, blk=128):
    bsz, seq, width = lf.shape
    return pl.pallas_call(
        functools.partial(_cumsum_kernel, blk=blk, heads=heads),
        grid=(bsz,),
        in_specs=[pl.BlockSpec((None, seq, width), lambda b: (b, 0, 0))],
        out_specs=[pl.BlockSpec((None, seq, width), lambda b: (b, 0, 0)),
                   pl.BlockSpec((None, heads, 1, seq), lambda b: (b, 0, 0, 0))],
        out_shape=[jax.ShapeDtypeStruct((bsz, seq, width), F32),
                   jax.ShapeDtypeStruct((bsz, heads, 1, seq), F32)],
        compiler_params=_params("parallel"),
        name="forget_cumsum",
    )(lf)


def _conv_kernel(u_ref, w_ref, cb_ref, g_ref, b_ref, o_ref, pad_ref, *, rows, halo):
    seq, ch = u_ref.shape
    pad_ref[0:halo, :] = jnp.zeros((halo, ch), F32)
    pad_ref[halo:halo + seq, :] = u_ref[...]
    shift = halo - (CONV_K - 1)

    span = rows + halo
    sub = 8

    def body(i, _):
        t0 = pl.multiple_of(i * rows, rows)
        win = pad_ref[pl.ds(t0, span), :]
        acc = jnp.zeros((rows, ch), F32) + cb_ref[...]
        for b in range(sub):
            rolled = win if b == 0 else pltpu.roll(win, span - b, axis=0)
            for a in range(halo // sub + 1):
                k = a * sub + b - shift
                if 0 <= k < CONV_K:
                    acc = acc + w_ref[k:k + 1, :] * rolled[a * sub:a * sub + rows, :]
        y = _layer_norm(acc, g_ref[...], b_ref[...])
        o_ref[pl.ds(t0, rows), :] = (y * jax.nn.sigmoid(y)).astype(o_ref.dtype)
        return 0

    lax.fori_loop(0, seq // rows, body, 0)


def _conv_module(u, w, cb, g, b, *, rows=64, halo=32):
    bsz, seq, ch = u.shape
    return pl.pallas_call(
        functools.partial(_conv_kernel, rows=rows, halo=halo),
        grid=(bsz,),
        in_specs=[pl.BlockSpec((None, seq, ch), lambda i: (i, 0, 0)),
                  _resident(w.shape), _resident(cb.shape), _resident(g.shape), _resident(b.shape)],
        out_specs=pl.BlockSpec((None, seq, ch), lambda i: (i, 0, 0)),
        out_shape=jax.ShapeDtypeStruct((bsz, seq, ch), BF16),
        scratch_shapes=[pltpu.VMEM((halo + seq, ch), F32)],
        compiler_params=_params("parallel"),
        name="conv_module",
    )(u, w, cb, g, b)


def _fox_kernel(q_ref, k_ref, v_ref, cc_ref, cr_ref, o_ref, m_ref, l_ref, acc_ref, *, tq):
    i = pl.program_id(2)
    q2 = q_ref[...]
    lane = lax.broadcasted_iota(jnp.int32, (1, LANES), 1)
    rr = lax.broadcasted_iota(jnp.int32, (tq, tq), 0)
    cc = lax.broadcasted_iota(jnp.int32, (tq, tq), 1)
    causal = cc <= rr

    for a in range(HEADS_PER_SLAB):
        qa = jnp.where(lane // HEAD_DIM == a, q2, jnp.zeros_like(q2))
        ci = cc_ref[:, a * HEAD_DIM:a * HEAD_DIM + 1]

        def scores(j):
            k0 = pl.multiple_of(j * tq, tq)
            s = lax.dot_general(qa, k_ref[pl.ds(k0, tq), :], (((1,), (1,)), ((), ())),
                                preferred_element_type=F32)
            return s + (ci - cr_ref[a, :, pl.ds(k0, tq)]), k0

        s, k0 = scores(i)
        s = jnp.where(causal, s, NEG_BIG)
        m = jnp.max(s, axis=-1, keepdims=True)
        p = jnp.exp(s - m)
        m_ref[a] = m
        l_ref[a] = jnp.sum(p, axis=-1, keepdims=True)
        acc_ref[a] = jnp.dot(p.astype(BF16), v_ref[pl.ds(k0, tq), :], preferred_element_type=F32)

        def body(j, _):
            s, k0 = scores(j)
            m_old = m_ref[a]
            m_new = jnp.maximum(m_old, jnp.max(s, axis=-1, keepdims=True))
            alpha = jnp.exp(m_old - m_new)
            p = jnp.exp(s - m_new)
            m_ref[a] = m_new
            l_ref[a] = alpha * l_ref[a] + jnp.sum(p, axis=-1, keepdims=True)
            acc_ref[a] = alpha * acc_ref[a] + jnp.dot(
                p.astype(BF16), v_ref[pl.ds(k0, tq), :], preferred_element_type=F32)
            return 0

        lax.fori_loop(0, i, body, 0)

    out = jnp.where(lane // HEAD_DIM == 0, acc_ref[0] / l_ref[0], acc_ref[1] / l_ref[1])
    o_ref[...] = out.astype(o_ref.dtype)


def _fox_attention(q, k, v, cum_col, cum_row, *, tq=512):
    bsz, seq, width = q.shape
    slabs = width // LANES
    qspec = pl.BlockSpec((None, tq, LANES), lambda b, p, i: (b, i, p))
    kvspec = pl.BlockSpec((None, seq, LANES), lambda b, p, i: (b, 0, p))
    return pl.pallas_call(
        functools.partial(_fox_kernel, tq=tq),
        grid=(bsz, slabs, seq // tq),
        in_specs=[qspec, kvspec, kvspec, qspec,
                  pl.BlockSpec((None, HEADS_PER_SLAB, 1, seq), lambda b, p, i: (b, p, 0, 0))],
        out_specs=qspec,
        out_shape=jax.ShapeDtypeStruct((bsz, seq, width), BF16),
        scratch_shapes=[pltpu.VMEM((HEADS_PER_SLAB, tq, 1), F32),
                        pltpu.VMEM((HEADS_PER_SLAB, tq, 1), F32),
                        pltpu.VMEM((HEADS_PER_SLAB, tq, LANES), F32)],
        compiler_params=_params("parallel", "parallel", "arbitrary"),
        name="fox_attention",
    )(q, k, v, cum_col, cum_row)


def _dil_kernel(q_ref, k_ref, v_ref, o_ref, m_ref, l_ref, acc_ref, *, blk):
    seq = q_ref.shape[0]
    lane = lax.broadcasted_iota(jnp.int32, (1, LANES), 1)
    head0 = lane // HEAD_DIM == 0
    rr = lax.broadcasted_iota(jnp.int32, (blk, blk), 0)
    cc = lax.broadcasted_iota(jnp.int32, (blk, blk), 1)
    cur_ok = cc <= rr
    prev_ok = cc >= rr

    for _, dil in DIL_CONFIGS:
        nb = seq // (dil * blk)

        def body(idx, _, dil=dil, nb=nb):
            r = idx // nb
            n = idx % nb
            start = r + n * (dil * blk)
            has_prev = n > 0
            start_prev = start - jnp.where(has_prev, dil * blk, 0)
            rows = pl.ds(start, blk, stride=dil) if dil > 1 else pl.ds(start, blk)
            rows_prev = pl.ds(start_prev, blk, stride=dil) if dil > 1 else pl.ds(start_prev, blk)
            qb = q_ref[rows, :].astype(BF16)
            kc = k_ref[rows, :].astype(BF16)
            vc = v_ref[rows, :].astype(BF16)
            kp = k_ref[rows_prev, :].astype(BF16)
            vp = v_ref[rows_prev, :].astype(BF16)
            pmask = jnp.logical_and(prev_ok, has_prev)
            ms, ls, pvs = [], [], []
            for a in range(HEADS_PER_SLAB):
                qa = jnp.where(lane // HEAD_DIM == a, qb, jnp.zeros_like(qb))
                dn = (((1,), (1,)), ((), ()))
                sc = jnp.where(cur_ok, lax.dot_general(qa, kc, dn, preferred_element_type=F32), NEG_BIG)
                sp = jnp.where(pmask, lax.dot_general(qa, kp, dn, preferred_element_type=F32), NEG_BIG)
                m = jnp.maximum(jnp.max(sc, axis=-1, keepdims=True), jnp.max(sp, axis=-1, keepdims=True))
                pc = jnp.exp(sc - m)
                pp = jnp.exp(sp - m)
                ms.append(m)
                ls.append(jnp.sum(pc, axis=-1, keepdims=True) + jnp.sum(pp, axis=-1, keepdims=True))
                pvs.append(jnp.dot(pc.astype(BF16), vc, preferred_element_type=F32)
                           + jnp.dot(pp.astype(BF16), vp, preferred_element_type=F32))
            m_b = jnp.where(head0, ms[0], ms[1])
            l_b = jnp.where(head0, ls[0], ls[1])
            pv_b = jnp.where(head0, pvs[0], pvs[1])
            if dil == DIL_CONFIGS[0][1]:
                m_ref[rows, :] = m_b
                l_ref[rows, :] = l_b
                acc_ref[rows, :] = pv_b
            else:
                m_old = m_ref[rows, :]
                m_new = jnp.maximum(m_old, m_b)
                w_old = jnp.exp(m_old - m_new)
                w_b = jnp.exp(m_b - m_new)
                m_ref[rows, :] = m_new
                l_ref[rows, :] = w_old * l_ref[rows, :] + w_b * l_b
                acc_ref[rows, :] = w_old * acc_ref[rows, :] + w_b * pv_b
            return 0

        lax.fori_loop(0, seq // blk, body, 0)

    o_ref[...] = (acc_ref[...] / l_ref[...]).astype(o_ref.dtype)


def _dilated_attention(q, k, v, *, blk=128):
    bsz, seq, width = q.shape
    spec = pl.BlockSpec((None, seq, LANES), lambda b, p: (b, 0, p))
    return pl.pallas_call(
        functools.partial(_dil_kernel, blk=blk),
        grid=(bsz, width // LANES),
        in_specs=[spec, spec, spec],
        out_specs=spec,
        out_shape=jax.ShapeDtypeStruct((bsz, seq, width), BF16),
        scratch_shapes=[pltpu.VMEM((seq, LANES), F32)] * 3,
        compiler_params=_params("parallel", "parallel"),
        name="dilated_attention",
    )(q, k, v)


def _out_kernel(h_ref, ya_ref, yb_ref, yc_ref, w_ref, g_ref, b_ref, o_ref, *, alpha, cw, fw):
    y = jnp.dot(ya_ref[...], w_ref[0:cw, :], preferred_element_type=F32)
    y = y + jnp.dot(yb_ref[...], w_ref[cw:cw + fw, :], preferred_element_type=F32)
    y = y + jnp.dot(yc_ref[...], w_ref[cw + fw:, :], preferred_element_type=F32)
    o_ref[...] = _layer_norm(alpha * h_ref[...] + y, g_ref[...], b_ref[...])


def _out_proj_ln(h, ya, yb, yc, w, g, b, *, alpha, tm=512):
    t, d = h.shape
    row = lambda i: (i, 0)
    return pl.pallas_call(
        functools.partial(_out_kernel, alpha=alpha, cw=ya.shape[1], fw=yb.shape[1]),
        grid=(t // tm,),
        in_specs=[pl.BlockSpec((tm, d), row), pl.BlockSpec((tm, ya.shape[1]), row),
                  pl.BlockSpec((tm, yb.shape[1]), row), pl.BlockSpec((tm, yc.shape[1]), row),
                  _resident(w.shape), _resident(g.shape), _resident(b.shape)],
        out_specs=pl.BlockSpec((tm, d), row),
        out_shape=jax.ShapeDtypeStruct((t, d), F32),
        compiler_params=_params("parallel"),
        name="out_proj_ln",
    )(h, ya, yb, yc, w, g, b)


def _rope_tables(seq):
    inv = 1.0 / (ROPE_THETA ** (jnp.arange(0, HEAD_DIM, 2, dtype=F32) / HEAD_DIM))
    ang = jnp.arange(seq, dtype=F32)[:, None] * inv[None, :]
    ang = jnp.concatenate([ang, ang], axis=-1)
    cos, sin = jnp.cos(ang), jnp.sin(ang)
    first_half = (jnp.arange(HEAD_DIM) < HEAD_DIM // 2)[None, :]
    sin_lo = jnp.where(first_half, -sin, 0.0)
    sin_hi = jnp.where(first_half, 0.0, sin)
    rep = lambda a: jnp.tile(a, (1, HEADS_PER_SLAB))
    return rep(cos), rep(sin_lo), rep(sin_hi)


def kernel(x, w_in, w_o, forget_bias, conv_w, conv_b, conv_ln_g, conv_ln_b, ffn_w_in, ffn_w_out, ln_g, ln_b):
    bsz, seq, d = x.shape
    depth = w_in.shape[0]
    cw = conv_w.shape[-1]
    heads_f = forget_bias.shape[-1]
    fw = heads_f * HEAD_DIM
    dw = d - cw - fw
    alpha = (2 * depth) ** 0.25
    t = bsz * seq
    cos, slo, shi = _rope_tables(seq)
    vec = lambda a: a.reshape(1, -1)

    h = x.reshape(t, d)
    for l in range(depth):
        o_f = 2 * cw + 3 * fw
        w_l = w_in[l]
        w_mix = jnp.concatenate(
            [w_l[:, :o_f], jnp.repeat(w_l[:, o_f:o_f + heads_f], HEAD_DIM, axis=1), w_l[:, o_f + heads_f:]],
            axis=1).astype(BF16)
        fb = vec(jnp.repeat(forget_bias[l], HEAD_DIM))
        conv_w_pad = jnp.concatenate([conv_w[l], jnp.zeros((1, cw), F32)], axis=0)

        h = _ffn_ln(h, ffn_w_in[l, 0].astype(BF16), ffn_w_out[l, 0].astype(BF16),
                    vec(ln_g[l, 0]), vec(ln_b[l, 0]), alpha=alpha)
        u, fq, fk, fv, lf, dq, dk, dv = _mixer_proj(h, w_mix, fb, cos, slo, shi,
                                                    seq=seq, cw=cw, fw=fw, dw=dw)
        b3 = lambda a: a.reshape(bsz, seq, a.shape[-1])
        cum_col, cum_row = _forget_cumsum(b3(lf), heads=heads_f)
        ya = _conv_module(b3(u), conv_w_pad, vec(conv_b[l]), vec(conv_ln_g[l]), vec(conv_ln_b[l]))
        yb = _fox_attention(b3(fq), b3(fk), b3(fv), cum_col, cum_row)
        yc = _dilated_attention(b3(dq), b3(dk), b3(dv))
        h = _out_proj_ln(h, ya.reshape(t, cw), yb.reshape(t, fw), yc.reshape(t, dw),
                         w_o[l].astype(BF16), vec(ln_g[l, 1]), vec(ln_b[l, 1]), alpha=alpha)
        h = _ffn_ln(h, ffn_w_in[l, 1].astype(BF16), ffn_w_out[l, 1].astype(BF16),
                    vec(ln_g[l, 2]), vec(ln_b[l, 2]), alpha=alpha)
    return h.reshape(bsz, seq, d)
```

```python
import functools

import jax
import jax.numpy as jnp
from jax import lax
from jax.experimental import pallas as pl
from jax.experimental.pallas import tpu as pltpu

HEAD_DIM = 64
CONV_K = 31
DIL_CONFIGS = ((128, 1), (512, 4), (2048, 16))
ROPE_THETA = 10000.0
LN_EPS = 1e-5
LANES = 128
HEADS_PER_SLAB = LANES // HEAD_DIM
NEG_BIG = -1e30
VMEM_LIMIT = 56 * 1024 * 1024

F32 = jnp.float32
BF16 = jnp.bfloat16


def _layer_norm(y, g, b):
    mu = jnp.mean(y, axis=-1, keepdims=True)
    yc = y - mu
    var = jnp.mean(yc * yc, axis=-1, keepdims=True)
    return yc * lax.rsqrt(var + LN_EPS) * g + b


def _params(*sem):
    return pltpu.CompilerParams(dimension_semantics=sem, vmem_limit_bytes=VMEM_LIMIT)


def _resident(shape):
    nd = len(shape)
    return pl.BlockSpec(shape, lambda *_: (0,) * nd, pipeline_mode=pl.Buffered(1))


def _ffn_kernel(x_ref, win_ref, wout_ref, g_ref, b_ref, o_ref, acc_ref, *, alpha, d_ff, chunk):
    x = x_ref[...]
    xb = x.astype(BF16)
    for c in range(d_ff // chunk):
        lo = c * chunk
        gate = jnp.dot(xb, win_ref[:, lo:lo + chunk], preferred_element_type=F32)
        up = jnp.dot(xb, win_ref[:, d_ff + lo:d_ff + lo + chunk], preferred_element_type=F32)
        hid = (gate * jax.nn.sigmoid(gate) * up).astype(BF16)
        part = jnp.dot(hid, wout_ref[lo:lo + chunk, :], preferred_element_type=F32)
        if c == 0:
            acc_ref[...] = part
        else:
            acc_ref[...] += part
    y = alpha * x + 0.5 * acc_ref[...]
    o_ref[...] = _layer_norm(y, g_ref[...], b_ref[...])


def _ffn_ln(x, w_in, w_out, g, b, *, alpha, tm=512, chunk=256):
    t, d = x.shape
    d_ff = w_out.shape[0]
    return pl.pallas_call(
        functools.partial(_ffn_kernel, alpha=alpha, d_ff=d_ff, chunk=chunk),
        grid=(t // tm,),
        in_specs=[pl.BlockSpec((tm, d), lambda i: (i, 0)),
                  _resident(w_in.shape), _resident(w_out.shape),
                  _resident(g.shape), _resident(b.shape)],
        out_specs=pl.BlockSpec((tm, d), lambda i: (i, 0)),
        out_shape=jax.ShapeDtypeStruct((t, d), F32),
        scratch_shapes=[pltpu.VMEM((tm, d), F32)],
        compiler_params=_params("parallel"),
        name="ffn_ln",
    )(x, w_in, w_out, g, b)


def _rope_slab(t, cos, sin_lo, sin_hi):
    nxt = pltpu.roll(t, LANES - HEAD_DIM // 2, axis=1)
    prv = pltpu.roll(t, HEAD_DIM // 2, axis=1)
    return t * cos + nxt * sin_lo + prv * sin_hi


def _proj_kernel(h_ref, w_ref, fb_ref, cos_ref, slo_ref, shi_ref,
                 u_ref, fq_ref, fk_ref, fv_ref, lf_ref, dq_ref, dk_ref, dv_ref, *, cw, fw, dw):
    hb = h_ref[...].astype(BF16)

    def cols(lo, width):
        return jnp.dot(hb, w_ref[:, lo:lo + width], preferred_element_type=F32)

    scale = HEAD_DIM ** -0.5
    val = cols(0, cw)
    gate = cols(cw, cw)
    u_ref[...] = val * jax.nn.sigmoid(gate)
    base = 2 * cw
    fq_ref[...] = (cols(base, fw) * scale).astype(BF16)
    fk_ref[...] = cols(base + fw, fw).astype(BF16)
    fv_ref[...] = cols(base + 2 * fw, fw).astype(BF16)
    lf_ref[...] = jax.nn.log_sigmoid(cols(base + 3 * fw, fw) + fb_ref[...])
    base = base + 4 * fw
    cos, slo, shi = cos_ref[...], slo_ref[...], shi_ref[...]
    for j in range(dw // LANES):
        lo = j * LANES
        dq_ref[:, lo:lo + LANES] = _rope_slab(cols(base + lo, LANES), cos, slo, shi) * scale
        dk_ref[:, lo:lo + LANES] = _rope_slab(cols(base + dw + lo, LANES), cos, slo, shi)
    dv_ref[...] = cols(base + 2 * dw, dw)


def _mixer_proj(h, w, fb, cos, slo, shi, *, seq, cw, fw, dw, tm=512):
    t, d = h.shape
    nsb = seq // tm
    row = lambda i: (i, 0)
    pos = lambda i: (i % nsb, 0)
    outs = [(cw, F32), (fw, BF16), (fw, BF16), (fw, BF16), (fw, F32), (dw, F32), (dw, F32), (dw, F32)]
    return pl.pallas_call(
        functools.partial(_proj_kernel, cw=cw, fw=fw, dw=dw),
        grid=(t // tm,),
        in_specs=[pl.BlockSpec((tm, d), row), _resident(w.shape), _resident(fb.shape),
                  pl.BlockSpec((tm, LANES), pos), pl.BlockSpec((tm, LANES), pos),
                  pl.BlockSpec((tm, LANES), pos)],
        out_specs=[pl.BlockSpec((tm, n), row) for n, _ in outs],
        out_shape=[jax.ShapeDtypeStruct((t, n), dt) for n, dt in outs],
        compiler_params=_params("parallel"),
        name="mixer_proj",
    )(h, w, fb, cos, slo, shi)


def _split3(x):
    hi = x.astype(BF16).astype(F32)
    mid = (x - hi).astype(BF16).astype(F32)
    return hi, mid, x - hi - mid


def _fox_prep_kernel(lf_ref, q_ref, k_ref, v_ref, qa_ref, ka_ref, va_ref, carry_ref, *, blk):
    @pl.when(pl.program_id(1) == 0)
    def _():
        carry_ref[...] = jnp.zeros_like(carry_ref)

    r = lax.broadcasted_iota(jnp.int32, (blk, blk), 0)
    c = lax.broadcasted_iota(jnp.int32, (blk, blk), 1)
    tri = (c <= r).astype(BF16)
    lane = lax.broadcasted_iota(jnp.int32, (1, LANES), 1)
    is_head = lane < HEAD_DIM
    aug = lane - HEAD_DIM
    heads = q_ref.shape[-1] // HEAD_DIM

    def body(i, carry):
        t0 = pl.multiple_of(i * blk, blk)
        rows = pl.ds(t0, blk)
        parts = _split3(lf_ref[rows, :])
        cum = carry
        for part in parts:
            cum = cum + jnp.dot(tri, part.astype(BF16), preferred_element_type=F32)
        for h in range(heads):
            slab = slice((h // HEADS_PER_SLAB) * LANES, (h // HEADS_PER_SLAB + 1) * LANES)
            odd = h % HEADS_PER_SLAB == 1

            def head_lanes(ref):
                x = ref[rows, slab].astype(F32)
                return pltpu.roll(x, HEAD_DIM, axis=1) if odd else x

            cc = cum[:, slab]
            cc = cc if odd else pltpu.roll(cc, HEAD_DIM, axis=1)
            hi, mid, lo = _split3(cc)
            q_aug = jnp.where(aug == 0, hi, jnp.where(aug == 1, mid, jnp.where(aug == 2, lo,
                              jnp.where(aug < 6, 1.0, 0.0))))
            k_aug = jnp.where(aug < 3, 1.0, jnp.where(aug == 3, -hi, jnp.where(aug == 4, -mid,
                              jnp.where(aug == 5, -lo, 0.0))))
            qa_ref[h, rows, :] = jnp.where(is_head, head_lanes(q_ref), q_aug).astype(BF16)
            ka_ref[h, rows, :] = jnp.where(is_head, head_lanes(k_ref), k_aug).astype(BF16)
            va_ref[h, rows, :] = jnp.where(is_head, head_lanes(v_ref), 1.0).astype(BF16)
        return cum[blk - 1:blk, :]

    carry_ref[...] = lax.fori_loop(0, lf_ref.shape[0] // blk, body, carry_ref[...])


def _fox_prep(lf, q, k, v, *, ts=1024, blk=128):
    bsz, seq, width = lf.shape
    heads = width // HEAD_DIM
    ispec = pl.BlockSpec((None, ts, width), lambda b, s: (b, s, 0))
    ospec = pl.BlockSpec((None, heads, ts, LANES), lambda b, s: (b, 0, s, 0))
    oshape = jax.ShapeDtypeStruct((bsz, heads, seq, LANES), BF16)
    return pl.pallas_call(
        functools.partial(_fox_prep_kernel, blk=blk),
        grid=(bsz, seq // ts),
        in_specs=[ispec] * 4,
        out_specs=[ospec] * 3,
        out_shape=[oshape] * 3,
        scratch_shapes=[pltpu.VMEM((1, width), F32)],
        compiler_params=_params("parallel", "arbitrary"),
        name="fox_prep",
    )(lf, q, k, v)


def _conv_kernel(u_ref, w_ref, cb_ref, g_ref, b_ref, o_ref, pad_ref, *, rows, halo):
    seq, ch = u_ref.shape
    pad_ref[0:halo, :] = jnp.zeros((halo, ch), F32)
    pad_ref[halo:halo + seq, :] = u_ref[...]
    shift = halo - (CONV_K - 1)

    span = rows + halo
    sub = 8

    def body(i, _):
        t0 = pl.multiple_of(i * rows, rows)
        win = pad_ref[pl.ds(t0, span), :]
        acc = jnp.zeros((rows, ch), F32) + cb_ref[...]
        for b in range(sub):
            rolled = win if b == 0 else pltpu.roll(win, span - b, axis=0)
            for a in range(halo // sub + 1):
                k = a * sub + b - shift
                if 0 <= k < CONV_K:
                    acc = acc + w_ref[k:k + 1, :] * rolled[a * sub:a * sub + rows, :]
        y = _layer_norm(acc, g_ref[...], b_ref[...])
        o_ref[pl.ds(t0, rows), :] = (y * jax.nn.sigmoid(y)).astype(o_ref.dtype)
        return 0

    lax.fori_loop(0, seq // rows, body, 0)


def _conv_module(u, w, cb, g, b, *, rows=64, halo=32):
    bsz, seq, ch = u.shape
    return pl.pallas_call(
        functools.partial(_conv_kernel, rows=rows, halo=halo),
        grid=(bsz,),
        in_specs=[pl.BlockSpec((None, seq, ch), lambda i: (i, 0, 0)),
                  _resident(w.shape), _resident(cb.shape), _resident(g.shape), _resident(b.shape)],
        out_specs=pl.BlockSpec((None, seq, ch), lambda i: (i, 0, 0)),
        out_shape=jax.ShapeDtypeStruct((bsz, seq, ch), BF16),
        scratch_shapes=[pltpu.VMEM((halo + seq, ch), F32)],
        compiler_params=_params("parallel"),
        name="conv_module",
    )(u, w, cb, g, b)


def _fox_kernel(q_ref, k_ref, v_ref, o_ref, m_ref, acc_ref, *, tq):
    i = pl.program_id(2)
    rr = lax.broadcasted_iota(jnp.int32, (tq, tq), 0)
    cc = lax.broadcasted_iota(jnp.int32, (tq, tq), 1)
    causal = cc <= rr
    dn = (((1,), (1,)), ((), ()))

    def step(j, masked):
        k0 = pl.multiple_of(j * tq, tq)
        for a in range(HEADS_PER_SLAB):
            s = lax.dot_general(q_ref[a], k_ref[a, pl.ds(k0, tq), :], dn, preferred_element_type=F32)
            if masked:
                s = jnp.where(causal, s, NEG_BIG)
            m_old = m_ref[a]
            m_new = jnp.maximum(m_old, jnp.max(s, axis=-1, keepdims=True))
            alpha = jnp.exp(m_old - m_new)
            p = jnp.exp(s - pltpu.repeat(m_new, tq // LANES, axis=1))
            m_ref[a] = m_new
            acc_ref[a] = alpha * acc_ref[a] + jnp.dot(
                p.astype(BF16), v_ref[a, pl.ds(k0, tq), :], preferred_element_type=F32)

    m_ref[...] = jnp.full(m_ref.shape, NEG_BIG, F32)
    acc_ref[...] = jnp.zeros(acc_ref.shape, F32)
    step(i, True)

    def body(j, _):
        step(j, False)
        return 0

    lax.fori_loop(0, i, body, 0)

    lane = lax.broadcasted_iota(jnp.int32, (1, LANES), 1)
    acc0, acc1 = acc_ref[0], acc_ref[1]
    out = jnp.where(lane < HEAD_DIM,
                    acc0 / pltpu.roll(acc0, HEAD_DIM, axis=1),
                    pltpu.roll(acc1, HEAD_DIM, axis=1) / acc1)
    o_ref[...] = out.astype(o_ref.dtype)


def _fox_attention(qa, ka, va, *, tq=512):
    bsz, heads, seq, _ = qa.shape
    qspec = pl.BlockSpec((None, HEADS_PER_SLAB, tq, LANES), lambda b, p, i: (b, p, i, 0))
    kvspec = pl.BlockSpec((None, HEADS_PER_SLAB, seq, LANES), lambda b, p, i: (b, p, 0, 0))
    return pl.pallas_call(
        functools.partial(_fox_kernel, tq=tq),
        grid=(bsz, heads // HEADS_PER_SLAB, seq // tq),
        in_specs=[qspec, kvspec, kvspec],
        out_specs=pl.BlockSpec((None, tq, LANES), lambda b, p, i: (b, i, p)),
        out_shape=jax.ShapeDtypeStruct((bsz, seq, heads * HEAD_DIM), BF16),
        scratch_shapes=[pltpu.VMEM((HEADS_PER_SLAB, tq, LANES), F32),
                        pltpu.VMEM((HEADS_PER_SLAB, tq, LANES), F32)],
        compiler_params=_params("parallel", "parallel", "arbitrary"),
        name="fox_attention",
    )(qa, ka, va)


def _dil_kernel(q_ref, k_ref, v_ref, o_ref, bias_ref, m_ref, l_ref, acc_ref, *, blk, unroll, rows_out):
    seq = q_ref.shape[0]
    win = 2 * blk
    lane = lax.broadcasted_iota(jnp.int32, (1, LANES), 1)
    head0 = lane < HEAD_DIM
    row = lax.broadcasted_iota(jnp.int32, (win, win), 0) % blk
    dist0 = row - lax.broadcasted_iota(jnp.int32, (win, win), 1)
    bias_ref[0] = jnp.where(dist0 >= 0, 0.0, NEG_BIG)
    bias_ref[1] = jnp.where(jnp.logical_and(dist0 + blk >= 0, dist0 <= 0), 0.0, NEG_BIG)
    ones = jnp.ones((win, LANES), BF16)
    dn = (((1,), (1,)), ((), ()))

    for bi, (_, dil) in enumerate(DIL_CONFIGS):
        nb = seq // (dil * blk)

        def body(idx, _, bi=bi, dil=dil, nb=nb):
            r = idx // nb
            n = idx % nb
            start = r + n * (dil * blk)
            has_prev = (n > 0).astype(jnp.int32)
            wstart = start - has_prev * (dil * blk)
            if dil > 1:
                rows, wrows = pl.ds(start, blk, stride=dil), pl.ds(wstart, win, stride=dil)
            else:
                rows, wrows = pl.ds(start, blk), pl.ds(wstart, win)
            qb = q_ref[rows, :].astype(BF16)
            zero = jnp.zeros_like(qb)
            qs = jnp.concatenate([jnp.where(head0, qb, zero), jnp.where(head0, zero, qb)], axis=0)
            kw = k_ref[wrows, :].astype(BF16)
            vw = jnp.concatenate([v_ref[wrows, :].astype(BF16), ones], axis=1)
            s = lax.dot_general(qs, kw, dn, preferred_element_type=F32) + bias_ref[has_prev]
            m = jnp.max(s, axis=-1, keepdims=True)
            p = jnp.exp(s - m).astype(BF16)
            pv = jnp.dot(p, vw, preferred_element_type=F32)
            mb = jnp.broadcast_to(m, (win, LANES))
            m_ref[bi, rows, :] = jnp.where(head0, mb[:blk], mb[blk:])
            acc_ref[bi, rows, :] = jnp.where(head0, pv[:blk, :LANES], pv[blk:, :LANES])
            l_ref[bi, rows, :] = jnp.where(head0, pv[:blk, LANES:], pv[blk:, LANES:])
            return 0

        lax.fori_loop(0, seq // blk, body, 0, unroll=unroll)

    def merge(i, _):
        rows = pl.ds(pl.multiple_of(i * rows_out, rows_out), rows_out)
        ms = [m_ref[bi, rows, :] for bi in range(len(DIL_CONFIGS))]
        m_all = functools.reduce(jnp.maximum, ms)
        ws = [jnp.exp(m - m_all) for m in ms]
        num = sum(w * acc_ref[bi, rows, :] for bi, w in enumerate(ws))
        den = sum(w * l_ref[bi, rows, :] for bi, w in enumerate(ws))
        o_ref[rows, :] = (num / den).astype(o_ref.dtype)
        return 0

    lax.fori_loop(0, seq // rows_out, merge, 0)


def _dilated_attention(q, k, v, *, blk=128, unroll=2, rows_out=256):
    bsz, seq, width = q.shape
    nbr = len(DIL_CONFIGS)
    spec = pl.BlockSpec((None, seq, LANES), lambda b, p: (b, 0, p))
    return pl.pallas_call(
        functools.partial(_dil_kernel, blk=blk, unroll=unroll, rows_out=rows_out),
        grid=(bsz, width // LANES),
        in_specs=[spec, spec, spec],
        out_specs=spec,
        out_shape=jax.ShapeDtypeStruct((bsz, seq, width), BF16),
        scratch_shapes=[pltpu.VMEM((2, 2 * blk, 2 * blk), F32)] + [pltpu.VMEM((nbr, seq, LANES), F32)] * 3,
        compiler_params=_params("parallel", "parallel"),
        name="dilated_attention",
    )(q, k, v)


def _out_kernel(h_ref, ya_ref, yb_ref, yc_ref, w_ref, g_ref, b_ref, o_ref, *, alpha, cw, fw):
    y = jnp.dot(ya_ref[...], w_ref[0:cw, :], preferred_element_type=F32)
    y = y + jnp.dot(yb_ref[...], w_ref[cw:cw + fw, :], preferred_element_type=F32)
    y = y + jnp.dot(yc_ref[...], w_ref[cw + fw:, :], preferred_element_type=F32)
    o_ref[...] = _layer_norm(alpha * h_ref[...] + y, g_ref[...], b_ref[...])


def _out_proj_ln(h, ya, yb, yc, w, g, b, *, alpha, tm=512):
    t, d = h.shape
    row = lambda i: (i, 0)
    return pl.pallas_call(
        functools.partial(_out_kernel, alpha=alpha, cw=ya.shape[1], fw=yb.shape[1]),
        grid=(t // tm,),
        in_specs=[pl.BlockSpec((tm, d), row), pl.BlockSpec((tm, ya.shape[1]), row),
                  pl.BlockSpec((tm, yb.shape[1]), row), pl.BlockSpec((tm, yc.shape[1]), row),
                  _resident(w.shape), _resident(g.shape), _resident(b.shape)],
        out_specs=pl.BlockSpec((tm, d), row),
        out_shape=jax.ShapeDtypeStruct((t, d), F32),
        compiler_params=_params("parallel"),
        name="out_proj_ln",
    )(h, ya, yb, yc, w, g, b)


def _rope_tables(seq):
    inv = 1.0 / (ROPE_THETA ** (jnp.arange(0, HEAD_DIM, 2, dtype=F32) / HEAD_DIM))
    ang = jnp.arange(seq, dtype=F32)[:, None] * inv[None, :]
    ang = jnp.concatenate([ang, ang], axis=-1)
    cos, sin = jnp.cos(ang), jnp.sin(ang)
    first_half = (jnp.arange(HEAD_DIM) < HEAD_DIM // 2)[None, :]
    sin_lo = jnp.where(first_half, -sin, 0.0)
    sin_hi = jnp.where(first_half, 0.0, sin)
    rep = lambda a: jnp.tile(a, (1, HEADS_PER_SLAB))
    return rep(cos), rep(sin_lo), rep(sin_hi)


def kernel(x, w_in, w_o, forget_bias, conv_w, conv_b, conv_ln_g, conv_ln_b, ffn_w_in, ffn_w_out, ln_g, ln_b):
    bsz, seq, d = x.shape
    depth = w_in.shape[0]
    cw = conv_w.shape[-1]
    heads_f = forget_bias.shape[-1]
    fw = heads_f * HEAD_DIM
    dw = d - cw - fw
    alpha = (2 * depth) ** 0.25
    t = bsz * seq
    cos, slo, shi = _rope_tables(seq)
    vec = lambda a: a.reshape(1, -1)

    h = x.reshape(t, d)
    for l in range(depth):
        o_f = 2 * cw + 3 * fw
        w_l = w_in[l]
        w_mix = jnp.concatenate(
            [w_l[:, :o_f], jnp.repeat(w_l[:, o_f:o_f + heads_f], HEAD_DIM, axis=1), w_l[:, o_f + heads_f:]],
            axis=1).astype(BF16)
        fb = vec(jnp.repeat(forget_bias[l], HEAD_DIM))
        conv_w_pad = jnp.concatenate([conv_w[l], jnp.zeros((1, cw), F32)], axis=0)

        h = _ffn_ln(h, ffn_w_in[l, 0].astype(BF16), ffn_w_out[l, 0].astype(BF16),
                    vec(ln_g[l, 0]), vec(ln_b[l, 0]), alpha=alpha)
        u, fq, fk, fv, lf, dq, dk, dv = _mixer_proj(h, w_mix, fb, cos, slo, shi,
                                                    seq=seq, cw=cw, fw=fw, dw=dw)
        b3 = lambda a: a.reshape(bsz, seq, a.shape[-1])
        qa, ka, va = _fox_prep(b3(lf), b3(fq), b3(fk), b3(fv))
        ya = _conv_module(b3(u), conv_w_pad, vec(conv_b[l]), vec(conv_ln_g[l]), vec(conv_ln_b[l]))
        yb = _fox_attention(qa, ka, va)
        yc = _dilated_attention(b3(dq), b3(dk), b3(dv))
        h = _out_proj_ln(h, ya.reshape(t, cw), yb.reshape(t, fw), yc.reshape(t, dw),
                         w_o[l].astype(BF16), vec(ln_g[l, 1]), vec(ln_b[l, 1]), alpha=alpha)
        h = _ffn_ln(h, ffn_w_in[l, 1].astype(BF16), ffn_w_out[l, 1].astype(BF16),
                    vec(ln_g[l, 2]), vec(ln_b[l, 2]), alpha=alpha)
    return h.reshape(bsz, seq, d)
```

```python
import functools

import jax
import jax.numpy as jnp
from jax import lax
from jax.experimental import pallas as pl
from jax.experimental.pallas import tpu as pltpu

HEAD_DIM = 64
CONV_K = 31
DIL_CONFIGS = ((128, 1), (512, 4), (2048, 16))
ROPE_THETA = 10000.0
LN_EPS = 1e-5
LANES = 128
HEADS_PER_SLAB = LANES // HEAD_DIM
NEG_BIG = -1e30
VMEM_LIMIT = 56 * 1024 * 1024

F32 = jnp.float32
BF16 = jnp.bfloat16


def _layer_norm(y, g, b):
    mu = jnp.mean(y, axis=-1, keepdims=True)
    yc = y - mu
    var = jnp.mean(yc * yc, axis=-1, keepdims=True)
    return yc * lax.rsqrt(var + LN_EPS) * g + b


def _params(*sem):
    return pltpu.CompilerParams(dimension_semantics=sem, vmem_limit_bytes=VMEM_LIMIT)


def _resident(shape):
    nd = len(shape)
    return pl.BlockSpec(shape, lambda *_: (0,) * nd, pipeline_mode=pl.Buffered(1))


def _ffn_kernel(x_ref, win_ref, wout_ref, g_ref, b_ref, o_ref, acc_ref, *, alpha, d_ff, chunk):
    x = x_ref[...]
    xb = x.astype(BF16)
    for c in range(d_ff // chunk):
        lo = c * chunk
        gate = jnp.dot(xb, win_ref[:, lo:lo + chunk], preferred_element_type=F32)
        up = jnp.dot(xb, win_ref[:, d_ff + lo:d_ff + lo + chunk], preferred_element_type=F32)
        hid = (gate * jax.nn.sigmoid(gate) * up).astype(BF16)
        part = jnp.dot(hid, wout_ref[lo:lo + chunk, :], preferred_element_type=F32)
        if c == 0:
            acc_ref[...] = part
        else:
            acc_ref[...] += part
    y = alpha * x + 0.5 * acc_ref[...]
    o_ref[...] = _layer_norm(y, g_ref[...], b_ref[...])


def _ffn_ln(x, w_in, w_out, g, b, *, alpha, tm=512, chunk=256):
    t, d = x.shape
    d_ff = w_out.shape[0]
    return pl.pallas_call(
        functools.partial(_ffn_kernel, alpha=alpha, d_ff=d_ff, chunk=chunk),
        grid=(t // tm,),
        in_specs=[pl.BlockSpec((tm, d), lambda i: (i, 0)),
                  _resident(w_in.shape), _resident(w_out.shape),
                  _resident(g.shape), _resident(b.shape)],
        out_specs=pl.BlockSpec((tm, d), lambda i: (i, 0)),
        out_shape=jax.ShapeDtypeStruct((t, d), F32),
        scratch_shapes=[pltpu.VMEM((tm, d), F32)],
        compiler_params=_params("parallel"),
        name="ffn_ln",
    )(x, w_in, w_out, g, b)


def _rope_slab(t, cos, sin_lo, sin_hi):
    nxt = pltpu.roll(t, LANES - HEAD_DIM // 2, axis=1)
    prv = pltpu.roll(t, HEAD_DIM // 2, axis=1)
    return t * cos + nxt * sin_lo + prv * sin_hi


def _proj_kernel(h_ref, w_ref, fb_ref, cos_ref, slo_ref, shi_ref,
                 u_ref, fq_ref, fk_ref, fv_ref, lf_ref, dq_ref, dk_ref, dv_ref, *, cw, fw, dw):
    hb = h_ref[...].astype(BF16)

    def cols(lo, width):
        return jnp.dot(hb, w_ref[:, lo:lo + width], preferred_element_type=F32)

    scale = HEAD_DIM ** -0.5
    val = cols(0, cw)
    gate = cols(cw, cw)
    u_ref[...] = val * jax.nn.sigmoid(gate)
    base = 2 * cw
    fq_ref[...] = (cols(base, fw) * scale).astype(BF16)
    fk_ref[...] = cols(base + fw, fw).astype(BF16)
    fv_ref[...] = cols(base + 2 * fw, fw).astype(BF16)
    lf_ref[...] = jax.nn.log_sigmoid(cols(base + 3 * fw, fw) + fb_ref[...])
    base = base + 4 * fw
    cos, slo, shi = cos_ref[...], slo_ref[...], shi_ref[...]
    for j in range(dw // LANES):
        lo = j * LANES
        dq_ref[:, lo:lo + LANES] = _rope_slab(cols(base + lo, LANES), cos, slo, shi) * scale
        dk_ref[:, lo:lo + LANES] = _rope_slab(cols(base + dw + lo, LANES), cos, slo, shi)
    dv_ref[...] = cols(base + 2 * dw, dw)


def _mixer_proj(h, w, fb, cos, slo, shi, *, seq, cw, fw, dw, tm=512):
    t, d = h.shape
    nsb = seq // tm
    row = lambda i: (i, 0)
    pos = lambda i: (i % nsb, 0)
    outs = [(cw, F32), (fw, BF16), (fw, BF16), (fw, BF16), (fw, F32), (dw, F32), (dw, F32), (dw, F32)]
    return pl.pallas_call(
        functools.partial(_proj_kernel, cw=cw, fw=fw, dw=dw),
        grid=(t // tm,),
        in_specs=[pl.BlockSpec((tm, d), row), _resident(w.shape), _resident(fb.shape),
                  pl.BlockSpec((tm, LANES), pos), pl.BlockSpec((tm, LANES), pos),
                  pl.BlockSpec((tm, LANES), pos)],
        out_specs=[pl.BlockSpec((tm, n), row) for n, _ in outs],
        out_shape=[jax.ShapeDtypeStruct((t, n), dt) for n, dt in outs],
        compiler_params=_params("parallel"),
        name="mixer_proj",
    )(h, w, fb, cos, slo, shi)


def _split3(x):
    hi = x.astype(BF16).astype(F32)
    mid = (x - hi).astype(BF16).astype(F32)
    return hi, mid, x - hi - mid


def _fox_prep_kernel(lf_ref, q_ref, k_ref, v_ref, qa_ref, ka_ref, va_ref, carry_ref, *, blk):
    @pl.when(pl.program_id(1) == 0)
    def _():
        carry_ref[...] = jnp.zeros_like(carry_ref)

    r = lax.broadcasted_iota(jnp.int32, (blk, blk), 0)
    c = lax.broadcasted_iota(jnp.int32, (blk, blk), 1)
    tri = (c <= r).astype(BF16)
    lane = lax.broadcasted_iota(jnp.int32, (1, LANES), 1)
    is_head = lane < HEAD_DIM
    aug = lane - HEAD_DIM
    heads = q_ref.shape[-1] // HEAD_DIM

    def body(i, carry):
        t0 = pl.multiple_of(i * blk, blk)
        rows = pl.ds(t0, blk)
        parts = _split3(lf_ref[rows, :])
        cum = carry
        for part in parts:
            cum = cum + jnp.dot(tri, part.astype(BF16), preferred_element_type=F32)
        for h in range(heads):
            slab = slice((h // HEADS_PER_SLAB) * LANES, (h // HEADS_PER_SLAB + 1) * LANES)
            odd = h % HEADS_PER_SLAB == 1

            def head_lanes(ref):
                x = ref[rows, slab].astype(F32)
                return pltpu.roll(x, HEAD_DIM, axis=1) if odd else x

            cc = cum[:, slab]
            cc = cc if odd else pltpu.roll(cc, HEAD_DIM, axis=1)
            hi, mid, lo = _split3(cc)
            q_aug = jnp.where(aug == 0, hi, jnp.where(aug == 1, mid, jnp.where(aug == 2, lo,
                              jnp.where(aug < 6, 1.0, 0.0))))
            k_aug = jnp.where(aug < 3, 1.0, jnp.where(aug == 3, -hi, jnp.where(aug == 4, -mid,
                              jnp.where(aug == 5, -lo, 0.0))))
            qa_ref[h, rows, :] = jnp.where(is_head, head_lanes(q_ref), q_aug).astype(BF16)
            ka_ref[h, rows, :] = jnp.where(is_head, head_lanes(k_ref), k_aug).astype(BF16)
            va_ref[h, rows, :] = jnp.where(is_head, head_lanes(v_ref), 1.0).astype(BF16)
        return cum[blk - 1:blk, :]

    carry_ref[...] = lax.fori_loop(0, lf_ref.shape[0] // blk, body, carry_ref[...])


def _fox_prep(lf, q, k, v, *, ts=1024, blk=128):
    bsz, seq, width = lf.shape
    heads = width // HEAD_DIM
    ispec = pl.BlockSpec((None, ts, width), lambda b, s: (b, s, 0))
    ospec = pl.BlockSpec((None, heads, ts, LANES), lambda b, s: (b, 0, s, 0))
    oshape = jax.ShapeDtypeStruct((bsz, heads, seq, LANES), BF16)
    return pl.pallas_call(
        functools.partial(_fox_prep_kernel, blk=blk),
        grid=(bsz, seq // ts),
        in_specs=[ispec] * 4,
        out_specs=[ospec] * 3,
        out_shape=[oshape] * 3,
        scratch_shapes=[pltpu.VMEM((1, width), F32)],
        compiler_params=_params("parallel", "arbitrary"),
        name="fox_prep",
    )(lf, q, k, v)


def _conv_kernel(u_ref, w_ref, cb_ref, g_ref, b_ref, o_ref, pad_ref, *, rows, halo):
    seq, ch = u_ref.shape
    pad_ref[0:halo, :] = jnp.zeros((halo, ch), F32)
    pad_ref[halo:halo + seq, :] = u_ref[...]
    shift = halo - (CONV_K - 1)

    span = rows + halo
    sub = 8

    def body(i, _):
        t0 = pl.multiple_of(i * rows, rows)
        win = pad_ref[pl.ds(t0, span), :]
        acc = jnp.zeros((rows, ch), F32) + cb_ref[...]
        for b in range(sub):
            rolled = win if b == 0 else pltpu.roll(win, span - b, axis=0)
            for a in range(halo // sub + 1):
                k = a * sub + b - shift
                if 0 <= k < CONV_K:
                    acc = acc + w_ref[k:k + 1, :] * rolled[a * sub:a * sub + rows, :]
        y = _layer_norm(acc, g_ref[...], b_ref[...])
        o_ref[pl.ds(t0, rows), :] = (y * jax.nn.sigmoid(y)).astype(o_ref.dtype)
        return 0

    lax.fori_loop(0, seq // rows, body, 0)


def _conv_module(u, w, cb, g, b, *, rows=64, halo=32):
    bsz, seq, ch = u.shape
    return pl.pallas_call(
        functools.partial(_conv_kernel, rows=rows, halo=halo),
        grid=(bsz,),
        in_specs=[pl.BlockSpec((None, seq, ch), lambda i: (i, 0, 0)),
                  _resident(w.shape), _resident(cb.shape), _resident(g.shape), _resident(b.shape)],
        out_specs=pl.BlockSpec((None, seq, ch), lambda i: (i, 0, 0)),
        out_shape=jax.ShapeDtypeStruct((bsz, seq, ch), BF16),
        scratch_shapes=[pltpu.VMEM((halo + seq, ch), F32)],
        compiler_params=_params("parallel"),
        name="conv_module",
    )(u, w, cb, g, b)


def _fox_kernel(q_ref, k_ref, v_ref, o_ref, m_ref, acc_ref, *, tq):
    i = pl.program_id(2)
    rr = lax.broadcasted_iota(jnp.int32, (tq, tq), 0)
    cc = lax.broadcasted_iota(jnp.int32, (tq, tq), 1)
    causal = cc <= rr
    dn = (((1,), (1,)), ((), ()))

    def step(j, masked):
        k0 = pl.multiple_of(j * tq, tq)
        for a in range(HEADS_PER_SLAB):
            s = lax.dot_general(q_ref[a], k_ref[a, pl.ds(k0, tq), :], dn, preferred_element_type=F32)
            if masked:
                s = jnp.where(causal, s, NEG_BIG)
            m_old = m_ref[a]
            m_new = jnp.maximum(m_old, jnp.max(s, axis=-1, keepdims=True))
            alpha = jnp.exp(m_old - m_new)
            p = jnp.exp(s - pltpu.repeat(m_new, tq // LANES, axis=1))
            m_ref[a] = m_new
            acc_ref[a] = alpha * acc_ref[a] + jnp.dot(
                p.astype(BF16), v_ref[a, pl.ds(k0, tq), :], preferred_element_type=F32)

    m_ref[...] = jnp.full(m_ref.shape, NEG_BIG, F32)
    acc_ref[...] = jnp.zeros(acc_ref.shape, F32)
    step(i, True)

    def body(j, _):
        step(j, False)
        return 0

    lax.fori_loop(0, i, body, 0)

    lane = lax.broadcasted_iota(jnp.int32, (1, LANES), 1)
    acc0, acc1 = acc_ref[0], acc_ref[1]
    out = jnp.where(lane < HEAD_DIM,
                    acc0 / pltpu.roll(acc0, HEAD_DIM, axis=1),
                    pltpu.roll(acc1, HEAD_DIM, axis=1) / acc1)
    o_ref[...] = out.astype(o_ref.dtype)


def _fox_attention(qa, ka, va, *, tq=512):
    bsz, heads, seq, _ = qa.shape
    qspec = pl.BlockSpec((None, HEADS_PER_SLAB, tq, LANES), lambda b, p, i: (b, p, i, 0))
    kvspec = pl.BlockSpec((None, HEADS_PER_SLAB, seq, LANES), lambda b, p, i: (b, p, 0, 0))
    return pl.pallas_call(
        functools.partial(_fox_kernel, tq=tq),
        grid=(bsz, heads // HEADS_PER_SLAB, seq // tq),
        in_specs=[qspec, kvspec, kvspec],
        out_specs=pl.BlockSpec((None, tq, LANES), lambda b, p, i: (b, i, p)),
        out_shape=jax.ShapeDtypeStruct((bsz, seq, heads * HEAD_DIM), BF16),
        scratch_shapes=[pltpu.VMEM((HEADS_PER_SLAB, tq, LANES), F32),
                        pltpu.VMEM((HEADS_PER_SLAB, tq, LANES), F32)],
        compiler_params=_params("parallel", "parallel", "arbitrary"),
        name="fox_attention",
    )(qa, ka, va)


def _dil_kernel(q_ref, k_ref, v_ref, o_ref, bias_ref, bias2_ref, m_ref, l_ref, acc_ref, *,
                blk, unroll, rows_out):
    seq = q_ref.shape[0]
    win = 2 * blk
    lane = lax.broadcasted_iota(jnp.int32, (1, LANES), 1)
    head0 = lane < HEAD_DIM
    dist0 = (lax.broadcasted_iota(jnp.int32, (win, win), 0) % blk
             - lax.broadcasted_iota(jnp.int32, (win, win), 1))
    row = lax.broadcasted_iota(jnp.int32, (2 * win, win), 0)
    col = lax.broadcasted_iota(jnp.int32, (2 * win, win), 1)
    bias_ref[0] = jnp.where(dist0 >= 0, 0.0, NEG_BIG)
    bias_ref[1] = jnp.where(jnp.logical_and(dist0 + blk >= 0, dist0 <= 0), 0.0, NEG_BIG)
    dist_all = (row % win) - col
    bias2_ref[...] = jnp.where(jnp.logical_and(dist_all >= 0, dist_all <= blk), 0.0, NEG_BIG)
    ones = jnp.ones((win, LANES), BF16)
    dn = (((1,), (1,)), ((), ()))

    def attend(bi, rows, wrows, bias):
        nq = rows.size
        qb = q_ref[rows, :].astype(BF16)
        zero = jnp.zeros_like(qb)
        qs = jnp.concatenate([jnp.where(head0, qb, zero), jnp.where(head0, zero, qb)], axis=0)
        kw = k_ref[wrows, :].astype(BF16)
        vw = jnp.concatenate([v_ref[wrows, :].astype(BF16), ones], axis=1)
        s = lax.dot_general(qs, kw, dn, preferred_element_type=F32) + bias
        m = jnp.max(s, axis=-1, keepdims=True)
        p = jnp.exp(s - m).astype(BF16)
        pv = jnp.dot(p, vw, preferred_element_type=F32)
        mb = jnp.broadcast_to(m, (2 * nq, LANES))
        m_ref[bi, rows, :] = jnp.where(head0, mb[:nq], mb[nq:])
        acc_ref[bi, rows, :] = jnp.where(head0, pv[:nq, :LANES], pv[nq:, :LANES])
        l_ref[bi, rows, :] = jnp.where(head0, pv[:nq, LANES:], pv[nq:, LANES:])

    for bi, (_, dil) in enumerate(DIL_CONFIGS):
        nb = seq // (dil * blk)
        ds = (lambda start, size, dil=dil: pl.ds(start, size, stride=dil) if dil > 1 else pl.ds(start, size))

        if nb == 2:
            def body(r, _, bi=bi, ds=ds):
                attend(bi, ds(r, win), ds(r, win), bias2_ref[...])
                return 0

            lax.fori_loop(0, dil, body, 0, unroll=unroll // 2)
            continue

        def body(idx, _, bi=bi, dil=dil, nb=nb, ds=ds):
            r = idx // nb
            n = idx % nb
            start = r + n * (dil * blk)
            has_prev = (n > 0).astype(jnp.int32)
            attend(bi, ds(start, blk), ds(start - has_prev * (dil * blk), win), bias_ref[has_prev])
            return 0

        lax.fori_loop(0, seq // blk, body, 0, unroll=unroll)

    def merge(i, _):
        rows = pl.ds(pl.multiple_of(i * rows_out, rows_out), rows_out)
        ms = [m_ref[bi, rows, :] for bi in range(len(DIL_CONFIGS))]
        m_all = functools.reduce(jnp.maximum, ms)
        ws = [jnp.exp(m - m_all) for m in ms]
        num = sum(w * acc_ref[bi, rows, :] for bi, w in enumerate(ws))
        den = sum(w * l_ref[bi, rows, :] for bi, w in enumerate(ws))
        o_ref[rows, :] = (num / den).astype(o_ref.dtype)
        return 0

    lax.fori_loop(0, seq // rows_out, merge, 0)


def _dilated_attention(q, k, v, *, blk=128, unroll=8, rows_out=256):
    bsz, seq, width = q.shape
    nbr = len(DIL_CONFIGS)
    spec = pl.BlockSpec((None, seq, LANES), lambda b, p: (b, 0, p))
    return pl.pallas_call(
        functools.partial(_dil_kernel, blk=blk, unroll=unroll, rows_out=rows_out),
        grid=(bsz, width // LANES),
        in_specs=[spec, spec, spec],
        out_specs=spec,
        out_shape=jax.ShapeDtypeStruct((bsz, seq, width), BF16),
        scratch_shapes=[pltpu.VMEM((2, 2 * blk, 2 * blk), F32), pltpu.VMEM((4 * blk, 2 * blk), F32)]
        + [pltpu.VMEM((nbr, seq, LANES), F32)] * 3,
        compiler_params=_params("parallel", "parallel"),
        name="dilated_attention",
    )(q, k, v)


def _out_kernel(h_ref, ya_ref, yb_ref, yc_ref, w_ref, g_ref, b_ref, o_ref, *, alpha, cw, fw):
    y = jnp.dot(ya_ref[...], w_ref[0:cw, :], preferred_element_type=F32)
    y = y + jnp.dot(yb_ref[...], w_ref[cw:cw + fw, :], preferred_element_type=F32)
    y = y + jnp.dot(yc_ref[...], w_ref[cw + fw:, :], preferred_element_type=F32)
    o_ref[...] = _layer_norm(alpha * h_ref[...] + y, g_ref[...], b_ref[...])


def _out_proj_ln(h, ya, yb, yc, w, g, b, *, alpha, tm=512):
    t, d = h.shape
    row = lambda i: (i, 0)
    return pl.pallas_call(
        functools.partial(_out_kernel, alpha=alpha, cw=ya.shape[1], fw=yb.shape[1]),
        grid=(t // tm,),
        in_specs=[pl.BlockSpec((tm, d), row), pl.BlockSpec((tm, ya.shape[1]), row),
                  pl.BlockSpec((tm, yb.shape[1]), row), pl.BlockSpec((tm, yc.shape[1]), row),
                  _resident(w.shape), _resident(g.shape), _resident(b.shape)],
        out_specs=pl.BlockSpec((tm, d), row),
        out_shape=jax.ShapeDtypeStruct((t, d), F32),
        compiler_params=_params("parallel"),
        name="out_proj_ln",
    )(h, ya, yb, yc, w, g, b)


def _rope_tables(seq):
    inv = 1.0 / (ROPE_THETA ** (jnp.arange(0, HEAD_DIM, 2, dtype=F32) / HEAD_DIM))
    ang = jnp.arange(seq, dtype=F32)[:, None] * inv[None, :]
    ang = jnp.concatenate([ang, ang], axis=-1)
    cos, sin = jnp.cos(ang), jnp.sin(ang)
    first_half = (jnp.arange(HEAD_DIM) < HEAD_DIM // 2)[None, :]
    sin_lo = jnp.where(first_half, -sin, 0.0)
    sin_hi = jnp.where(first_half, 0.0, sin)
    rep = lambda a: jnp.tile(a, (1, HEADS_PER_SLAB))
    return rep(cos), rep(sin_lo), rep(sin_hi)


def kernel(x, w_in, w_o, forget_bias, conv_w, conv_b, conv_ln_g, conv_ln_b, ffn_w_in, ffn_w_out, ln_g, ln_b):
    bsz, seq, d = x.shape
    depth = w_in.shape[0]
    cw = conv_w.shape[-1]
    heads_f = forget_bias.shape[-1]
    fw = heads_f * HEAD_DIM
    dw = d - cw - fw
    alpha = (2 * depth) ** 0.25
    t = bsz * seq
    cos, slo, shi = _rope_tables(seq)
    vec = lambda a: a.reshape(1, -1)

    h = x.reshape(t, d)
    for l in range(depth):
        o_f = 2 * cw + 3 * fw
        w_l = w_in[l]
        w_mix = jnp.concatenate(
            [w_l[:, :o_f], jnp.repeat(w_l[:, o_f:o_f + heads_f], HEAD_DIM, axis=1), w_l[:, o_f + heads_f:]],
            axis=1).astype(BF16)
        fb = vec(jnp.repeat(forget_bias[l], HEAD_DIM))
        conv_w_pad = jnp.concatenate([conv_w[l], jnp.zeros((1, cw), F32)], axis=0)

        h = _ffn_ln(h, ffn_w_in[l, 0].astype(BF16), ffn_w_out[l, 0].astype(BF16),
                    vec(ln_g[l, 0]), vec(ln_b[l, 0]), alpha=alpha)
        u, fq, fk, fv, lf, dq, dk, dv = _mixer_proj(h, w_mix, fb, cos, slo, shi,
                                                    seq=seq, cw=cw, fw=fw, dw=dw)
        b3 = lambda a: a.reshape(bsz, seq, a.shape[-1])
        qa, ka, va = _fox_prep(b3(lf), b3(fq), b3(fk), b3(fv))
        ya = _conv_module(b3(u), conv_w_pad, vec(conv_b[l]), vec(conv_ln_g[l]), vec(conv_ln_b[l]))
        yb = _fox_attention(qa, ka, va)
        yc = _dilated_attention(b3(dq), b3(dk), b3(dv))
        h = _out_proj_ln(h, ya.reshape(t, cw), yb.reshape(t, fw), yc.reshape(t, dw),
                         w_o[l].astype(BF16), vec(ln_g[l, 1]), vec(ln_b[l, 1]), alpha=alpha)
        h = _ffn_ln(h, ffn_w_in[l, 1].astype(BF16), ffn_w_out[l, 1].astype(BF16),
                    vec(ln_g[l, 2]), vec(ln_b[l, 2]), alpha=alpha)
    return h.reshape(bsz, seq, d)
```

```python
import functools

import jax
import jax.numpy as jnp
from jax import lax
from jax.experimental import pallas as pl
from jax.experimental.pallas import tpu as pltpu

HEAD_DIM = 64
CONV_K = 31
DIL_CONFIGS = ((128, 1), (512, 4), (2048, 16))
ROPE_THETA = 10000.0
LN_EPS = 1e-5
LANES = 128
HEADS_PER_SLAB = LANES // HEAD_DIM
NEG_BIG = -1e30
VMEM_LIMIT = 56 * 1024 * 1024

F32 = jnp.float32
BF16 = jnp.bfloat16


def _layer_norm(y, g, b):
    mu = jnp.mean(y, axis=-1, keepdims=True)
    yc = y - mu
    var = jnp.mean(yc * yc, axis=-1, keepdims=True)
    return yc * lax.rsqrt(var + LN_EPS) * g + b


def _params(*sem):
    return pltpu.CompilerParams(dimension_semantics=sem, vmem_limit_bytes=VMEM_LIMIT)


def _resident(arr, *lead):
    rest = arr.shape[len(lead):]
    index = tuple(lead) + (0,) * len(rest)
    return pl.BlockSpec((None,) * len(lead) + rest, lambda *_: index, pipeline_mode=pl.Buffered(1))


def _ffn_kernel(x_ref, win_ref, wout_ref, g_ref, b_ref, o_ref, acc_ref, *, alpha, d_ff, chunk):
    x = x_ref[...]
    xb = x.astype(BF16)
    for c in range(d_ff // chunk):
        lo = c * chunk
        gate = jnp.dot(xb, win_ref[:, lo:lo + chunk], preferred_element_type=F32)
        up = jnp.dot(xb, win_ref[:, d_ff + lo:d_ff + lo + chunk], preferred_element_type=F32)
        hid = (gate * jax.nn.sigmoid(gate) * up).astype(BF16)
        part = jnp.dot(hid, wout_ref[lo:lo + chunk, :], preferred_element_type=F32)
        if c == 0:
            acc_ref[...] = part
        else:
            acc_ref[...] += part
    y = alpha * x + 0.5 * acc_ref[...]
    o_ref[...] = _layer_norm(y, g_ref[...], b_ref[...])


def _ffn_ln(x, w_in, w_out, g, b, l, j, n, *, alpha, tm=512, chunk=256):
    t, d = x.shape
    d_ff = w_out.shape[-2]
    return pl.pallas_call(
        functools.partial(_ffn_kernel, alpha=alpha, d_ff=d_ff, chunk=chunk),
        grid=(t // tm,),
        in_specs=[pl.BlockSpec((tm, d), lambda i: (i, 0)),
                  _resident(w_in, l, j), _resident(w_out, l, j),
                  _resident(g, l, n), _resident(b, l, n)],
        out_specs=pl.BlockSpec((tm, d), lambda i: (i, 0)),
        out_shape=jax.ShapeDtypeStruct((t, d), F32),
        scratch_shapes=[pltpu.VMEM((tm, d), F32)],
        compiler_params=_params("parallel"),
        name="ffn_ln",
    )(x, w_in, w_out, g, b)


def _rope_slab(t, cos, sin_lo, sin_hi):
    nxt = pltpu.roll(t, LANES - HEAD_DIM // 2, axis=1)
    prv = pltpu.roll(t, HEAD_DIM // 2, axis=1)
    return t * cos + nxt * sin_lo + prv * sin_hi


def _proj_kernel(h_ref, w_ref, fb_ref, cos_ref, slo_ref, shi_ref,
                 u_ref, fq_ref, fk_ref, fv_ref, lf_ref, dq_ref, dk_ref, dv_ref, *, cw, fw, dw):
    hb = h_ref[...].astype(BF16)

    def cols(lo, width):
        return jnp.dot(hb, w_ref[:, lo:lo + width], preferred_element_type=F32)

    scale = HEAD_DIM ** -0.5
    val = cols(0, cw)
    gate = cols(cw, cw)
    u_ref[...] = val * jax.nn.sigmoid(gate)
    base = 2 * cw
    fq_ref[...] = (cols(base, fw) * scale).astype(BF16)
    fk_ref[...] = cols(base + fw, fw).astype(BF16)
    fv_ref[...] = cols(base + 2 * fw, fw).astype(BF16)
    lf_ref[...] = jax.nn.log_sigmoid(cols(base + 3 * fw, fw) + fb_ref[...])
    base = base + 4 * fw
    cos, slo, shi = cos_ref[...], slo_ref[...], shi_ref[...]
    for j in range(dw // LANES):
        lo = j * LANES
        dq_ref[:, lo:lo + LANES] = _rope_slab(cols(base + lo, LANES), cos, slo, shi) * scale
        dk_ref[:, lo:lo + LANES] = _rope_slab(cols(base + dw + lo, LANES), cos, slo, shi)
    dv_ref[...] = cols(base + 2 * dw, dw)


def _mixer_proj(h, w, fb, cos, slo, shi, l, *, seq, cw, fw, dw, tm=512):
    t, d = h.shape
    nsb = seq // tm
    row = lambda i: (i, 0)
    pos = lambda i: (i % nsb, 0)
    outs = [(cw, F32), (fw, BF16), (fw, BF16), (fw, BF16), (fw, F32), (dw, F32), (dw, F32), (dw, F32)]
    return pl.pallas_call(
        functools.partial(_proj_kernel, cw=cw, fw=fw, dw=dw),
        grid=(t // tm,),
        in_specs=[pl.BlockSpec((tm, d), row), _resident(w, l), _resident(fb, l),
                  pl.BlockSpec((tm, LANES), pos), pl.BlockSpec((tm, LANES), pos),
                  pl.BlockSpec((tm, LANES), pos)],
        out_specs=[pl.BlockSpec((tm, n), row) for n, _ in outs],
        out_shape=[jax.ShapeDtypeStruct((t, n), dt) for n, dt in outs],
        compiler_params=_params("parallel"),
        name="mixer_proj",
    )(h, w, fb, cos, slo, shi)


def _split3(x):
    hi = x.astype(BF16).astype(F32)
    mid = (x - hi).astype(BF16).astype(F32)
    return hi, mid, x - hi - mid


def _fox_prep_kernel(lf_ref, q_ref, k_ref, v_ref, qa_ref, ka_ref, va_ref, carry_ref, *, blk):
    @pl.when(pl.program_id(1) == 0)
    def _():
        carry_ref[...] = jnp.zeros_like(carry_ref)

    r = lax.broadcasted_iota(jnp.int32, (blk, blk), 0)
    c = lax.broadcasted_iota(jnp.int32, (blk, blk), 1)
    tri = (c <= r).astype(BF16)
    lane = lax.broadcasted_iota(jnp.int32, (1, LANES), 1)
    is_head = lane < HEAD_DIM
    aug = lane - HEAD_DIM
    heads = q_ref.shape[-1] // HEAD_DIM

    def body(i, carry):
        t0 = pl.multiple_of(i * blk, blk)
        rows = pl.ds(t0, blk)
        parts = _split3(lf_ref[rows, :])
        cum = carry
        for part in parts:
            cum = cum + jnp.dot(tri, part.astype(BF16), preferred_element_type=F32)
        for h in range(heads):
            slab = slice((h // HEADS_PER_SLAB) * LANES, (h // HEADS_PER_SLAB + 1) * LANES)
            odd = h % HEADS_PER_SLAB == 1

            def head_lanes(ref):
                x = ref[rows, slab].astype(F32)
                return pltpu.roll(x, HEAD_DIM, axis=1) if odd else x

            cc = cum[:, slab]
            cc = cc if odd else pltpu.roll(cc, HEAD_DIM, axis=1)
            hi, mid, lo = _split3(cc)
            q_aug = jnp.where(aug == 0, hi, jnp.where(aug == 1, mid, jnp.where(aug == 2, lo,
                              jnp.where(aug < 6, 1.0, 0.0))))
            k_aug = jnp.where(aug < 3, 1.0, jnp.where(aug == 3, -hi, jnp.where(aug == 4, -mid,
                              jnp.where(aug == 5, -lo, 0.0))))
            qa_ref[h, rows, :] = jnp.where(is_head, head_lanes(q_ref), q_aug).astype(BF16)
            ka_ref[h, rows, :] = jnp.where(is_head, head_lanes(k_ref), k_aug).astype(BF16)
            va_ref[h, rows, :] = jnp.where(is_head, head_lanes(v_ref), 1.0).astype(BF16)
        return cum[blk - 1:blk, :]

    carry_ref[...] = lax.fori_loop(0, lf_ref.shape[0] // blk, body, carry_ref[...])


def _fox_prep(lf, q, k, v, *, ts=1024, blk=128):
    bsz, seq, width = lf.shape
    heads = width // HEAD_DIM
    ispec = pl.BlockSpec((None, ts, width), lambda b, s: (b, s, 0))
    ospec = pl.BlockSpec((None, heads, ts, LANES), lambda b, s: (b, 0, s, 0))
    oshape = jax.ShapeDtypeStruct((bsz, heads, seq, LANES), BF16)
    return pl.pallas_call(
        functools.partial(_fox_prep_kernel, blk=blk),
        grid=(bsz, seq // ts),
        in_specs=[ispec] * 4,
        out_specs=[ospec] * 3,
        out_shape=[oshape] * 3,
        scratch_shapes=[pltpu.VMEM((1, width), F32)],
        compiler_params=_params("parallel", "arbitrary"),
        name="fox_prep",
    )(lf, q, k, v)


def _conv_kernel(u_ref, w_ref, cb_ref, g_ref, b_ref, o_ref, pad_ref, *, rows, halo):
    seq, ch = u_ref.shape
    pad_ref[0:halo, :] = jnp.zeros((halo, ch), F32)
    pad_ref[halo:halo + seq, :] = u_ref[...]
    shift = halo - (CONV_K - 1)

    span = rows + halo
    sub = 8

    slabs = [slice(c0, c0 + LANES) for c0 in range(0, ch, LANES)]

    def body(i, _):
        t0 = pl.multiple_of(i * rows, rows)
        accs = []
        for cs in slabs:
            win = pad_ref[pl.ds(t0, span), cs]
            parts = []
            for b in range(sub):
                rolled = win if b == 0 else pltpu.roll(win, span - b, axis=0)
                taps = [a * sub + b - shift for a in range(halo // sub + 1)]
                prods = [w_ref[k:k + 1, cs] * rolled[k + shift - b:k + shift - b + rows, :]
                         for k in taps if 0 <= k < CONV_K]
                parts.append(functools.reduce(jnp.add, prods))
            while len(parts) > 1:
                parts = [parts[i] + parts[i + 1] for i in range(0, len(parts), 2)]
            accs.append(parts[0] + cb_ref[:, cs])
        mu = sum(jnp.sum(a, axis=-1, keepdims=True) for a in accs) / ch
        cen = [a - mu for a in accs]
        var = sum(jnp.sum(c * c, axis=-1, keepdims=True) for c in cen) / ch
        inv = lax.rsqrt(var + LN_EPS)
        for cs, c in zip(slabs, cen):
            y = c * inv * g_ref[:, cs] + b_ref[:, cs]
            o_ref[pl.ds(t0, rows), cs] = (y * jax.nn.sigmoid(y)).astype(o_ref.dtype)
        return 0

    lax.fori_loop(0, seq // rows, body, 0, unroll=4)


def _conv_module(u, w, cb, g, b, l, *, rows=64, halo=32):
    bsz, seq, ch = u.shape
    return pl.pallas_call(
        functools.partial(_conv_kernel, rows=rows, halo=halo),
        grid=(bsz,),
        in_specs=[pl.BlockSpec((None, seq, ch), lambda i: (i, 0, 0)),
                  _resident(w, l), _resident(cb, l), _resident(g, l), _resident(b, l)],
        out_specs=pl.BlockSpec((None, seq, ch), lambda i: (i, 0, 0)),
        out_shape=jax.ShapeDtypeStruct((bsz, seq, ch), BF16),
        scratch_shapes=[pltpu.VMEM((halo + seq, ch), F32)],
        compiler_params=_params("parallel"),
        name="conv_module",
    )(u, w, cb, g, b)


def _fox_kernel(q_ref, k_ref, v_ref, o_ref, m_ref, acc_ref, *, tq):
    i = pl.program_id(2)
    rr = lax.broadcasted_iota(jnp.int32, (tq, tq), 0)
    cc = lax.broadcasted_iota(jnp.int32, (tq, tq), 1)
    causal = cc <= rr
    dn = (((1,), (1,)), ((), ()))

    def step(j, masked):
        k0 = pl.multiple_of(j * tq, tq)
        for a in range(HEADS_PER_SLAB):
            s = lax.dot_general(q_ref[a], k_ref[a, pl.ds(k0, tq), :], dn, preferred_element_type=F32)
            if masked:
                s = jnp.where(causal, s, NEG_BIG)
            m_old = m_ref[a]
            m_new = jnp.maximum(m_old, jnp.max(s, axis=-1, keepdims=True))
            alpha = jnp.exp(m_old - m_new)
            p = jnp.exp(s - pltpu.repeat(m_new, tq // LANES, axis=1))
            m_ref[a] = m_new
            acc_ref[a] = alpha * acc_ref[a] + jnp.dot(
                p.astype(BF16), v_ref[a, pl.ds(k0, tq), :], preferred_element_type=F32)

    m_ref[...] = jnp.full(m_ref.shape, NEG_BIG, F32)
    acc_ref[...] = jnp.zeros(acc_ref.shape, F32)
    step(i, True)

    def body(j, _):
        step(j, False)
        return 0

    lax.fori_loop(0, i, body, 0)

    lane = lax.broadcasted_iota(jnp.int32, (1, LANES), 1)
    acc0, acc1 = acc_ref[0], acc_ref[1]
    out = jnp.where(lane < HEAD_DIM,
                    acc0 / pltpu.roll(acc0, HEAD_DIM, axis=1),
                    pltpu.roll(acc1, HEAD_DIM, axis=1) / acc1)
    o_ref[...] = out.astype(o_ref.dtype)


def _fox_attention(qa, ka, va, *, tq=512):
    bsz, heads, seq, _ = qa.shape
    qspec = pl.BlockSpec((None, HEADS_PER_SLAB, tq, LANES), lambda b, p, i: (b, p, i, 0))
    kvspec = pl.BlockSpec((None, HEADS_PER_SLAB, seq, LANES), lambda b, p, i: (b, p, 0, 0))
    return pl.pallas_call(
        functools.partial(_fox_kernel, tq=tq),
        grid=(bsz, heads // HEADS_PER_SLAB, seq // tq),
        in_specs=[qspec, kvspec, kvspec],
        out_specs=pl.BlockSpec((None, tq, LANES), lambda b, p, i: (b, i, p)),
        out_shape=jax.ShapeDtypeStruct((bsz, seq, heads * HEAD_DIM), BF16),
        scratch_shapes=[pltpu.VMEM((HEADS_PER_SLAB, tq, LANES), F32),
                        pltpu.VMEM((HEADS_PER_SLAB, tq, LANES), F32)],
        compiler_params=_params("parallel", "parallel", "arbitrary"),
        name="fox_attention",
    )(qa, ka, va)


def _dil_kernel(q_ref, k_ref, v_ref, o_ref, bias_ref, bias2_ref, m_ref, l_ref, acc_ref, *,
                blk, unroll, rows_out):
    seq = q_ref.shape[0]
    win = 2 * blk
    lane = lax.broadcasted_iota(jnp.int32, (1, LANES), 1)
    head0 = lane < HEAD_DIM
    dist0 = (lax.broadcasted_iota(jnp.int32, (win, win), 0) % blk
             - lax.broadcasted_iota(jnp.int32, (win, win), 1))
    row = lax.broadcasted_iota(jnp.int32, (2 * win, win), 0)
    col = lax.broadcasted_iota(jnp.int32, (2 * win, win), 1)
    bias_ref[0] = jnp.where(dist0 >= 0, 0.0, NEG_BIG)
    bias_ref[1] = jnp.where(jnp.logical_and(dist0 + blk >= 0, dist0 <= 0), 0.0, NEG_BIG)
    dist_all = (row % win) - col
    bias2_ref[...] = jnp.where(jnp.logical_and(dist_all >= 0, dist_all <= blk), 0.0, NEG_BIG)
    ones = jnp.ones((win, LANES), BF16)
    dn = (((1,), (1,)), ((), ()))

    def attend(bi, rows, wrows, bias):
        nq = rows.size
        qb = q_ref[rows, :].astype(BF16)
        zero = jnp.zeros_like(qb)
        qs = jnp.concatenate([jnp.where(head0, qb, zero), jnp.where(head0, zero, qb)], axis=0)
        kw = k_ref[wrows, :].astype(BF16)
        vw = jnp.concatenate([v_ref[wrows, :].astype(BF16), ones], axis=1)
        s = lax.dot_general(qs, kw, dn, preferred_element_type=F32) + bias
        m = jnp.max(s, axis=-1, keepdims=True)
        p = jnp.exp(s - m).astype(BF16)
        pv = jnp.dot(p, vw, preferred_element_type=F32)
        mb = jnp.broadcast_to(m, (2 * nq, LANES))
        m_ref[bi, rows, :] = jnp.where(head0, mb[:nq], mb[nq:])
        acc_ref[bi, rows, :] = jnp.where(head0, pv[:nq, :LANES], pv[nq:, :LANES])
        l_ref[bi, rows, :] = jnp.where(head0, pv[:nq, LANES:], pv[nq:, LANES:])

    for bi, (_, dil) in enumerate(DIL_CONFIGS):
        nb = seq // (dil * blk)
        ds = (lambda start, size, dil=dil: pl.ds(start, size, stride=dil) if dil > 1 else pl.ds(start, size))

        if nb == 2:
            def body(r, _, bi=bi, ds=ds):
                attend(bi, ds(r, win), ds(r, win), bias2_ref[...])
                return 0

            lax.fori_loop(0, dil, body, 0, unroll=unroll // 2)
            continue

        def body(idx, _, bi=bi, dil=dil, nb=nb, ds=ds):
            r = idx // nb
            n = idx % nb
            start = r + n * (dil * blk)
            has_prev = (n > 0).astype(jnp.int32)
            attend(bi, ds(start, blk), ds(start - has_prev * (dil * blk), win), bias_ref[has_prev])
            return 0

        lax.fori_loop(0, seq // blk, body, 0, unroll=unroll)

    def merge(i, _):
        rows = pl.ds(pl.multiple_of(i * rows_out, rows_out), rows_out)
        ms = [m_ref[bi, rows, :] for bi in range(len(DIL_CONFIGS))]
        m_all = functools.reduce(jnp.maximum, ms)
        ws = [jnp.exp(m - m_all) for m in ms]
        num = sum(w * acc_ref[bi, rows, :] for bi, w in enumerate(ws))
        den = sum(w * l_ref[bi, rows, :] for bi, w in enumerate(ws))
        o_ref[rows, :] = (num / den).astype(o_ref.dtype)
        return 0

    lax.fori_loop(0, seq // rows_out, merge, 0)


def _dilated_attention(q, k, v, *, blk=128, unroll=8, rows_out=256):
    bsz, seq, width = q.shape
    nbr = len(DIL_CONFIGS)
    spec = pl.BlockSpec((None, seq, LANES), lambda b, p: (b, 0, p))
    return pl.pallas_call(
        functools.partial(_dil_kernel, blk=blk, unroll=unroll, rows_out=rows_out),
        grid=(bsz, width // LANES),
        in_specs=[spec, spec, spec],
        out_specs=spec,
        out_shape=jax.ShapeDtypeStruct((bsz, seq, width), BF16),
        scratch_shapes=[pltpu.VMEM((2, 2 * blk, 2 * blk), F32), pltpu.VMEM((4 * blk, 2 * blk), F32)]
        + [pltpu.VMEM((nbr, seq, LANES), F32)] * 3,
        compiler_params=_params("parallel", "parallel"),
        name="dilated_attention",
    )(q, k, v)


def _out_kernel(h_ref, ya_ref, yb_ref, yc_ref, w_ref, g_ref, b_ref, o_ref, *, alpha, cw, fw):
    y = jnp.dot(ya_ref[...], w_ref[0:cw, :], preferred_element_type=F32)
    y = y + jnp.dot(yb_ref[...], w_ref[cw:cw + fw, :], preferred_element_type=F32)
    y = y + jnp.dot(yc_ref[...], w_ref[cw + fw:, :], preferred_element_type=F32)
    o_ref[...] = _layer_norm(alpha * h_ref[...] + y, g_ref[...], b_ref[...])


def _out_proj_ln(h, ya, yb, yc, w, g, b, l, n, *, alpha, tm=512):
    t, d = h.shape
    row = lambda i: (i, 0)
    return pl.pallas_call(
        functools.partial(_out_kernel, alpha=alpha, cw=ya.shape[1], fw=yb.shape[1]),
        grid=(t // tm,),
        in_specs=[pl.BlockSpec((tm, d), row), pl.BlockSpec((tm, ya.shape[1]), row),
                  pl.BlockSpec((tm, yb.shape[1]), row), pl.BlockSpec((tm, yc.shape[1]), row),
                  _resident(w, l), _resident(g, l, n), _resident(b, l, n)],
        out_specs=pl.BlockSpec((tm, d), row),
        out_shape=jax.ShapeDtypeStruct((t, d), F32),
        compiler_params=_params("parallel"),
        name="out_proj_ln",
    )(h, ya, yb, yc, w, g, b)


def _rope_tables(seq):
    inv = 1.0 / (ROPE_THETA ** (jnp.arange(0, HEAD_DIM, 2, dtype=F32) / HEAD_DIM))
    ang = jnp.arange(seq, dtype=F32)[:, None] * inv[None, :]
    ang = jnp.concatenate([ang, ang], axis=-1)
    cos, sin = jnp.cos(ang), jnp.sin(ang)
    first_half = (jnp.arange(HEAD_DIM) < HEAD_DIM // 2)[None, :]
    sin_lo = jnp.where(first_half, -sin, 0.0)
    sin_hi = jnp.where(first_half, 0.0, sin)
    rep = lambda a: jnp.tile(a, (1, HEADS_PER_SLAB))
    return rep(cos), rep(sin_lo), rep(sin_hi)


def kernel(x, w_in, w_o, forget_bias, conv_w, conv_b, conv_ln_g, conv_ln_b, ffn_w_in, ffn_w_out, ln_g, ln_b):
    bsz, seq, d = x.shape
    depth = w_in.shape[0]
    cw = conv_w.shape[-1]
    heads_f = forget_bias.shape[-1]
    fw = heads_f * HEAD_DIM
    dw = d - cw - fw
    alpha = (2 * depth) ** 0.25
    t = bsz * seq
    cos, slo, shi = _rope_tables(seq)

    ffn_in = ffn_w_in.astype(BF16)
    ffn_out = ffn_w_out.astype(BF16)
    w_out = w_o.astype(BF16)
    o_f = 2 * cw + 3 * fw
    w_mix = jnp.concatenate(
        [w_in[:, :, :o_f], jnp.repeat(w_in[:, :, o_f:o_f + heads_f], HEAD_DIM, axis=2),
         w_in[:, :, o_f + heads_f:]], axis=2).astype(BF16)
    fb = jnp.repeat(forget_bias, HEAD_DIM, axis=1)[:, None, :]
    row3 = lambda a: a[:, None, :]
    cb, cg, cbeta = row3(conv_b), row3(conv_ln_g), row3(conv_ln_b)
    g4, b4 = ln_g[:, :, None, :], ln_b[:, :, None, :]
    b3 = lambda a: a.reshape(bsz, seq, a.shape[-1])

    h = x.reshape(t, d)
    for l in range(depth):
        h = _ffn_ln(h, ffn_in, ffn_out, g4, b4, l, 0, 0, alpha=alpha)
        u, fq, fk, fv, lf, dq, dk, dv = _mixer_proj(h, w_mix, fb, cos, slo, shi, l,
                                                    seq=seq, cw=cw, fw=fw, dw=dw)
        qa, ka, va = _fox_prep(b3(lf), b3(fq), b3(fk), b3(fv))
        ya = _conv_module(b3(u), conv_w, cb, cg, cbeta, l)
        yb = _fox_attention(qa, ka, va)
        yc = _dilated_attention(b3(dq), b3(dk), b3(dv))
        h = _out_proj_ln(h, ya.reshape(t, cw), yb.reshape(t, fw), yc.reshape(t, dw),
                         w_out, g4, b4, l, 1, alpha=alpha)
        h = _ffn_ln(h, ffn_in, ffn_out, g4, b4, l, 1, 2, alpha=alpha)
    return h.reshape(bsz, seq, d)
```

```python
import functools

import jax
import jax.numpy as jnp
from jax import lax
from jax.experimental import pallas as pl
from jax.experimental.pallas import tpu as pltpu

HEAD_DIM = 64
CONV_K = 31
DIL_CONFIGS = ((128, 1), (512, 4), (2048, 16))
ROPE_THETA = 10000.0
LN_EPS = 1e-5
LANES = 128
HEADS_PER_SLAB = LANES // HEAD_DIM
NEG_BIG = -1e30
VMEM_LIMIT = 56 * 1024 * 1024

F32 = jnp.float32
BF16 = jnp.bfloat16


def _layer_norm(y, g, b):
    mu = jnp.mean(y, axis=-1, keepdims=True)
    yc = y - mu
    var = jnp.mean(yc * yc, axis=-1, keepdims=True)
    return yc * lax.rsqrt(var + LN_EPS) * g + b


def _params(*sem):
    return pltpu.CompilerParams(dimension_semantics=sem, vmem_limit_bytes=VMEM_LIMIT)


def _resident(arr, *lead):
    rest = arr.shape[len(lead):]
    index = tuple(lead) + (0,) * len(rest)
    return pl.BlockSpec((None,) * len(lead) + rest, lambda *_: index, pipeline_mode=pl.Buffered(1))


def _ffn_kernel(x_ref, win_ref, wout_ref, g_ref, b_ref, o_ref, acc_ref, *, alpha, d_ff, chunk):
    x = x_ref[...]
    xb = x.astype(BF16)
    for c in range(d_ff // chunk):
        lo = c * chunk
        gate = jnp.dot(xb, win_ref[:, lo:lo + chunk], preferred_element_type=F32)
        up = jnp.dot(xb, win_ref[:, d_ff + lo:d_ff + lo + chunk], preferred_element_type=F32)
        hid = (gate * jax.nn.sigmoid(gate) * up).astype(BF16)
        part = jnp.dot(hid, wout_ref[lo:lo + chunk, :], preferred_element_type=F32)
        if c == 0:
            acc_ref[...] = part
        else:
            acc_ref[...] += part
    y = alpha * x + 0.5 * acc_ref[...]
    o_ref[...] = _layer_norm(y, g_ref[...], b_ref[...])


def _ffn_ln(x, w_in, w_out, g, b, l, j, n, *, alpha, tm=512, chunk=256):
    t, d = x.shape
    d_ff = w_out.shape[-2]
    return pl.pallas_call(
        functools.partial(_ffn_kernel, alpha=alpha, d_ff=d_ff, chunk=chunk),
        grid=(t // tm,),
        in_specs=[pl.BlockSpec((tm, d), lambda i: (i, 0)),
                  _resident(w_in, l, j), _resident(w_out, l, j),
                  _resident(g, l, n), _resident(b, l, n)],
        out_specs=pl.BlockSpec((tm, d), lambda i: (i, 0)),
        out_shape=jax.ShapeDtypeStruct((t, d), F32),
        scratch_shapes=[pltpu.VMEM((tm, d), F32)],
        compiler_params=_params("parallel"),
        name="ffn_ln",
    )(x, w_in, w_out, g, b)


def _rope_slab(t, cos, sin_lo, sin_hi):
    nxt = pltpu.roll(t, LANES - HEAD_DIM // 2, axis=1)
    prv = pltpu.roll(t, HEAD_DIM // 2, axis=1)
    return t * cos + nxt * sin_lo + prv * sin_hi


def _split3(x):
    hi = x.astype(BF16).astype(F32)
    mid = (x - hi).astype(BF16).astype(F32)
    return hi, mid, x - hi - mid


def _store_fox_operands(cum, q, k, v, qa_ref, ka_ref, va_ref, rows):
    lane = lax.broadcasted_iota(jnp.int32, (1, LANES), 1)
    is_head = lane < HEAD_DIM
    aug = lane - HEAD_DIM
    for h in range(q.shape[-1] // HEAD_DIM):
        slab = slice((h // HEADS_PER_SLAB) * LANES, (h // HEADS_PER_SLAB + 1) * LANES)
        odd = h % HEADS_PER_SLAB == 1
        head_lanes = lambda x: pltpu.roll(x[:, slab], HEAD_DIM, axis=1) if odd else x[:, slab]
        cc = cum[:, slab] if odd else pltpu.roll(cum[:, slab], HEAD_DIM, axis=1)
        hi, mid, lo = _split3(cc)
        q_aug = jnp.where(aug == 0, hi, jnp.where(aug == 1, mid, jnp.where(aug == 2, lo,
                          jnp.where(aug < 6, 1.0, 0.0))))
        k_aug = jnp.where(aug < 3, 1.0, jnp.where(aug == 3, -hi, jnp.where(aug == 4, -mid,
                          jnp.where(aug == 5, -lo, 0.0))))
        qa_ref[h, rows, :] = jnp.where(is_head, head_lanes(q), q_aug).astype(BF16)
        ka_ref[h, rows, :] = jnp.where(is_head, head_lanes(k), k_aug).astype(BF16)
        va_ref[h, rows, :] = jnp.where(is_head, head_lanes(v), 1.0).astype(BF16)


def _proj_kernel(h_ref, w_ref, fb_ref, cos_ref, slo_ref, shi_ref,
                 u_ref, qa_ref, ka_ref, va_ref, dq_ref, dk_ref, dv_ref, carry_ref,
                 *, cw, fw, dw, nsb, blk):
    @pl.when(pl.program_id(0) % nsb == 0)
    def _():
        carry_ref[...] = jnp.zeros_like(carry_ref)

    hb = h_ref[...].astype(BF16)

    def cols(lo, width):
        return jnp.dot(hb, w_ref[:, lo:lo + width], preferred_element_type=F32)

    scale = HEAD_DIM ** -0.5
    val = cols(0, cw)
    gate = cols(cw, cw)
    u_ref[...] = val * jax.nn.sigmoid(gate)
    base = 2 * cw
    fq = cols(base, fw) * scale
    fk = cols(base + fw, fw)
    fv = cols(base + 2 * fw, fw)
    log_f = jax.nn.log_sigmoid(cols(base + 3 * fw, fw) + fb_ref[...])

    r = lax.broadcasted_iota(jnp.int32, (blk, blk), 0)
    c = lax.broadcasted_iota(jnp.int32, (blk, blk), 1)
    tri = (c <= r).astype(BF16)
    cum = carry_ref[...]
    for r0 in range(0, hb.shape[0], blk):
        rows = slice(r0, r0 + blk)
        cum = cum[-1:, :]
        for part in _split3(log_f[rows]):
            cum = cum + jnp.dot(tri, part.astype(BF16), preferred_element_type=F32)
        _store_fox_operands(cum, fq[rows], fk[rows], fv[rows], qa_ref, ka_ref, va_ref, rows)
    carry_ref[...] = cum[-1:, :]

    base = base + 4 * fw
    cos, slo, shi = cos_ref[...], slo_ref[...], shi_ref[...]
    q, k = cols(base, dw), cols(base + dw, dw)
    for j in range(dw // LANES):
        sl = slice(j * LANES, (j + 1) * LANES)
        dq_ref[:, sl] = _rope_slab(q[:, sl], cos, slo, shi) * scale
        dk_ref[:, sl] = _rope_slab(k[:, sl], cos, slo, shi)
    dv_ref[...] = cols(base + 2 * dw, dw)


def _mixer_proj(h, w, fb, cos, slo, shi, l, *, bsz, cw, fw, dw, tm=512, blk=128):
    t, d = h.shape
    seq = t // bsz
    nsb = seq // tm
    heads = fw // HEAD_DIM
    row = lambda i: (i, 0)
    pos = lambda i: (i % nsb, 0)
    tok = lambda n: (pl.BlockSpec((tm, n), row), jax.ShapeDtypeStruct((t, n), F32))
    fox = (pl.BlockSpec((None, heads, tm, LANES), lambda i: (i // nsb, 0, i % nsb, 0)),
           jax.ShapeDtypeStruct((bsz, heads, seq, LANES), BF16))
    outs = [tok(cw), fox, fox, fox, tok(dw), tok(dw), tok(dw)]
    return pl.pallas_call(
        functools.partial(_proj_kernel, cw=cw, fw=fw, dw=dw, nsb=nsb, blk=blk),
        grid=(t // tm,),
        in_specs=[pl.BlockSpec((tm, d), row), _resident(w, l), _resident(fb, l),
                  pl.BlockSpec((tm, LANES), pos), pl.BlockSpec((tm, LANES), pos),
                  pl.BlockSpec((tm, LANES), pos)],
        out_specs=[spec for spec, _ in outs],
        out_shape=[shape for _, shape in outs],
        scratch_shapes=[pltpu.VMEM((1, fw), F32)],
        compiler_params=_params("arbitrary"),
        name="mixer_proj",
    )(h, w, fb, cos, slo, shi)


def _conv_kernel(u_ref, w_ref, cb_ref, g_ref, b_ref, o_ref, pad_ref, *, rows, halo):
    seq, ch = u_ref.shape
    pad_ref[0:halo, :] = jnp.zeros((halo, ch), F32)
    pad_ref[halo:halo + seq, :] = u_ref[...]
    shift = halo - (CONV_K - 1)

    span = rows + halo
    sub = 8

    slabs = [slice(c0, c0 + LANES) for c0 in range(0, ch, LANES)]

    def body(i, _):
        t0 = pl.multiple_of(i * rows, rows)
        accs = []
        for cs in slabs:
            win = pad_ref[pl.ds(t0, span), cs]
            parts = []
            for b in range(sub):
                rolled = win if b == 0 else pltpu.roll(win, span - b, axis=0)
                taps = [a * sub + b - shift for a in range(halo // sub + 1)]
                prods = [w_ref[k:k + 1, cs] * rolled[k + shift - b:k + shift - b + rows, :]
                         for k in taps if 0 <= k < CONV_K]
                parts.append(functools.reduce(jnp.add, prods))
            while len(parts) > 1:
                parts = [parts[i] + parts[i + 1] for i in range(0, len(parts), 2)]
            accs.append(parts[0] + cb_ref[:, cs])
        mu = sum(jnp.sum(a, axis=-1, keepdims=True) for a in accs) / ch
        cen = [a - mu for a in accs]
        var = sum(jnp.sum(c * c, axis=-1, keepdims=True) for c in cen) / ch
        inv = lax.rsqrt(var + LN_EPS)
        for cs, c in zip(slabs, cen):
            y = c * inv * g_ref[:, cs] + b_ref[:, cs]
            o_ref[pl.ds(t0, rows), cs] = (y * jax.nn.sigmoid(y)).astype(o_ref.dtype)
        return 0

    lax.fori_loop(0, seq // rows, body, 0, unroll=4)


def _conv_module(u, w, cb, g, b, l, *, rows=64, halo=32):
    bsz, seq, ch = u.shape
    return pl.pallas_call(
        functools.partial(_conv_kernel, rows=rows, halo=halo),
        grid=(bsz,),
        in_specs=[pl.BlockSpec((None, seq, ch), lambda i: (i, 0, 0)),
                  _resident(w, l), _resident(cb, l), _resident(g, l), _resident(b, l)],
        out_specs=pl.BlockSpec((None, seq, ch), lambda i: (i, 0, 0)),
        out_shape=jax.ShapeDtypeStruct((bsz, seq, ch), BF16),
        scratch_shapes=[pltpu.VMEM((halo + seq, ch), F32)],
        compiler_params=_params("parallel"),
        name="conv_module",
    )(u, w, cb, g, b)


def _fox_kernel(q_ref, k_ref, v_ref, o_ref, m_ref, acc_ref, *, tq):
    i = pl.program_id(2)
    rr = lax.broadcasted_iota(jnp.int32, (tq, tq), 0)
    cc = lax.broadcasted_iota(jnp.int32, (tq, tq), 1)
    causal = cc <= rr
    dn = (((1,), (1,)), ((), ()))

    def step(j, masked):
        k0 = pl.multiple_of(j * tq, tq)
        for a in range(HEADS_PER_SLAB):
            s = lax.dot_general(q_ref[a], k_ref[a, pl.ds(k0, tq), :], dn, preferred_element_type=F32)
            if masked:
                s = jnp.where(causal, s, NEG_BIG)
            m_old = m_ref[a]
            m_new = jnp.maximum(m_old, jnp.max(s, axis=-1, keepdims=True))
            alpha = jnp.exp(m_old - m_new)
            p = jnp.exp(s - pltpu.repeat(m_new, tq // LANES, axis=1))
            m_ref[a] = m_new
            acc_ref[a] = alpha * acc_ref[a] + jnp.dot(
                p.astype(BF16), v_ref[a, pl.ds(k0, tq), :], preferred_element_type=F32)

    m_ref[...] = jnp.full(m_ref.shape, NEG_BIG, F32)
    acc_ref[...] = jnp.zeros(acc_ref.shape, F32)
    step(i, True)

    def body(j, _):
        step(j, False)
        return 0

    lax.fori_loop(0, i, body, 0)

    lane = lax.broadcasted_iota(jnp.int32, (1, LANES), 1)
    acc0, acc1 = acc_ref[0], acc_ref[1]
    out = jnp.where(lane < HEAD_DIM,
                    acc0 / pltpu.roll(acc0, HEAD_DIM, axis=1),
                    pltpu.roll(acc1, HEAD_DIM, axis=1) / acc1)
    o_ref[...] = out.astype(o_ref.dtype)


def _fox_attention(qa, ka, va, *, tq=512):
    bsz, heads, seq, _ = qa.shape
    qspec = pl.BlockSpec((None, HEADS_PER_SLAB, tq, LANES), lambda b, p, i: (b, p, i, 0))
    kvspec = pl.BlockSpec((None, HEADS_PER_SLAB, seq, LANES), lambda b, p, i: (b, p, 0, 0))
    return pl.pallas_call(
        functools.partial(_fox_kernel, tq=tq),
        grid=(bsz, heads // HEADS_PER_SLAB, seq // tq),
        in_specs=[qspec, kvspec, kvspec],
        out_specs=pl.BlockSpec((None, tq, LANES), lambda b, p, i: (b, i, p)),
        out_shape=jax.ShapeDtypeStruct((bsz, seq, heads * HEAD_DIM), BF16),
        scratch_shapes=[pltpu.VMEM((HEADS_PER_SLAB, tq, LANES), F32),
                        pltpu.VMEM((HEADS_PER_SLAB, tq, LANES), F32)],
        compiler_params=_params("parallel", "parallel", "arbitrary"),
        name="fox_attention",
    )(qa, ka, va)


def _dil_kernel(q_ref, k_ref, v_ref, o_ref, bias_ref, bias2_ref, m_ref, l_ref, acc_ref, *,
                blk, unroll, rows_out):
    seq = q_ref.shape[0]
    win = 2 * blk
    lane = lax.broadcasted_iota(jnp.int32, (1, LANES), 1)
    head0 = lane < HEAD_DIM
    dist0 = (lax.broadcasted_iota(jnp.int32, (win, win), 0) % blk
             - lax.broadcasted_iota(jnp.int32, (win, win), 1))
    row = lax.broadcasted_iota(jnp.int32, (2 * win, win), 0)
    col = lax.broadcasted_iota(jnp.int32, (2 * win, win), 1)
    bias_ref[0] = jnp.where(dist0 >= 0, 0.0, NEG_BIG)
    bias_ref[1] = jnp.where(jnp.logical_and(dist0 + blk >= 0, dist0 <= 0), 0.0, NEG_BIG)
    dist_all = (row % win) - col
    bias2_ref[...] = jnp.where(jnp.logical_and(dist_all >= 0, dist_all <= blk), 0.0, NEG_BIG)
    ones = jnp.ones((win, LANES), BF16)
    dn = (((1,), (1,)), ((), ()))

    def attend(bi, rows, wrows, bias):
        nq = rows.size
        qb = q_ref[rows, :].astype(BF16)
        zero = jnp.zeros_like(qb)
        qs = jnp.concatenate([jnp.where(head0, qb, zero), jnp.where(head0, zero, qb)], axis=0)
        kw = k_ref[wrows, :].astype(BF16)
        vw = jnp.concatenate([v_ref[wrows, :].astype(BF16), ones], axis=1)
        s = lax.dot_general(qs, kw, dn, preferred_element_type=F32) + bias
        m = jnp.max(s, axis=-1, keepdims=True)
        p = jnp.exp(s - m).astype(BF16)
        pv = jnp.dot(p, vw, preferred_element_type=F32)
        mb = jnp.broadcast_to(m, (2 * nq, LANES))
        m_ref[bi, rows, :] = jnp.where(head0, mb[:nq], mb[nq:])
        acc_ref[bi, rows, :] = jnp.where(head0, pv[:nq, :LANES], pv[nq:, :LANES])
        l_ref[bi, rows, :] = jnp.where(head0, pv[:nq, LANES:], pv[nq:, LANES:])

    for bi, (_, dil) in enumerate(DIL_CONFIGS):
        nb = seq // (dil * blk)
        ds = (lambda start, size, dil=dil: pl.ds(start, size, stride=dil) if dil > 1 else pl.ds(start, size))

        if nb == 2:
            def body(r, _, bi=bi, ds=ds):
                attend(bi, ds(r, win), ds(r, win), bias2_ref[...])
                return 0

            lax.fori_loop(0, dil, body, 0, unroll=unroll // 2)
            continue

        def body(idx, _, bi=bi, dil=dil, nb=nb, ds=ds):
            r = idx // nb
            n = idx % nb
            start = r + n * (dil * blk)
            has_prev = (n > 0).astype(jnp.int32)
            attend(bi, ds(start, blk), ds(start - has_prev * (dil * blk), win), bias_ref[has_prev])
            return 0

        lax.fori_loop(0, seq // blk, body, 0, unroll=unroll)

    def merge(i, _):
        rows = pl.ds(pl.multiple_of(i * rows_out, rows_out), rows_out)
        ms = [m_ref[bi, rows, :] for bi in range(len(DIL_CONFIGS))]
        m_all = functools.reduce(jnp.maximum, ms)
        ws = [jnp.exp(m - m_all) for m in ms]
        num = sum(w * acc_ref[bi, rows, :] for bi, w in enumerate(ws))
        den = sum(w * l_ref[bi, rows, :] for bi, w in enumerate(ws))
        o_ref[rows, :] = (num / den).astype(o_ref.dtype)
        return 0

    lax.fori_loop(0, seq // rows_out, merge, 0)


def _dilated_attention(q, k, v, *, blk=128, unroll=8, rows_out=256):
    bsz, seq, width = q.shape
    nbr = len(DIL_CONFIGS)
    spec = pl.BlockSpec((None, seq, LANES), lambda b, p: (b, 0, p))
    return pl.pallas_call(
        functools.partial(_dil_kernel, blk=blk, unroll=unroll, rows_out=rows_out),
        grid=(bsz, width // LANES),
        in_specs=[spec, spec, spec],
        out_specs=spec,
        out_shape=jax.ShapeDtypeStruct((bsz, seq, width), BF16),
        scratch_shapes=[pltpu.VMEM((2, 2 * blk, 2 * blk), F32), pltpu.VMEM((4 * blk, 2 * blk), F32)]
        + [pltpu.VMEM((nbr, seq, LANES), F32)] * 3,
        compiler_params=_params("parallel", "parallel"),
        name="dilated_attention",
    )(q, k, v)


def _out_kernel(h_ref, ya_ref, yb_ref, yc_ref, w_ref, g_ref, b_ref, o_ref, *, alpha, cw, fw):
    y = jnp.dot(ya_ref[...], w_ref[0:cw, :], preferred_element_type=F32)
    y = y + jnp.dot(yb_ref[...], w_ref[cw:cw + fw, :], preferred_element_type=F32)
    y = y + jnp.dot(yc_ref[...], w_ref[cw + fw:, :], preferred_element_type=F32)
    o_ref[...] = _layer_norm(alpha * h_ref[...] + y, g_ref[...], b_ref[...])


def _out_proj_ln(h, ya, yb, yc, w, g, b, l, n, *, alpha, tm=512):
    t, d = h.shape
    row = lambda i: (i, 0)
    return pl.pallas_call(
        functools.partial(_out_kernel, alpha=alpha, cw=ya.shape[1], fw=yb.shape[1]),
        grid=(t // tm,),
        in_specs=[pl.BlockSpec((tm, d), row), pl.BlockSpec((tm, ya.shape[1]), row),
                  pl.BlockSpec((tm, yb.shape[1]), row), pl.BlockSpec((tm, yc.shape[1]), row),
                  _resident(w, l), _resident(g, l, n), _resident(b, l, n)],
        out_specs=pl.BlockSpec((tm, d), row),
        out_shape=jax.ShapeDtypeStruct((t, d), F32),
        compiler_params=_params("parallel"),
        name="out_proj_ln",
    )(h, ya, yb, yc, w, g, b)


def _rope_tables(seq):
    inv = 1.0 / (ROPE_THETA ** (jnp.arange(0, HEAD_DIM, 2, dtype=F32) / HEAD_DIM))
    ang = jnp.arange(seq, dtype=F32)[:, None] * inv[None, :]
    ang = jnp.concatenate([ang, ang], axis=-1)
    cos, sin = jnp.cos(ang), jnp.sin(ang)
    first_half = (jnp.arange(HEAD_DIM) < HEAD_DIM // 2)[None, :]
    sin_lo = jnp.where(first_half, -sin, 0.0)
    sin_hi = jnp.where(first_half, 0.0, sin)
    rep = lambda a: jnp.tile(a, (1, HEADS_PER_SLAB))
    return rep(cos), rep(sin_lo), rep(sin_hi)


def kernel(x, w_in, w_o, forget_bias, conv_w, conv_b, conv_ln_g, conv_ln_b, ffn_w_in, ffn_w_out, ln_g, ln_b):
    bsz, seq, d = x.shape
    depth = w_in.shape[0]
    cw = conv_w.shape[-1]
    heads_f = forget_bias.shape[-1]
    fw = heads_f * HEAD_DIM
    dw = d - cw - fw
    alpha = (2 * depth) ** 0.25
    t = bsz * seq
    cos, slo, shi = _rope_tables(seq)

    ffn_in = ffn_w_in.astype(BF16)
    ffn_out = ffn_w_out.astype(BF16)
    w_out = w_o.astype(BF16)
    o_f = 2 * cw + 3 * fw
    w_mix = jnp.concatenate(
        [w_in[:, :, :o_f], jnp.repeat(w_in[:, :, o_f:o_f + heads_f], HEAD_DIM, axis=2),
         w_in[:, :, o_f + heads_f:]], axis=2).astype(BF16)
    fb = jnp.repeat(forget_bias, HEAD_DIM, axis=1)[:, None, :]
    row3 = lambda a: a[:, None, :]
    cb, cg, cbeta = row3(conv_b), row3(conv_ln_g), row3(conv_ln_b)
    g4, b4 = ln_g[:, :, None, :], ln_b[:, :, None, :]
    b3 = lambda a: a.reshape(bsz, seq, a.shape[-1])

    h = x.reshape(t, d)
    for l in range(depth):
        h = _ffn_ln(h, ffn_in, ffn_out, g4, b4, l, 0, 0, alpha=alpha)
        u, qa, ka, va, dq, dk, dv = _mixer_proj(h, w_mix, fb, cos, slo, shi, l,
                                                bsz=bsz, cw=cw, fw=fw, dw=dw)
        ya = _conv_module(b3(u), conv_w, cb, cg, cbeta, l)
        yb = _fox_attention(qa, ka, va)
        yc = _dilated_attention(b3(dq), b3(dk), b3(dv))
        h = _out_proj_ln(h, ya.reshape(t, cw), yb.reshape(t, fw), yc.reshape(t, dw),
                         w_out, g4, b4, l, 1, alpha=alpha)
        h = _ffn_ln(h, ffn_in, ffn_out, g4, b4, l, 1, 2, alpha=alpha)
    return h.reshape(bsz, seq, d)
```

```python
import functools

import jax
import jax.numpy as jnp
from jax import lax
from jax.experimental import pallas as pl
from jax.experimental.pallas import tpu as pltpu

HEAD_DIM = 64
CONV_K = 31
DIL_CONFIGS = ((128, 1), (512, 4), (2048, 16))
ROPE_THETA = 10000.0
LN_EPS = 1e-5
LANES = 128
HEADS_PER_SLAB = LANES // HEAD_DIM
NEG_BIG = -1e30
LOG2_E = 1.4426950408889634
SCORE_SCALE = HEAD_DIM ** -0.5 * LOG2_E
VMEM_LIMIT = 56 * 1024 * 1024

F32 = jnp.float32
BF16 = jnp.bfloat16


def _layer_norm(y, g, b):
    mu = jnp.mean(y, axis=-1, keepdims=True)
    yc = y - mu
    var = jnp.mean(yc * yc, axis=-1, keepdims=True)
    return yc * lax.rsqrt(var + LN_EPS) * g + b


def _params(*sem):
    return pltpu.CompilerParams(dimension_semantics=sem, vmem_limit_bytes=VMEM_LIMIT)


def _resident(arr, *lead):
    rest = arr.shape[len(lead):]
    index = tuple(lead) + (0,) * len(rest)
    return pl.BlockSpec((None,) * len(lead) + rest, lambda *_: index, pipeline_mode=pl.Buffered(1))


def _ffn_kernel(x_ref, win_ref, wout_ref, g_ref, b_ref, o_ref, acc_ref, *, alpha, d_ff, chunk):
    x = x_ref[...]
    xb = x.astype(BF16)
    for c in range(d_ff // chunk):
        lo = c * chunk
        gate = jnp.dot(xb, win_ref[:, lo:lo + chunk], preferred_element_type=F32)
        up = jnp.dot(xb, win_ref[:, d_ff + lo:d_ff + lo + chunk], preferred_element_type=F32)
        hid = (gate * jax.nn.sigmoid(gate) * up).astype(BF16)
        part = jnp.dot(hid, wout_ref[lo:lo + chunk, :], preferred_element_type=F32)
        if c == 0:
            acc_ref[...] = part
        else:
            acc_ref[...] += part
    y = alpha * x + 0.5 * acc_ref[...]
    o_ref[...] = _layer_norm(y, g_ref[...], b_ref[...])


def _ffn_ln(x, w_in, w_out, g, b, l, j, n, *, alpha, tm=512, chunk=256):
    t, d = x.shape
    d_ff = w_out.shape[-2]
    return pl.pallas_call(
        functools.partial(_ffn_kernel, alpha=alpha, d_ff=d_ff, chunk=chunk),
        grid=(t // tm,),
        in_specs=[pl.BlockSpec((tm, d), lambda i: (i, 0)),
                  _resident(w_in, l, j), _resident(w_out, l, j),
                  _resident(g, l, n), _resident(b, l, n)],
        out_specs=pl.BlockSpec((tm, d), lambda i: (i, 0)),
        out_shape=jax.ShapeDtypeStruct((t, d), F32),
        scratch_shapes=[pltpu.VMEM((tm, d), F32)],
        compiler_params=_params("parallel"),
        name="ffn_ln",
    )(x, w_in, w_out, g, b)


def _rope_slab(t, cos, sin_lo, sin_hi):
    nxt = pltpu.roll(t, LANES - HEAD_DIM // 2, axis=1)
    prv = pltpu.roll(t, HEAD_DIM // 2, axis=1)
    return t * cos + nxt * sin_lo + prv * sin_hi


def _split3(x):
    hi = x.astype(BF16).astype(F32)
    mid = (x - hi).astype(BF16).astype(F32)
    return hi, mid, x - hi - mid


def _store_fox_operands(cum, q, k, v, qa_ref, ka_ref, va_ref, rows):
    lane = lax.broadcasted_iota(jnp.int32, (1, LANES), 1)
    is_head = lane < HEAD_DIM
    aug = lane - HEAD_DIM
    for h in range(q.shape[-1] // HEAD_DIM):
        slab = slice((h // HEADS_PER_SLAB) * LANES, (h // HEADS_PER_SLAB + 1) * LANES)
        odd = h % HEADS_PER_SLAB == 1
        head_lanes = lambda x: pltpu.roll(x[:, slab], HEAD_DIM, axis=1) if odd else x[:, slab]
        cc = cum[:, slab] if odd else pltpu.roll(cum[:, slab], HEAD_DIM, axis=1)
        hi, mid, lo = _split3(cc)
        q_aug = jnp.where(aug == 0, hi, jnp.where(aug == 1, mid, jnp.where(aug == 2, lo,
                          jnp.where(aug < 6, 1.0, 0.0))))
        k_aug = jnp.where(aug < 3, 1.0, jnp.where(aug == 3, -hi, jnp.where(aug == 4, -mid,
                          jnp.where(aug == 5, -lo, 0.0))))
        qa_ref[h, rows, :] = jnp.where(is_head, head_lanes(q), q_aug).astype(BF16)
        ka_ref[h, rows, :] = jnp.where(is_head, head_lanes(k), k_aug).astype(BF16)
        va_ref[h, rows, :] = jnp.where(is_head, head_lanes(v), 1.0).astype(BF16)


def _proj_kernel(h_ref, w_ref, fb_ref, cos_ref, slo_ref, shi_ref,
                 u_ref, qa_ref, ka_ref, va_ref, dq_ref, dk_ref, dv_ref, carry_ref,
                 *, cw, fw, dw, nsb, blk):
    @pl.when(pl.program_id(0) % nsb == 0)
    def _():
        carry_ref[...] = jnp.zeros_like(carry_ref)

    hb = h_ref[...].astype(BF16)

    def cols(lo, width):
        return jnp.dot(hb, w_ref[:, lo:lo + width], preferred_element_type=F32)

    scale = SCORE_SCALE
    val = cols(0, cw)
    gate = cols(cw, cw)
    u_ref[...] = val * jax.nn.sigmoid(gate)
    base = 2 * cw
    fq = cols(base, fw) * scale
    fk = cols(base + fw, fw)
    fv = cols(base + 2 * fw, fw)
    log_f = jax.nn.log_sigmoid(cols(base + 3 * fw, fw) + fb_ref[...])

    r = lax.broadcasted_iota(jnp.int32, (blk, blk), 0)
    c = lax.broadcasted_iota(jnp.int32, (blk, blk), 1)
    tri = (c <= r).astype(BF16)
    cum = carry_ref[...]
    for r0 in range(0, hb.shape[0], blk):
        rows = slice(r0, r0 + blk)
        cum = cum[-1:, :]
        for part in _split3(log_f[rows]):
            cum = cum + jnp.dot(tri, part.astype(BF16), preferred_element_type=F32)
        _store_fox_operands(cum * LOG2_E, fq[rows], fk[rows], fv[rows], qa_ref, ka_ref, va_ref, rows)
    carry_ref[...] = cum[-1:, :]

    base = base + 4 * fw
    cos, slo, shi = cos_ref[...], slo_ref[...], shi_ref[...]
    q, k = cols(base, dw), cols(base + dw, dw)
    for j in range(dw // LANES):
        sl = slice(j * LANES, (j + 1) * LANES)
        dq_ref[:, sl] = _rope_slab(q[:, sl], cos, slo, shi) * scale
        dk_ref[:, sl] = _rope_slab(k[:, sl], cos, slo, shi)
    dv_ref[...] = cols(base + 2 * dw, dw)


def _mixer_proj(h, w, fb, cos, slo, shi, l, *, bsz, cw, fw, dw, tm=512, blk=128):
    t, d = h.shape
    seq = t // bsz
    nsb = seq // tm
    heads = fw // HEAD_DIM
    row = lambda i: (i, 0)
    pos = lambda i: (i % nsb, 0)
    tok = lambda n: (pl.BlockSpec((tm, n), row), jax.ShapeDtypeStruct((t, n), F32))
    fox = (pl.BlockSpec((None, heads, tm, LANES), lambda i: (i // nsb, 0, i % nsb, 0)),
           jax.ShapeDtypeStruct((bsz, heads, seq, LANES), BF16))
    outs = [tok(cw), fox, fox, fox, tok(dw), tok(dw), tok(dw)]
    return pl.pallas_call(
        functools.partial(_proj_kernel, cw=cw, fw=fw, dw=dw, nsb=nsb, blk=blk),
        grid=(t // tm,),
        in_specs=[pl.BlockSpec((tm, d), row), _resident(w, l), _resident(fb, l),
                  pl.BlockSpec((tm, LANES), pos), pl.BlockSpec((tm, LANES), pos),
                  pl.BlockSpec((tm, LANES), pos)],
        out_specs=[spec for spec, _ in outs],
        out_shape=[shape for _, shape in outs],
        scratch_shapes=[pltpu.VMEM((1, fw), F32)],
        compiler_params=_params("arbitrary"),
        name="mixer_proj",
    )(h, w, fb, cos, slo, shi)


def _conv_kernel(u_ref, w_ref, cb_ref, g_ref, b_ref, o_ref, pad_ref, *, rows, halo):
    seq, ch = u_ref.shape
    pad_ref[0:halo, :] = jnp.zeros((halo, ch), F32)
    pad_ref[halo:halo + seq, :] = u_ref[...]
    shift = halo - (CONV_K - 1)

    span = rows + halo
    sub = 8

    slabs = [slice(c0, c0 + LANES) for c0 in range(0, ch, LANES)]

    def body(i, _):
        t0 = pl.multiple_of(i * rows, rows)
        accs = []
        for cs in slabs:
            win = pad_ref[pl.ds(t0, span), cs]
            parts = []
            for b in range(sub):
                rolled = win if b == 0 else pltpu.roll(win, span - b, axis=0)
                taps = [a * sub + b - shift for a in range(halo // sub + 1)]
                prods = [w_ref[k:k + 1, cs] * rolled[k + shift - b:k + shift - b + rows, :]
                         for k in taps if 0 <= k < CONV_K]
                parts.append(functools.reduce(jnp.add, prods))
            while len(parts) > 1:
                parts = [parts[i] + parts[i + 1] for i in range(0, len(parts), 2)]
            accs.append(parts[0] + cb_ref[:, cs])
        mu = sum(jnp.sum(a, axis=-1, keepdims=True) for a in accs) / ch
        cen = [a - mu for a in accs]
        var = sum(jnp.sum(c * c, axis=-1, keepdims=True) for c in cen) / ch
        inv = lax.rsqrt(var + LN_EPS)
        for cs, c in zip(slabs, cen):
            y = c * inv * g_ref[:, cs] + b_ref[:, cs]
            o_ref[pl.ds(t0, rows), cs] = (y * jax.nn.sigmoid(y)).astype(o_ref.dtype)
        return 0

    lax.fori_loop(0, seq // rows, body, 0, unroll=4)


def _conv_module(u, w, cb, g, b, l, *, rows=64, halo=32):
    bsz, seq, ch = u.shape
    return pl.pallas_call(
        functools.partial(_conv_kernel, rows=rows, halo=halo),
        grid=(bsz,),
        in_specs=[pl.BlockSpec((None, seq, ch), lambda i: (i, 0, 0)),
                  _resident(w, l), _resident(cb, l), _resident(g, l), _resident(b, l)],
        out_specs=pl.BlockSpec((None, seq, ch), lambda i: (i, 0, 0)),
        out_shape=jax.ShapeDtypeStruct((bsz, seq, ch), BF16),
        scratch_shapes=[pltpu.VMEM((halo + seq, ch), F32)],
        compiler_params=_params("parallel"),
        name="conv_module",
    )(u, w, cb, g, b)


def _fox_kernel(q_ref, k_ref, v_ref, o_ref, m_ref, acc_ref, *, tq):
    i = pl.program_id(2)
    rr = lax.broadcasted_iota(jnp.int32, (tq, tq), 0)
    cc = lax.broadcasted_iota(jnp.int32, (tq, tq), 1)
    causal = cc <= rr
    dn = (((1,), (1,)), ((), ()))

    def step(k0, width, masked):
        keys = pl.ds(pl.multiple_of(k0, tq), width)
        for a in range(HEADS_PER_SLAB):
            s = lax.dot_general(q_ref[a], k_ref[a, keys, :], dn, preferred_element_type=F32)
            if masked:
                s = jnp.where(causal, s, NEG_BIG)
            m_old = m_ref[a]
            m_new = jnp.maximum(m_old, jnp.max(s, axis=-1, keepdims=True))
            alpha = jnp.exp2(m_old - m_new)
            p = jnp.exp2(s - pltpu.repeat(m_new, width // LANES, axis=1))
            m_ref[a] = m_new
            acc_ref[a] = alpha * acc_ref[a] + jnp.dot(
                p.astype(BF16), v_ref[a, keys, :], preferred_element_type=F32)

    m_ref[...] = jnp.full(m_ref.shape, NEG_BIG, F32)
    acc_ref[...] = jnp.zeros(acc_ref.shape, F32)
    step(i * tq, tq, True)

    def body(j, _):
        step(j * (2 * tq), 2 * tq, False)
        return 0

    lax.fori_loop(0, i // 2, body, 0)

    @pl.when(i % 2 == 1)
    def _():
        step((i - 1) * tq, tq, False)

    lane = lax.broadcasted_iota(jnp.int32, (1, LANES), 1)
    acc0, acc1 = acc_ref[0], acc_ref[1]
    out = jnp.where(lane < HEAD_DIM,
                    acc0 / pltpu.roll(acc0, HEAD_DIM, axis=1),
                    pltpu.roll(acc1, HEAD_DIM, axis=1) / acc1)
    o_ref[...] = out.astype(o_ref.dtype)


def _fox_attention(qa, ka, va, *, tq=512):
    bsz, heads, seq, _ = qa.shape
    qspec = pl.BlockSpec((None, HEADS_PER_SLAB, tq, LANES), lambda b, p, i: (b, p, i, 0))
    kvspec = pl.BlockSpec((None, HEADS_PER_SLAB, seq, LANES), lambda b, p, i: (b, p, 0, 0))
    return pl.pallas_call(
        functools.partial(_fox_kernel, tq=tq),
        grid=(bsz, heads // HEADS_PER_SLAB, seq // tq),
        in_specs=[qspec, kvspec, kvspec],
        out_specs=pl.BlockSpec((None, tq, LANES), lambda b, p, i: (b, i, p)),
        out_shape=jax.ShapeDtypeStruct((bsz, seq, heads * HEAD_DIM), BF16),
        scratch_shapes=[pltpu.VMEM((HEADS_PER_SLAB, tq, LANES), F32),
                        pltpu.VMEM((HEADS_PER_SLAB, tq, LANES), F32)],
        compiler_params=_params("parallel", "parallel", "arbitrary"),
        name="fox_attention",
    )(qa, ka, va)


def _dil_kernel(q_ref, k_ref, v_ref, o_ref, bias_ref, bias2_ref, m_ref, l_ref, acc_ref, *,
                blk, unroll, rows_out):
    seq = q_ref.shape[0]
    win = 2 * blk
    lane = lax.broadcasted_iota(jnp.int32, (1, LANES), 1)
    head0 = lane < HEAD_DIM
    dist0 = (lax.broadcasted_iota(jnp.int32, (win, win), 0) % blk
             - lax.broadcasted_iota(jnp.int32, (win, win), 1))
    row = lax.broadcasted_iota(jnp.int32, (2 * win, win), 0)
    col = lax.broadcasted_iota(jnp.int32, (2 * win, win), 1)
    bias_ref[0] = jnp.where(dist0 >= 0, 0.0, NEG_BIG)
    bias_ref[1] = jnp.where(jnp.logical_and(dist0 + blk >= 0, dist0 <= 0), 0.0, NEG_BIG)
    dist_all = (row % win) - col
    bias2_ref[...] = jnp.where(jnp.logical_and(dist_all >= 0, dist_all <= blk), 0.0, NEG_BIG)
    ones = jnp.ones((win, LANES), BF16)
    dn = (((1,), (1,)), ((), ()))

    def attend(bi, rows, wrows, bias):
        nq = rows.size
        qb = q_ref[rows, :].astype(BF16)
        zero = jnp.zeros_like(qb)
        qs = jnp.concatenate([jnp.where(head0, qb, zero), jnp.where(head0, zero, qb)], axis=0)
        kw = k_ref[wrows, :].astype(BF16)
        vw = jnp.concatenate([v_ref[wrows, :].astype(BF16), ones], axis=1)
        s = lax.dot_general(qs, kw, dn, preferred_element_type=F32) + bias
        m = jnp.max(s, axis=-1, keepdims=True)
        p = jnp.exp2(s - m).astype(BF16)
        pv = jnp.dot(p, vw, preferred_element_type=F32)
        mb = jnp.broadcast_to(m, (2 * nq, LANES))
        m_ref[bi, rows, :] = jnp.where(head0, mb[:nq], mb[nq:])
        acc_ref[bi, rows, :] = jnp.where(head0, pv[:nq, :LANES], pv[nq:, :LANES])
        l_ref[bi, rows, :] = jnp.where(head0, pv[:nq, LANES:], pv[nq:, LANES:])

    for bi, (_, dil) in enumerate(DIL_CONFIGS):
        nb = seq // (dil * blk)
        ds = (lambda start, size, dil=dil: pl.ds(start, size, stride=dil) if dil > 1 else pl.ds(start, size))

        if nb == 2:
            def body(r, _, bi=bi, ds=ds):
                attend(bi, ds(r, win), ds(r, win), bias2_ref[...])
                return 0

            lax.fori_loop(0, dil, body, 0, unroll=unroll // 2)
            continue

        def body(idx, _, bi=bi, dil=dil, nb=nb, ds=ds):
            r = idx // nb
            n = idx % nb
            start = r + n * (dil * blk)
            has_prev = (n > 0).astype(jnp.int32)
            attend(bi, ds(start, blk), ds(start - has_prev * (dil * blk), win), bias_ref[has_prev])
            return 0

        lax.fori_loop(0, seq // blk, body, 0, unroll=unroll)

    def merge(i, _):
        rows = pl.ds(pl.multiple_of(i * rows_out, rows_out), rows_out)
        ms = [m_ref[bi, rows, :] for bi in range(len(DIL_CONFIGS))]
        m_all = functools.reduce(jnp.maximum, ms)
        ws = [jnp.exp2(m - m_all) for m in ms]
        num = sum(w * acc_ref[bi, rows, :] for bi, w in enumerate(ws))
        den = sum(w * l_ref[bi, rows, :] for bi, w in enumerate(ws))
        o_ref[rows, :] = (num / den).astype(o_ref.dtype)
        return 0

    lax.fori_loop(0, seq // rows_out, merge, 0)


def _dilated_attention(q, k, v, *, blk=128, unroll=8, rows_out=256):
    bsz, seq, width = q.shape
    nbr = len(DIL_CONFIGS)
    spec = pl.BlockSpec((None, seq, LANES), lambda b, p: (b, 0, p))
    return pl.pallas_call(
        functools.partial(_dil_kernel, blk=blk, unroll=unroll, rows_out=rows_out),
        grid=(bsz, width // LANES),
        in_specs=[spec, spec, spec],
        out_specs=spec,
        out_shape=jax.ShapeDtypeStruct((bsz, seq, width), BF16),
        scratch_shapes=[pltpu.VMEM((2, 2 * blk, 2 * blk), F32), pltpu.VMEM((4 * blk, 2 * blk), F32)]
        + [pltpu.VMEM((nbr, seq, LANES), F32)] * 3,
        compiler_params=_params("parallel", "parallel"),
        name="dilated_attention",
    )(q, k, v)


def _out_kernel(h_ref, ya_ref, yb_ref, yc_ref, w_ref, g_ref, b_ref, o_ref, *, alpha, cw, fw):
    y = jnp.dot(ya_ref[...], w_ref[0:cw, :], preferred_element_type=F32)
    y = y + jnp.dot(yb_ref[...], w_ref[cw:cw + fw, :], preferred_element_type=F32)
    y = y + jnp.dot(yc_ref[...], w_ref[cw + fw:, :], preferred_element_type=F32)
    o_ref[...] = _layer_norm(alpha * h_ref[...] + y, g_ref[...], b_ref[...])


def _out_proj_ln(h, ya, yb, yc, w, g, b, l, n, *, alpha, tm=512):
    t, d = h.shape
    row = lambda i: (i, 0)
    return pl.pallas_call(
        functools.partial(_out_kernel, alpha=alpha, cw=ya.shape[1], fw=yb.shape[1]),
        grid=(t // tm,),
        in_specs=[pl.BlockSpec((tm, d), row), pl.BlockSpec((tm, ya.shape[1]), row),
                  pl.BlockSpec((tm, yb.shape[1]), row), pl.BlockSpec((tm, yc.shape[1]), row),
                  _resident(w, l), _resident(g, l, n), _resident(b, l, n)],
        out_specs=pl.BlockSpec((tm, d), row),
        out_shape=jax.ShapeDtypeStruct((t, d), F32),
        compiler_params=_params("parallel"),
        name="out_proj_ln",
    )(h, ya, yb, yc, w, g, b)


def _rope_tables(seq):
    inv = 1.0 / (ROPE_THETA ** (jnp.arange(0, HEAD_DIM, 2, dtype=F32) / HEAD_DIM))
    ang = jnp.arange(seq, dtype=F32)[:, None] * inv[None, :]
    ang = jnp.concatenate([ang, ang], axis=-1)
    cos, sin = jnp.cos(ang), jnp.sin(ang)
    first_half = (jnp.arange(HEAD_DIM) < HEAD_DIM // 2)[None, :]
    sin_lo = jnp.where(first_half, -sin, 0.0)
    sin_hi = jnp.where(first_half, 0.0, sin)
    rep = lambda a: jnp.tile(a, (1, HEADS_PER_SLAB))
    return rep(cos), rep(sin_lo), rep(sin_hi)


def kernel(x, w_in, w_o, forget_bias, conv_w, conv_b, conv_ln_g, conv_ln_b, ffn_w_in, ffn_w_out, ln_g, ln_b):
    bsz, seq, d = x.shape
    depth = w_in.shape[0]
    cw = conv_w.shape[-1]
    heads_f = forget_bias.shape[-1]
    fw = heads_f * HEAD_DIM
    dw = d - cw - fw
    alpha = (2 * depth) ** 0.25
    t = bsz * seq
    cos, slo, shi = _rope_tables(seq)

    ffn_in = ffn_w_in.astype(BF16)
    ffn_out = ffn_w_out.astype(BF16)
    w_out = w_o.astype(BF16)
    o_f = 2 * cw + 3 * fw
    w_mix = jnp.concatenate(
        [w_in[:, :, :o_f], jnp.repeat(w_in[:, :, o_f:o_f + heads_f], HEAD_DIM, axis=2),
         w_in[:, :, o_f + heads_f:]], axis=2).astype(BF16)
    fb = jnp.repeat(forget_bias, HEAD_DIM, axis=1)[:, None, :]
    row3 = lambda a: a[:, None, :]
    cb, cg, cbeta = row3(conv_b), row3(conv_ln_g), row3(conv_ln_b)
    g4, b4 = ln_g[:, :, None, :], ln_b[:, :, None, :]
    b3 = lambda a: a.reshape(bsz, seq, a.shape[-1])

    h = x.reshape(t, d)
    for l in range(depth):
        h = _ffn_ln(h, ffn_in, ffn_out, g4, b4, l, 0, 0, alpha=alpha)
        u, qa, ka, va, dq, dk, dv = _mixer_proj(h, w_mix, fb, cos, slo, shi, l,
                                                bsz=bsz, cw=cw, fw=fw, dw=dw)
        ya = _conv_module(b3(u), conv_w, cb, cg, cbeta, l)
        yb = _fox_attention(qa, ka, va)
        yc = _dilated_attention(b3(dq), b3(dk), b3(dv))
        h = _out_proj_ln(h, ya.reshape(t, cw), yb.reshape(t, fw), yc.reshape(t, dw),
                         w_out, g4, b4, l, 1, alpha=alpha)
        h = _ffn_ln(h, ffn_in, ffn_out, g4, b4, l, 1, 2, alpha=alpha)
    return h.reshape(bsz, seq, d)
```

```python
import functools

import jax
import jax.numpy as jnp
from jax import lax
from jax.experimental import pallas as pl
from jax.experimental.pallas import tpu as pltpu

HEAD_DIM = 64
CONV_K = 31
DIL_CONFIGS = ((128, 1), (512, 4), (2048, 16))
ROPE_THETA = 10000.0
LN_EPS = 1e-5
LANES = 128
HEADS_PER_SLAB = LANES // HEAD_DIM
NEG_BIG = -1e30
LOG2_E = 1.4426950408889634
SCORE_SCALE = HEAD_DIM ** -0.5 * LOG2_E
VMEM_LIMIT = 56 * 1024 * 1024

F32 = jnp.float32
BF16 = jnp.bfloat16


def _layer_norm(y, g, b):
    mu = jnp.mean(y, axis=-1, keepdims=True)
    yc = y - mu
    var = jnp.mean(yc * yc, axis=-1, keepdims=True)
    return yc * lax.rsqrt(var + LN_EPS) * g + b


def _params(*sem):
    return pltpu.CompilerParams(dimension_semantics=sem, vmem_limit_bytes=VMEM_LIMIT)


def _resident(arr, *lead):
    rest = arr.shape[len(lead):]
    index = tuple(lead) + (0,) * len(rest)
    return pl.BlockSpec((None,) * len(lead) + rest, lambda *_: index, pipeline_mode=pl.Buffered(1))


def _ffn_kernel(x_ref, win_ref, wout_ref, g_ref, b_ref, o_ref, acc_ref, *, alpha, d_ff, chunk):
    x = x_ref[...]
    xb = x.astype(BF16)
    for c in range(d_ff // chunk):
        lo = c * chunk
        gate = jnp.dot(xb, win_ref[:, lo:lo + chunk], preferred_element_type=F32)
        up = jnp.dot(xb, win_ref[:, d_ff + lo:d_ff + lo + chunk], preferred_element_type=F32)
        hid = (gate * jax.nn.sigmoid(gate) * up).astype(BF16)
        part = jnp.dot(hid, wout_ref[lo:lo + chunk, :], preferred_element_type=F32)
        if c == 0:
            acc_ref[...] = part
        else:
            acc_ref[...] += part
    y = alpha * x + 0.5 * acc_ref[...]
    o_ref[...] = _layer_norm(y, g_ref[...], b_ref[...])


def _ffn_ln(x, w_in, w_out, g, b, l, j, n, *, alpha, tm=1024, chunk=256):
    t, d = x.shape
    d_ff = w_out.shape[-2]
    return pl.pallas_call(
        functools.partial(_ffn_kernel, alpha=alpha, d_ff=d_ff, chunk=chunk),
        grid=(t // tm,),
        in_specs=[pl.BlockSpec((tm, d), lambda i: (i, 0)),
                  _resident(w_in, l, j), _resident(w_out, l, j),
                  _resident(g, l, n), _resident(b, l, n)],
        out_specs=pl.BlockSpec((tm, d), lambda i: (i, 0)),
        out_shape=jax.ShapeDtypeStruct((t, d), F32),
        scratch_shapes=[pltpu.VMEM((tm, d), F32)],
        compiler_params=_params("parallel"),
        name="ffn_ln",
    )(x, w_in, w_out, g, b)


def _rope_slab(t, cos, sin_lo, sin_hi):
    nxt = pltpu.roll(t, LANES - HEAD_DIM // 2, axis=1)
    prv = pltpu.roll(t, HEAD_DIM // 2, axis=1)
    return t * cos + nxt * sin_lo + prv * sin_hi


def _split3(x):
    hi = x.astype(BF16).astype(F32)
    mid = (x - hi).astype(BF16).astype(F32)
    return hi, mid, x - hi - mid


def _store_fox_operands(cum, q, k, v, qa_ref, ka_ref, va_ref, rows):
    lane = lax.broadcasted_iota(jnp.int32, (1, LANES), 1)
    is_head = lane < HEAD_DIM
    aug = lane - HEAD_DIM
    for h in range(q.shape[-1] // HEAD_DIM):
        slab = slice((h // HEADS_PER_SLAB) * LANES, (h // HEADS_PER_SLAB + 1) * LANES)
        odd = h % HEADS_PER_SLAB == 1
        head_lanes = lambda x: pltpu.roll(x[:, slab], HEAD_DIM, axis=1) if odd else x[:, slab]
        cc = cum[:, slab] if odd else pltpu.roll(cum[:, slab], HEAD_DIM, axis=1)
        hi, mid, lo = _split3(cc)
        q_aug = jnp.where(aug == 0, hi, jnp.where(aug == 1, mid, jnp.where(aug == 2, lo,
                          jnp.where(aug < 6, 1.0, 0.0))))
        k_aug = jnp.where(aug < 3, 1.0, jnp.where(aug == 3, -hi, jnp.where(aug == 4, -mid,
                          jnp.where(aug == 5, -lo, 0.0))))
        qa_ref[h, rows, :] = jnp.where(is_head, head_lanes(q), q_aug).astype(BF16)
        ka_ref[h, rows, :] = jnp.where(is_head, head_lanes(k), k_aug).astype(BF16)
        va_ref[h, rows, :] = jnp.where(is_head, head_lanes(v), 1.0).astype(BF16)


def _stage_mixer_weights(wraw_ref, w_ref, *, split, heads, rows):
    lane = lax.broadcasted_iota(jnp.int32, (1, LANES), 1)
    tail = wraw_ref.shape[1] - split - heads
    for r0 in range(0, wraw_ref.shape[0], rows):
        rs = slice(r0, r0 + rows)
        w_ref[rs, 0:split] = wraw_ref[rs, 0:split].astype(BF16)
        gates = wraw_ref[rs, split:split + heads]
        for p in range(heads // HEADS_PER_SLAB):
            even = jnp.broadcast_to(gates[:, HEADS_PER_SLAB * p:HEADS_PER_SLAB * p + 1], (rows, LANES))
            odd = jnp.broadcast_to(gates[:, HEADS_PER_SLAB * p + 1:HEADS_PER_SLAB * p + 2], (rows, LANES))
            w_ref[rs, split + p * LANES:split + (p + 1) * LANES] = (
                jnp.where(lane < HEAD_DIM, even, odd).astype(BF16))
        lo = split + heads * HEAD_DIM
        w_ref[rs, lo:lo + tail] = wraw_ref[rs, split + heads:split + heads + tail].astype(BF16)


def _proj_kernel(h_ref, wraw_ref, fb_ref, cos_ref, slo_ref, shi_ref,
                 u_ref, qa_ref, ka_ref, va_ref, dq_ref, dk_ref, dv_ref, w_ref, carry_ref,
                 *, cw, fw, dw, nsb, blk):
    @pl.when(pl.program_id(0) == 0)
    def _():
        _stage_mixer_weights(wraw_ref, w_ref, split=2 * cw + 3 * fw, heads=fw // HEAD_DIM, rows=blk)

    @pl.when(pl.program_id(0) % nsb == 0)
    def _():
        carry_ref[...] = jnp.zeros_like(carry_ref)

    hb = h_ref[...].astype(BF16)

    def cols(lo, width):
        return jnp.dot(hb, w_ref[:, lo:lo + width], preferred_element_type=F32)

    scale = SCORE_SCALE
    val = cols(0, cw)
    gate = cols(cw, cw)
    u_ref[...] = val * jax.nn.sigmoid(gate)
    base = 2 * cw
    fq = cols(base, fw) * scale
    fk = cols(base + fw, fw)
    fv = cols(base + 2 * fw, fw)
    log_f = jax.nn.log_sigmoid(cols(base + 3 * fw, fw) + fb_ref[...])

    r = lax.broadcasted_iota(jnp.int32, (blk, blk), 0)
    c = lax.broadcasted_iota(jnp.int32, (blk, blk), 1)
    tri = (c <= r).astype(BF16)
    cum = carry_ref[...]
    for r0 in range(0, hb.shape[0], blk):
        rows = slice(r0, r0 + blk)
        cum = cum[-1:, :]
        for part in _split3(log_f[rows]):
            cum = cum + jnp.dot(tri, part.astype(BF16), preferred_element_type=F32)
        _store_fox_operands(cum * LOG2_E, fq[rows], fk[rows], fv[rows], qa_ref, ka_ref, va_ref, rows)
    carry_ref[...] = cum[-1:, :]

    base = base + 4 * fw
    cos, slo, shi = cos_ref[...], slo_ref[...], shi_ref[...]
    q, k = cols(base, dw), cols(base + dw, dw)
    for j in range(dw // LANES):
        sl = slice(j * LANES, (j + 1) * LANES)
        dq_ref[:, sl] = _rope_slab(q[:, sl], cos, slo, shi) * scale
        dk_ref[:, sl] = _rope_slab(k[:, sl], cos, slo, shi)
    dv_ref[...] = cols(base + 2 * dw, dw)


def _mixer_proj(h, w, fb, cos, slo, shi, l, *, bsz, cw, fw, dw, tm=512, blk=128):
    t, d = h.shape
    seq = t // bsz
    nsb = seq // tm
    heads = fw // HEAD_DIM
    row = lambda i: (i, 0)
    pos = lambda i: (i % nsb, 0)
    tok = lambda n: (pl.BlockSpec((tm, n), row), jax.ShapeDtypeStruct((t, n), F32))
    fox = (pl.BlockSpec((None, heads, tm, LANES), lambda i: (i // nsb, 0, i % nsb, 0)),
           jax.ShapeDtypeStruct((bsz, heads, seq, LANES), BF16))
    outs = [tok(cw), fox, fox, fox, tok(dw), tok(dw), tok(dw)]
    return pl.pallas_call(
        functools.partial(_proj_kernel, cw=cw, fw=fw, dw=dw, nsb=nsb, blk=blk),
        grid=(t // tm,),
        in_specs=[pl.BlockSpec((tm, d), row), _resident(w, l), _resident(fb, l),
                  pl.BlockSpec((tm, LANES), pos), pl.BlockSpec((tm, LANES), pos),
                  pl.BlockSpec((tm, LANES), pos)],
        out_specs=[spec for spec, _ in outs],
        out_shape=[shape for _, shape in outs],
        scratch_shapes=[pltpu.VMEM((d, 2 * cw + 4 * fw + 3 * dw), BF16), pltpu.VMEM((1, fw), F32)],
        compiler_params=_params("arbitrary"),
        name="mixer_proj",
    )(h, w, fb, cos, slo, shi)


def _conv_kernel(u_ref, w_ref, cb_ref, g_ref, b_ref, o_ref, pad_ref, *, rows, halo):
    seq, ch = u_ref.shape
    pad_ref[0:halo, :] = jnp.zeros((halo, ch), F32)
    pad_ref[halo:halo + seq, :] = u_ref[...]
    shift = halo - (CONV_K - 1)

    span = rows + halo
    sub = 8

    slabs = [slice(c0, c0 + LANES) for c0 in range(0, ch, LANES)]

    def body(i, _):
        t0 = pl.multiple_of(i * rows, rows)
        accs = []
        for cs in slabs:
            win = pad_ref[pl.ds(t0, span), cs]
            parts = []
            for b in range(sub):
                rolled = win if b == 0 else pltpu.roll(win, span - b, axis=0)
                taps = [a * sub + b - shift for a in range(halo // sub + 1)]
                prods = [w_ref[k:k + 1, cs] * rolled[k + shift - b:k + shift - b + rows, :]
                         for k in taps if 0 <= k < CONV_K]
                parts.append(functools.reduce(jnp.add, prods))
            while len(parts) > 1:
                parts = [parts[i] + parts[i + 1] for i in range(0, len(parts), 2)]
            accs.append(parts[0] + cb_ref[:, cs])
        mu = sum(jnp.sum(a, axis=-1, keepdims=True) for a in accs) / ch
        cen = [a - mu for a in accs]
        var = sum(jnp.sum(c * c, axis=-1, keepdims=True) for c in cen) / ch
        inv = lax.rsqrt(var + LN_EPS)
        for cs, c in zip(slabs, cen):
            y = c * inv * g_ref[:, cs] + b_ref[:, cs]
            o_ref[pl.ds(t0, rows), cs] = (y * jax.nn.sigmoid(y)).astype(o_ref.dtype)
        return 0

    lax.fori_loop(0, seq // rows, body, 0, unroll=4)


def _conv_module(u, w, cb, g, b, l, *, rows=64, halo=32):
    bsz, seq, ch = u.shape
    return pl.pallas_call(
        functools.partial(_conv_kernel, rows=rows, halo=halo),
        grid=(bsz,),
        in_specs=[pl.BlockSpec((None, seq, ch), lambda i: (i, 0, 0)),
                  _resident(w, l), _resident(cb, l), _resident(g, l), _resident(b, l)],
        out_specs=pl.BlockSpec((None, seq, ch), lambda i: (i, 0, 0)),
        out_shape=jax.ShapeDtypeStruct((bsz, seq, ch), BF16),
        scratch_shapes=[pltpu.VMEM((halo + seq, ch), F32)],
        compiler_params=_params("parallel"),
        name="conv_module",
    )(u, w, cb, g, b)


def _fox_kernel(q_ref, k_ref, v_ref, o_ref, m_ref, acc_ref, *, tq):
    i = pl.program_id(2)
    rr = lax.broadcasted_iota(jnp.int32, (tq, tq), 0)
    cc = lax.broadcasted_iota(jnp.int32, (tq, tq), 1)
    causal = cc <= rr
    dn = (((1,), (1,)), ((), ()))

    def step(k0, width, masked):
        keys = pl.ds(pl.multiple_of(k0, tq), width)
        for a in range(HEADS_PER_SLAB):
            s = lax.dot_general(q_ref[a], k_ref[a, keys, :], dn, preferred_element_type=F32)
            if masked:
                s = jnp.where(causal, s, NEG_BIG)
            m_old = m_ref[a]
            m_new = jnp.maximum(m_old, jnp.max(s, axis=-1, keepdims=True))
            alpha = jnp.exp2(m_old - m_new)
            p = jnp.exp2(s - pltpu.repeat(m_new, width // LANES, axis=1))
            m_ref[a] = m_new
            acc_ref[a] = alpha * acc_ref[a] + jnp.dot(
                p.astype(BF16), v_ref[a, keys, :], preferred_element_type=F32)

    m_ref[...] = jnp.full(m_ref.shape, NEG_BIG, F32)
    acc_ref[...] = jnp.zeros(acc_ref.shape, F32)
    step(i * tq, tq, True)

    def body(j, _):
        step(j * (2 * tq), 2 * tq, False)
        return 0

    lax.fori_loop(0, i // 2, body, 0)

    @pl.when(i % 2 == 1)
    def _():
        step((i - 1) * tq, tq, False)

    lane = lax.broadcasted_iota(jnp.int32, (1, LANES), 1)
    acc0, acc1 = acc_ref[0], acc_ref[1]
    out = jnp.where(lane < HEAD_DIM,
                    acc0 / pltpu.roll(acc0, HEAD_DIM, axis=1),
                    pltpu.roll(acc1, HEAD_DIM, axis=1) / acc1)
    o_ref[...] = out.astype(o_ref.dtype)


def _fox_attention(qa, ka, va, *, tq=512):
    bsz, heads, seq, _ = qa.shape
    qspec = pl.BlockSpec((None, HEADS_PER_SLAB, tq, LANES), lambda b, p, i: (b, p, i, 0))
    kvspec = pl.BlockSpec((None, HEADS_PER_SLAB, seq, LANES), lambda b, p, i: (b, p, 0, 0))
    return pl.pallas_call(
        functools.partial(_fox_kernel, tq=tq),
        grid=(bsz, heads // HEADS_PER_SLAB, seq // tq),
        in_specs=[qspec, kvspec, kvspec],
        out_specs=pl.BlockSpec((None, tq, LANES), lambda b, p, i: (b, i, p)),
        out_shape=jax.ShapeDtypeStruct((bsz, seq, heads * HEAD_DIM), BF16),
        scratch_shapes=[pltpu.VMEM((HEADS_PER_SLAB, tq, LANES), F32),
                        pltpu.VMEM((HEADS_PER_SLAB, tq, LANES), F32)],
        compiler_params=_params("parallel", "parallel", "arbitrary"),
        name="fox_attention",
    )(qa, ka, va)


def _dil_kernel(q_ref, k_ref, v_ref, o_ref, bias_ref, bias2_ref, m_ref, l_ref, acc_ref, *,
                blk, unroll, rows_out):
    seq = q_ref.shape[0]
    win = 2 * blk
    lane = lax.broadcasted_iota(jnp.int32, (1, LANES), 1)
    head0 = lane < HEAD_DIM
    dist0 = (lax.broadcasted_iota(jnp.int32, (win, win), 0) % blk
             - lax.broadcasted_iota(jnp.int32, (win, win), 1))
    row = lax.broadcasted_iota(jnp.int32, (2 * win, win), 0)
    col = lax.broadcasted_iota(jnp.int32, (2 * win, win), 1)
    bias_ref[0] = jnp.where(dist0 >= 0, 0.0, NEG_BIG)
    bias_ref[1] = jnp.where(jnp.logical_and(dist0 + blk >= 0, dist0 <= 0), 0.0, NEG_BIG)
    dist_all = (row % win) - col
    bias2_ref[...] = jnp.where(jnp.logical_and(dist_all >= 0, dist_all <= blk), 0.0, NEG_BIG)
    ones = jnp.ones((win, LANES), BF16)
    dn = (((1,), (1,)), ((), ()))

    def attend(bi, rows, wrows, bias):
        nq = rows.size
        qb = q_ref[rows, :].astype(BF16)
        zero = jnp.zeros_like(qb)
        qs = jnp.concatenate([jnp.where(head0, qb, zero), jnp.where(head0, zero, qb)], axis=0)
        kw = k_ref[wrows, :].astype(BF16)
        vw = jnp.concatenate([v_ref[wrows, :].astype(BF16), ones], axis=1)
        s = lax.dot_general(qs, kw, dn, preferred_element_type=F32) + bias
        m = jnp.max(s, axis=-1, keepdims=True)
        p = jnp.exp2(s - m).astype(BF16)
        pv = jnp.dot(p, vw, preferred_element_type=F32)
        mb = jnp.broadcast_to(m, (2 * nq, LANES))
        m_ref[bi, rows, :] = jnp.where(head0, mb[:nq], mb[nq:])
        acc_ref[bi, rows, :] = jnp.where(head0, pv[:nq, :LANES], pv[nq:, :LANES])
        l_ref[bi, rows, :] = jnp.where(head0, pv[:nq, LANES:], pv[nq:, LANES:])

    for bi, (_, dil) in enumerate(DIL_CONFIGS):
        nb = seq // (dil * blk)
        ds = (lambda start, size, dil=dil: pl.ds(start, size, stride=dil) if dil > 1 else pl.ds(start, size))

        if nb == 2:
            def body(r, _, bi=bi, ds=ds):
                attend(bi, ds(r, win), ds(r, win), bias2_ref[...])
                return 0

            lax.fori_loop(0, dil, body, 0, unroll=unroll // 2)
            continue

        def body(idx, _, bi=bi, dil=dil, nb=nb, ds=ds):
            r = idx // nb
            n = idx % nb
            start = r + n * (dil * blk)
            has_prev = (n > 0).astype(jnp.int32)
            attend(bi, ds(start, blk), ds(start - has_prev * (dil * blk), win), bias_ref[has_prev])
            return 0

        lax.fori_loop(0, seq // blk, body, 0, unroll=unroll)

    def merge(i, _):
        rows = pl.ds(pl.multiple_of(i * rows_out, rows_out), rows_out)
        ms = [m_ref[bi, rows, :] for bi in range(len(DIL_CONFIGS))]
        m_all = functools.reduce(jnp.maximum, ms)
        ws = [jnp.exp2(m - m_all) for m in ms]
        num = sum(w * acc_ref[bi, rows, :] for bi, w in enumerate(ws))
        den = sum(w * l_ref[bi, rows, :] for bi, w in enumerate(ws))
        o_ref[rows, :] = (num / den).astype(o_ref.dtype)
        return 0

    lax.fori_loop(0, seq // rows_out, merge, 0)


def _dilated_attention(q, k, v, *, blk=128, unroll=8, rows_out=256):
    bsz, seq, width = q.shape
    nbr = len(DIL_CONFIGS)
    spec = pl.BlockSpec((None, seq, LANES), lambda b, p: (b, 0, p))
    return pl.pallas_call(
        functools.partial(_dil_kernel, blk=blk, unroll=unroll, rows_out=rows_out),
        grid=(bsz, width // LANES),
        in_specs=[spec, spec, spec],
        out_specs=spec,
        out_shape=jax.ShapeDtypeStruct((bsz, seq, width), BF16),
        scratch_shapes=[pltpu.VMEM((2, 2 * blk, 2 * blk), F32), pltpu.VMEM((4 * blk, 2 * blk), F32)]
        + [pltpu.VMEM((nbr, seq, LANES), F32)] * 3,
        compiler_params=_params("parallel", "parallel"),
        name="dilated_attention",
    )(q, k, v)


def _out_kernel(h_ref, ya_ref, yb_ref, yc_ref, wraw_ref, g_ref, b_ref, o_ref, w_ref, *, alpha, cw, fw):
    @pl.when(pl.program_id(0) == 0)
    def _():
        w_ref[...] = wraw_ref[...].astype(BF16)

    y = jnp.dot(ya_ref[...], w_ref[0:cw, :], preferred_element_type=F32)
    y = y + jnp.dot(yb_ref[...], w_ref[cw:cw + fw, :], preferred_element_type=F32)
    y = y + jnp.dot(yc_ref[...], w_ref[cw + fw:, :], preferred_element_type=F32)
    o_ref[...] = _layer_norm(alpha * h_ref[...] + y, g_ref[...], b_ref[...])


def _out_proj_ln(h, ya, yb, yc, w, g, b, l, n, *, alpha, tm=512):
    t, d = h.shape
    row = lambda i: (i, 0)
    return pl.pallas_call(
        functools.partial(_out_kernel, alpha=alpha, cw=ya.shape[1], fw=yb.shape[1]),
        grid=(t // tm,),
        in_specs=[pl.BlockSpec((tm, d), row), pl.BlockSpec((tm, ya.shape[1]), row),
                  pl.BlockSpec((tm, yb.shape[1]), row), pl.BlockSpec((tm, yc.shape[1]), row),
                  _resident(w, l), _resident(g, l, n), _resident(b, l, n)],
        out_specs=pl.BlockSpec((tm, d), row),
        out_shape=jax.ShapeDtypeStruct((t, d), F32),
        scratch_shapes=[pltpu.VMEM((d, d), BF16)],
        compiler_params=_params("arbitrary"),
        name="out_proj_ln",
    )(h, ya, yb, yc, w, g, b)


def _rope_tables(seq):
    inv = 1.0 / (ROPE_THETA ** (jnp.arange(0, HEAD_DIM, 2, dtype=F32) / HEAD_DIM))
    ang = jnp.arange(seq, dtype=F32)[:, None] * inv[None, :]
    ang = jnp.concatenate([ang, ang], axis=-1)
    cos, sin = jnp.cos(ang), jnp.sin(ang)
    first_half = (jnp.arange(HEAD_DIM) < HEAD_DIM // 2)[None, :]
    sin_lo = jnp.where(first_half, -sin, 0.0)
    sin_hi = jnp.where(first_half, 0.0, sin)
    rep = lambda a: jnp.tile(a, (1, HEADS_PER_SLAB))
    return rep(cos), rep(sin_lo), rep(sin_hi)


def kernel(x, w_in, w_o, forget_bias, conv_w, conv_b, conv_ln_g, conv_ln_b, ffn_w_in, ffn_w_out, ln_g, ln_b):
    bsz, seq, d = x.shape
    depth = w_in.shape[0]
    cw = conv_w.shape[-1]
    heads_f = forget_bias.shape[-1]
    fw = heads_f * HEAD_DIM
    dw = d - cw - fw
    alpha = (2 * depth) ** 0.25
    t = bsz * seq
    cos, slo, shi = _rope_tables(seq)

    ffn_in = ffn_w_in.astype(BF16)
    ffn_out = ffn_w_out.astype(BF16)
    fb =jnp.repeat(forget_bias, HEAD_DIM, axis=1)[:, None, :]
    row3 = lambda a: a[:, None, :]
    cb, cg, cbeta = row3(conv_b), row3(conv_ln_g), row3(conv_ln_b)
    g4, b4 = ln_g[:, :, None, :], ln_b[:, :, None, :]
    b3 = lambda a: a.reshape(bsz, seq, a.shape[-1])

    h = x.reshape(t, d)
    for l in range(depth):
        h = _ffn_ln(h, ffn_in, ffn_out, g4, b4, l, 0, 0, alpha=alpha)
        u, qa, ka, va, dq, dk, dv = _mixer_proj(h, w_in, fb, cos, slo, shi, l,
                                                bsz=bsz, cw=cw, fw=fw, dw=dw)
        ya = _conv_module(b3(u), conv_w, cb, cg, cbeta, l)
        yb = _fox_attention(qa, ka, va)
        yc = _dilated_attention(b3(dq), b3(dk), b3(dv))
        h = _out_proj_ln(h, ya.reshape(t, cw), yb.reshape(t, fw), yc.reshape(t, dw),
                         w_o, g4, b4, l, 1, alpha=alpha)
        h = _ffn_ln(h, ffn_in, ffn_out, g4, b4, l, 1, 2, alpha=alpha)
    return h.reshape(bsz, seq, d)
```

```python
import functools

import jax
import jax.numpy as jnp
from jax import lax
from jax.experimental import pallas as pl
from jax.experimental.pallas import tpu as pltpu

HEAD_DIM = 64
CONV_K = 31
DIL_CONFIGS = ((128, 1), (512, 4), (2048, 16))
ROPE_THETA = 10000.0
LN_EPS = 1e-5
LANES = 128
HEADS_PER_SLAB = LANES // HEAD_DIM
NEG_BIG = -1e30
LOG2_E = 1.4426950408889634
SCORE_SCALE = HEAD_DIM ** -0.5 * LOG2_E
VMEM_LIMIT = 56 * 1024 * 1024

F32 = jnp.float32
BF16 = jnp.bfloat16


def _layer_norm(y, g, b):
    mu = jnp.mean(y, axis=-1, keepdims=True)
    yc = y - mu
    var = jnp.mean(yc * yc, axis=-1, keepdims=True)
    return yc * lax.rsqrt(var + LN_EPS) * g + b


def _params(*sem):
    return pltpu.CompilerParams(dimension_semantics=sem, vmem_limit_bytes=VMEM_LIMIT)


def _resident(arr, *lead):
    rest = arr.shape[len(lead):]
    index = tuple(lead) + (0,) * len(rest)
    return pl.BlockSpec((None,) * len(lead) + rest, lambda *_: index, pipeline_mode=pl.Buffered(1))


def _ffn_kernel(x_ref, win_ref, wout_ref, g_ref, b_ref, o_ref, acc_ref, *, alpha, d_ff, chunk):
    x = x_ref[...]
    xb = x.astype(BF16)
    for c in range(d_ff // chunk):
        lo = c * chunk
        gate = jnp.dot(xb, win_ref[:, lo:lo + chunk], preferred_element_type=F32)
        up = jnp.dot(xb, win_ref[:, d_ff + lo:d_ff + lo + chunk], preferred_element_type=F32)
        hid = (gate * jax.nn.sigmoid(gate) * up).astype(BF16)
        part = jnp.dot(hid, wout_ref[lo:lo + chunk, :], preferred_element_type=F32)
        if c == 0:
            acc_ref[...] = part
        else:
            acc_ref[...] += part
    y = alpha * x + 0.5 * acc_ref[...]
    o_ref[...] = _layer_norm(y, g_ref[...], b_ref[...])


def _ffn_ln(x, w_in, w_out, g, b, l, j, n, *, alpha, tm=1024, chunk=256):
    t, d = x.shape
    d_ff = w_out.shape[-2]
    return pl.pallas_call(
        functools.partial(_ffn_kernel, alpha=alpha, d_ff=d_ff, chunk=chunk),
        grid=(t // tm,),
        in_specs=[pl.BlockSpec((tm, d), lambda i: (i, 0)),
                  _resident(w_in, l, j), _resident(w_out, l, j),
                  _resident(g, l, n), _resident(b, l, n)],
        out_specs=pl.BlockSpec((tm, d), lambda i: (i, 0)),
        out_shape=jax.ShapeDtypeStruct((t, d), F32),
        scratch_shapes=[pltpu.VMEM((tm, d), F32)],
        compiler_params=_params("parallel"),
        name="ffn_ln",
    )(x, w_in, w_out, g, b)


def _rope_slab(t, cos, sin_lo, sin_hi):
    nxt = pltpu.roll(t, LANES - HEAD_DIM // 2, axis=1)
    prv = pltpu.roll(t, HEAD_DIM // 2, axis=1)
    return t * cos + nxt * sin_lo + prv * sin_hi


def _split3(x):
    hi = x.astype(BF16).astype(F32)
    mid = (x - hi).astype(BF16).astype(F32)
    return hi, mid, x - hi - mid


def _store_fox_operands(cum, q, k, v, qa_ref, ka_ref, va_ref, rows):
    lane = lax.broadcasted_iota(jnp.int32, (1, LANES), 1)
    is_head = lane < HEAD_DIM
    aug = lane - HEAD_DIM
    for h in range(q.shape[-1] // HEAD_DIM):
        slab = slice((h // HEADS_PER_SLAB) * LANES, (h // HEADS_PER_SLAB + 1) * LANES)
        odd = h % HEADS_PER_SLAB == 1
        head_lanes = lambda x: pltpu.roll(x[:, slab], HEAD_DIM, axis=1) if odd else x[:, slab]
        cc = cum[:, slab] if odd else pltpu.roll(cum[:, slab], HEAD_DIM, axis=1)
        hi, mid, lo = _split3(cc)
        q_aug = jnp.where(aug == 0, hi, jnp.where(aug == 1, mid, jnp.where(aug == 2, lo,
                          jnp.where(aug < 6, 1.0, 0.0))))
        k_aug = jnp.where(aug < 3, 1.0, jnp.where(aug == 3, -hi, jnp.where(aug == 4, -mid,
                          jnp.where(aug == 5, -lo, 0.0))))
        qa_ref[h, rows, :] = jnp.where(is_head, head_lanes(q), q_aug).astype(BF16)
        ka_ref[h, rows, :] = jnp.where(is_head, head_lanes(k), k_aug).astype(BF16)
        va_ref[h, rows, :] = jnp.where(is_head, head_lanes(v), 1.0).astype(BF16)


def _stage_mixer_weights(wraw_ref, w_ref, *, split, heads, rows):
    lane = lax.broadcasted_iota(jnp.int32, (1, LANES), 1)
    tail = wraw_ref.shape[1] - split - heads
    for r0 in range(0, wraw_ref.shape[0], rows):
        rs = slice(r0, r0 + rows)
        w_ref[rs, 0:split] = wraw_ref[rs, 0:split].astype(BF16)
        gates = wraw_ref[rs, split:split + heads]
        for p in range(heads // HEADS_PER_SLAB):
            even = jnp.broadcast_to(gates[:, HEADS_PER_SLAB * p:HEADS_PER_SLAB * p + 1], (rows, LANES))
            odd = jnp.broadcast_to(gates[:, HEADS_PER_SLAB * p + 1:HEADS_PER_SLAB * p + 2], (rows, LANES))
            w_ref[rs, split + p * LANES:split + (p + 1) * LANES] = (
                jnp.where(lane < HEAD_DIM, even, odd).astype(BF16))
        lo = split + heads * HEAD_DIM
        w_ref[rs, lo:lo + tail] = wraw_ref[rs, split + heads:split + heads + tail].astype(BF16)


def _proj_kernel(h_ref, wraw_ref, fb_ref, cos_ref, slo_ref, shi_ref,
                 u_ref, qa_ref, ka_ref, va_ref, dq_ref, dk_ref, dv_ref, w_ref, carry_ref,
                 *, cw, fw, dw, nsb, blk):
    @pl.when(pl.program_id(0) == 0)
    def _():
        _stage_mixer_weights(wraw_ref, w_ref, split=2 * cw + 3 * fw, heads=fw // HEAD_DIM, rows=blk)

    @pl.when(pl.program_id(0) % nsb == 0)
    def _():
        carry_ref[...] = jnp.zeros_like(carry_ref)

    hb = h_ref[...].astype(BF16)

    def cols(lo, width):
        return jnp.dot(hb, w_ref[:, lo:lo + width], preferred_element_type=F32)

    scale = SCORE_SCALE
    val = cols(0, cw)
    gate = cols(cw, cw)
    u_ref[...] = val * jax.nn.sigmoid(gate)
    base = 2 * cw
    fq = cols(base, fw) * scale
    fk = cols(base + fw, fw)
    fv = cols(base + 2 * fw, fw)
    log_f = jax.nn.log_sigmoid(cols(base + 3 * fw, fw) + fb_ref[...])

    r = lax.broadcasted_iota(jnp.int32, (blk, blk), 0)
    c = lax.broadcasted_iota(jnp.int32, (blk, blk), 1)
    tri = (c <= r).astype(BF16)
    cum = carry_ref[...]
    for r0 in range(0, hb.shape[0], blk):
        rows = slice(r0, r0 + blk)
        cum = cum[-1:, :]
        for part in _split3(log_f[rows]):
            cum = cum + jnp.dot(tri, part.astype(BF16), preferred_element_type=F32)
        _store_fox_operands(cum * LOG2_E, fq[rows], fk[rows], fv[rows], qa_ref, ka_ref, va_ref, rows)
    carry_ref[...] = cum[-1:, :]

    base = base + 4 * fw
    cos, slo, shi = cos_ref[...], slo_ref[...], shi_ref[...]
    q, k = cols(base, dw), cols(base + dw, dw)
    for j in range(dw // LANES):
        sl = slice(j * LANES, (j + 1) * LANES)
        dq_ref[:, sl] = _rope_slab(q[:, sl], cos, slo, shi) * scale
        dk_ref[:, sl] = _rope_slab(k[:, sl], cos, slo, shi)
    dv_ref[...] = cols(base + 2 * dw, dw)


def _mixer_proj(h, w, fb, cos, slo, shi, l, *, bsz, cw, fw, dw, tm=512, blk=128):
    t, d = h.shape
    seq = t // bsz
    nsb = seq // tm
    heads = fw // HEAD_DIM
    row = lambda i: (i, 0)
    pos = lambda i: (i % nsb, 0)
    tok = lambda n: (pl.BlockSpec((tm, n), row), jax.ShapeDtypeStruct((t, n), F32))
    fox = (pl.BlockSpec((None, heads, tm, LANES), lambda i: (i // nsb, 0, i % nsb, 0)),
           jax.ShapeDtypeStruct((bsz, heads, seq, LANES), BF16))
    outs = [tok(cw), fox, fox, fox, tok(dw), tok(dw), tok(dw)]
    return pl.pallas_call(
        functools.partial(_proj_kernel, cw=cw, fw=fw, dw=dw, nsb=nsb, blk=blk),
        grid=(t // tm,),
        in_specs=[pl.BlockSpec((tm, d), row), _resident(w, l), _resident(fb, l),
                  pl.BlockSpec((tm, LANES), pos), pl.BlockSpec((tm, LANES), pos),
                  pl.BlockSpec((tm, LANES), pos)],
        out_specs=[spec for spec, _ in outs],
        out_shape=[shape for _, shape in outs],
        scratch_shapes=[pltpu.VMEM((d, 2 * cw + 4 * fw + 3 * dw), BF16), pltpu.VMEM((1, fw), F32)],
        compiler_params=_params("arbitrary"),
        name="mixer_proj",
    )(h, w, fb, cos, slo, shi)


def _conv_kernel(u_ref, w_ref, cb_ref, g_ref, b_ref, o_ref, pad_ref, *, rows, halo):
    seq, ch = u_ref.shape
    pad_ref[0:halo, :] = jnp.zeros((halo, ch), F32)
    pad_ref[halo:halo + seq, :] = u_ref[...]
    shift = halo - (CONV_K - 1)

    span = rows + halo
    sub = 8

    slabs = [slice(c0, c0 + LANES) for c0 in range(0, ch, LANES)]

    def body(i, _):
        t0 = pl.multiple_of(i * rows, rows)
        accs = []
        for cs in slabs:
            win = pad_ref[pl.ds(t0, span), cs]
            parts = []
            for b in range(sub):
                rolled = win if b == 0 else pltpu.roll(win, span - b, axis=0)
                taps = [a * sub + b - shift for a in range(halo // sub + 1)]
                prods = [w_ref[k:k + 1, cs] * rolled[k + shift - b:k + shift - b + rows, :]
                         for k in taps if 0 <= k < CONV_K]
                parts.append(functools.reduce(jnp.add, prods))
            while len(parts) > 1:
                parts = [parts[i] + parts[i + 1] for i in range(0, len(parts), 2)]
            accs.append(parts[0] + cb_ref[:, cs])
        mu = sum(jnp.sum(a, axis=-1, keepdims=True) for a in accs) / ch
        cen = [a - mu for a in accs]
        var = sum(jnp.sum(c * c, axis=-1, keepdims=True) for c in cen) / ch
        inv = lax.rsqrt(var + LN_EPS)
        for cs, c in zip(slabs, cen):
            y = c * inv * g_ref[:, cs] + b_ref[:, cs]
            o_ref[pl.ds(t0, rows), cs] = (y * jax.nn.sigmoid(y)).astype(o_ref.dtype)
        return 0

    lax.fori_loop(0, seq // rows, body, 0, unroll=4)


def _conv_module(u, w, cb, g, b, l, *, rows=64, halo=32):
    bsz, seq, ch = u.shape
    return pl.pallas_call(
        functools.partial(_conv_kernel, rows=rows, halo=halo),
        grid=(bsz,),
        in_specs=[pl.BlockSpec((None, seq, ch), lambda i: (i, 0, 0)),
                  _resident(w, l), _resident(cb, l), _resident(g, l), _resident(b, l)],
        out_specs=pl.BlockSpec((None, seq, ch), lambda i: (i, 0, 0)),
        out_shape=jax.ShapeDtypeStruct((bsz, seq, ch), BF16),
        scratch_shapes=[pltpu.VMEM((halo + seq, ch), F32)],
        compiler_params=_params("parallel"),
        name="conv_module",
    )(u, w, cb, g, b)


def _fox_kernel(q_ref, k_ref, v_ref, o_ref, m_ref, acc_ref, *, tq):
    i = pl.program_id(2)
    half = tq // 2
    dn = (((1,), (1,)), ((), ()))

    def update(a, rows, k0, width, mask):
        keys = pl.ds(pl.multiple_of(k0, tq), width)
        s = lax.dot_general(q_ref[a, rows, :], k_ref[a, keys, :], dn, preferred_element_type=F32)
        if mask is not None:
            s = jnp.where(mask, s, NEG_BIG)
        m_old = m_ref[a, rows, :]
        m_new = jnp.maximum(m_old, jnp.max(s, axis=-1, keepdims=True))
        alpha = jnp.exp2(m_old - m_new)
        p = jnp.exp2(s - pltpu.repeat(m_new, width // LANES, axis=1))
        m_ref[a, rows, :] = m_new
        acc_ref[a, rows, :] = alpha * acc_ref[a, rows, :] + jnp.dot(
            p.astype(BF16), v_ref[a, keys, :], preferred_element_type=F32)

    m_ref[...] = jnp.full(m_ref.shape, NEG_BIG, F32)
    acc_ref[...] = jnp.zeros(acc_ref.shape, F32)
    lower = (lax.broadcasted_iota(jnp.int32, (half, half), 1)
             <= lax.broadcasted_iota(jnp.int32, (half, half), 0))
    upper = (lax.broadcasted_iota(jnp.int32, (half, tq), 1)
             <= lax.broadcasted_iota(jnp.int32, (half, tq), 0) + half)
    for a in range(HEADS_PER_SLAB):
        update(a, slice(0, half), i * tq, half, lower)
        update(a, slice(half, tq), i * tq, tq, upper)

    def body(j, _):
        for a in range(HEADS_PER_SLAB):
            update(a, slice(0, tq), j * tq, tq, None)
        return 0

    lax.fori_loop(0, i, body, 0)

    lane = lax.broadcasted_iota(jnp.int32, (1, LANES), 1)
    acc0, acc1 = acc_ref[0], acc_ref[1]
    out = jnp.where(lane < HEAD_DIM,
                    acc0 / pltpu.roll(acc0, HEAD_DIM, axis=1),
                    pltpu.roll(acc1, HEAD_DIM, axis=1) / acc1)
    o_ref[...] = out.astype(o_ref.dtype)


def _fox_attention(qa, ka, va, *, tq=1024):
    bsz, heads, seq, _ = qa.shape
    qspec = pl.BlockSpec((None, HEADS_PER_SLAB, tq, LANES), lambda b, p, i: (b, p, i, 0))
    kvspec = pl.BlockSpec((None, HEADS_PER_SLAB, seq, LANES), lambda b, p, i: (b, p, 0, 0))
    return pl.pallas_call(
        functools.partial(_fox_kernel, tq=tq),
        grid=(bsz, heads // HEADS_PER_SLAB, seq // tq),
        in_specs=[qspec, kvspec, kvspec],
        out_specs=pl.BlockSpec((None, tq, LANES), lambda b, p, i: (b, i, p)),
        out_shape=jax.ShapeDtypeStruct((bsz, seq, heads * HEAD_DIM), BF16),
        scratch_shapes=[pltpu.VMEM((HEADS_PER_SLAB, tq, LANES), F32),
                        pltpu.VMEM((HEADS_PER_SLAB, tq, LANES), F32)],
        compiler_params=_params("parallel", "parallel", "arbitrary"),
        name="fox_attention",
    )(qa, ka, va)


def _dil_kernel(q_ref, k_ref, v_ref, o_ref, bias_ref, bias2_ref, m_ref, l_ref, acc_ref, *,
                blk, unroll, rows_out):
    seq = q_ref.shape[0]
    win = 2 * blk
    lane = lax.broadcasted_iota(jnp.int32, (1, LANES), 1)
    head0 = lane < HEAD_DIM
    dist0 = (lax.broadcasted_iota(jnp.int32, (win, win), 0) % blk
             - lax.broadcasted_iota(jnp.int32, (win, win), 1))
    row = lax.broadcasted_iota(jnp.int32, (2 * win, win), 0)
    col = lax.broadcasted_iota(jnp.int32, (2 * win, win), 1)
    bias_ref[0] = jnp.where(dist0 >= 0, 0.0, NEG_BIG)
    bias_ref[1] = jnp.where(jnp.logical_and(dist0 + blk >= 0, dist0 <= 0), 0.0, NEG_BIG)
    dist_all = (row % win) - col
    bias2_ref[...] = jnp.where(jnp.logical_and(dist_all >= 0, dist_all <= blk), 0.0, NEG_BIG)
    ones = jnp.ones((win, LANES), BF16)
    dn = (((1,), (1,)), ((), ()))

    def attend(bi, rows, wrows, bias):
        nq = rows.size
        qb = q_ref[rows, :].astype(BF16)
        zero = jnp.zeros_like(qb)
        qs = jnp.concatenate([jnp.where(head0, qb, zero), jnp.where(head0, zero, qb)], axis=0)
        kw = k_ref[wrows, :].astype(BF16)
        vw = jnp.concatenate([v_ref[wrows, :].astype(BF16), ones], axis=1)
        s = lax.dot_general(qs, kw, dn, preferred_element_type=F32) + bias
        m = jnp.max(s, axis=-1, keepdims=True)
        p = jnp.exp2(s - m).astype(BF16)
        pv = jnp.dot(p, vw, preferred_element_type=F32)
        mb = jnp.broadcast_to(m, (2 * nq, LANES))
        m_ref[bi, rows, :] = jnp.where(head0, mb[:nq], mb[nq:])
        acc_ref[bi, rows, :] = jnp.where(head0, pv[:nq, :LANES], pv[nq:, :LANES])
        l_ref[bi, rows, :] = jnp.where(head0, pv[:nq, LANES:], pv[nq:, LANES:])

    for bi, (_, dil) in enumerate(DIL_CONFIGS):
        nb = seq // (dil * blk)
        ds = (lambda start, size, dil=dil: pl.ds(start, size, stride=dil) if dil > 1 else pl.ds(start, size))

        if nb == 2:
            def body(r, _, bi=bi, ds=ds):
                attend(bi, ds(r, win), ds(r, win), bias2_ref[...])
                return 0

            lax.fori_loop(0, dil, body, 0, unroll=unroll // 2)
            continue

        def body(idx, _, bi=bi, dil=dil, nb=nb, ds=ds):
            r = idx // nb
            n = idx % nb
            start = r + n * (dil * blk)
            has_prev = (n > 0).astype(jnp.int32)
            attend(bi, ds(start, blk), ds(start - has_prev * (dil * blk), win), bias_ref[has_prev])
            return 0

        lax.fori_loop(0, seq // blk, body, 0, unroll=unroll)

    def merge(i, _):
        rows = pl.ds(pl.multiple_of(i * rows_out, rows_out), rows_out)
        ms = [m_ref[bi, rows, :] for bi in range(len(DIL_CONFIGS))]
        m_all = functools.reduce(jnp.maximum, ms)
        ws = [jnp.exp2(m - m_all) for m in ms]
        num = sum(w * acc_ref[bi, rows, :] for bi, w in enumerate(ws))
        den = sum(w * l_ref[bi, rows, :] for bi, w in enumerate(ws))
        o_ref[rows, :] = (num / den).astype(o_ref.dtype)
        return 0

    lax.fori_loop(0, seq // rows_out, merge, 0)


def _dilated_attention(q, k, v, *, blk=128, unroll=8, rows_out=256):
    bsz, seq, width = q.shape
    nbr = len(DIL_CONFIGS)
    spec = pl.BlockSpec((None, seq, LANES), lambda b, p: (b, 0, p))
    return pl.pallas_call(
        functools.partial(_dil_kernel, blk=blk, unroll=unroll, rows_out=rows_out),
        grid=(bsz, width // LANES),
        in_specs=[spec, spec, spec],
        out_specs=spec,
        out_shape=jax.ShapeDtypeStruct((bsz, seq, width), BF16),
        scratch_shapes=[pltpu.VMEM((2, 2 * blk, 2 * blk), F32), pltpu.VMEM((4 * blk, 2 * blk), F32)]
        + [pltpu.VMEM((nbr, seq, LANES), F32)] * 3,
        compiler_params=_params("parallel", "parallel"),
        name="dilated_attention",
    )(q, k, v)


def _out_kernel(h_ref, ya_ref, yb_ref, yc_ref, wraw_ref, g_ref, b_ref, o_ref, w_ref, *,
                alpha, cw, fw, parts):
    @pl.when(pl.program_id(0) == 0)
    def _():
        w_ref[...] = wraw_ref[...].astype(BF16)

    rows_per = h_ref.shape[0] // parts
    for part_idx in range(parts):
        rows = slice(part_idx * rows_per, (part_idx + 1) * rows_per)
        y = jnp.dot(ya_ref[rows, :], w_ref[0:cw, :], preferred_element_type=F32)
        y = y + jnp.dot(yb_ref[rows, :], w_ref[cw:cw + fw, :], preferred_element_type=F32)
        y = y + jnp.dot(yc_ref[rows, :], w_ref[cw + fw:, :], preferred_element_type=F32)
        o_ref[rows, :] = _layer_norm(alpha * h_ref[rows, :] + y, g_ref[...], b_ref[...])


def _out_proj_ln(h, ya, yb, yc, w, g, b, l, n, *, alpha, tm=1024, parts=2):
    t, d = h.shape
    row = lambda i: (i, 0)
    return pl.pallas_call(
        functools.partial(_out_kernel, alpha=alpha, cw=ya.shape[1], fw=yb.shape[1], parts=parts),
        grid=(t // tm,),
        in_specs=[pl.BlockSpec((tm, d), row), pl.BlockSpec((tm, ya.shape[1]), row),
                  pl.BlockSpec((tm, yb.shape[1]), row), pl.BlockSpec((tm, yc.shape[1]), row),
                  _resident(w, l), _resident(g, l, n), _resident(b, l, n)],
        out_specs=pl.BlockSpec((tm, d), row),
        out_shape=jax.ShapeDtypeStruct((t, d), F32),
        scratch_shapes=[pltpu.VMEM((d, d), BF16)],
        compiler_params=_params("arbitrary"),
        name="out_proj_ln",
    )(h, ya, yb, yc, w, g, b)


def _rope_tables(seq):
    inv = 1.0 / (ROPE_THETA ** (jnp.arange(0, HEAD_DIM, 2, dtype=F32) / HEAD_DIM))
    ang = jnp.arange(seq, dtype=F32)[:, None] * inv[None, :]
    ang = jnp.concatenate([ang, ang], axis=-1)
    cos, sin = jnp.cos(ang), jnp.sin(ang)
    first_half = (jnp.arange(HEAD_DIM) < HEAD_DIM // 2)[None, :]
    sin_lo = jnp.where(first_half, -sin, 0.0)
    sin_hi = jnp.where(first_half, 0.0, sin)
    rep = lambda a: jnp.tile(a, (1, HEADS_PER_SLAB))
    return rep(cos), rep(sin_lo), rep(sin_hi)


def kernel(x, w_in, w_o, forget_bias, conv_w, conv_b, conv_ln_g, conv_ln_b, ffn_w_in, ffn_w_out, ln_g, ln_b):
    bsz, seq, d = x.shape
    depth = w_in.shape[0]
    cw = conv_w.shape[-1]
    heads_f = forget_bias.shape[-1]
    fw = heads_f * HEAD_DIM
    dw = d - cw - fw
    alpha = (2 * depth) ** 0.25
    t = bsz * seq
    cos, slo, shi = _rope_tables(seq)

    ffn_in = ffn_w_in.astype(BF16)
    ffn_out = ffn_w_out.astype(BF16)
    fb =jnp.repeat(forget_bias, HEAD_DIM, axis=1)[:, None, :]
    row3 = lambda a: a[:, None, :]
    cb, cg, cbeta = row3(conv_b), row3(conv_ln_g), row3(conv_ln_b)
    g4, b4 = ln_g[:, :, None, :], ln_b[:, :, None, :]
    b3 = lambda a: a.reshape(bsz, seq, a.shape[-1])

    h = x.reshape(t, d)
    for l in range(depth):
        h = _ffn_ln(h, ffn_in, ffn_out, g4, b4, l, 0, 0, alpha=alpha)
        u, qa, ka, va, dq, dk, dv = _mixer_proj(h, w_in, fb, cos, slo, shi, l,
                                                bsz=bsz, cw=cw, fw=fw, dw=dw)
        ya = _conv_module(b3(u), conv_w, cb, cg, cbeta, l)
        yb = _fox_attention(qa, ka, va)
        yc = _dilated_attention(b3(dq), b3(dk), b3(dv))
        h = _out_proj_ln(h, ya.reshape(t, cw), yb.reshape(t, fw), yc.reshape(t, dw),
                         w_o, g4, b4, l, 1, alpha=alpha)
        h = _ffn_ln(h, ffn_in, ffn_out, g4, b4, l, 1, 2, alpha=alpha)
    return h.reshape(bsz, seq, d)
```

```python
import functools

import jax
import jax.numpy as jnp
from jax import lax
from jax.experimental import pallas as pl
from jax.experimental.pallas import tpu as pltpu

HEAD_DIM = 64
CONV_K = 31
DIL_CONFIGS = ((128, 1), (512, 4), (2048, 16))
ROPE_THETA = 10000.0
LN_EPS = 1e-5
LANES = 128
HEADS_PER_SLAB = LANES // HEAD_DIM
NEG_BIG = -1e30
LOG2_E = 1.4426950408889634
SCORE_SCALE = HEAD_DIM ** -0.5 * LOG2_E
VMEM_LIMIT = 56 * 1024 * 1024

F32 = jnp.float32
BF16 = jnp.bfloat16


def _layer_norm(y, g, b):
    mu = jnp.mean(y, axis=-1, keepdims=True)
    yc = y - mu
    var = jnp.mean(yc * yc, axis=-1, keepdims=True)
    return yc * lax.rsqrt(var + LN_EPS) * g + b


def _params(*sem):
    return pltpu.CompilerParams(dimension_semantics=sem, vmem_limit_bytes=VMEM_LIMIT)


def _resident(arr, *lead):
    rest = arr.shape[len(lead):]
    index = tuple(lead) + (0,) * len(rest)
    return pl.BlockSpec((None,) * len(lead) + rest, lambda *_: index, pipeline_mode=pl.Buffered(1))


def _ffn_kernel(x_ref, win_ref, wout_ref, g_ref, b_ref, o_ref, acc_ref, *, alpha, d_ff, chunk):
    x = x_ref[...]
    xb = x.astype(BF16)
    for c in range(d_ff // chunk):
        lo = c * chunk
        gate = jnp.dot(xb, win_ref[:, lo:lo + chunk], preferred_element_type=F32)
        up = jnp.dot(xb, win_ref[:, d_ff + lo:d_ff + lo + chunk], preferred_element_type=F32)
        hid = (gate * jax.nn.sigmoid(gate) * up).astype(BF16)
        part = jnp.dot(hid, wout_ref[lo:lo + chunk, :], preferred_element_type=F32)
        if c == 0:
            acc_ref[...] = part
        else:
            acc_ref[...] += part
    y = alpha * x + 0.5 * acc_ref[...]
    o_ref[...] = _layer_norm(y, g_ref[...], b_ref[...])


def _ffn_ln(x, w_in, w_out, g, b, l, j, n, *, alpha, tm=1024, chunk=256):
    t, d = x.shape
    d_ff = w_out.shape[-2]
    return pl.pallas_call(
        functools.partial(_ffn_kernel, alpha=alpha, d_ff=d_ff, chunk=chunk),
        grid=(t // tm,),
        in_specs=[pl.BlockSpec((tm, d), lambda i: (i, 0)),
                  _resident(w_in, l, j), _resident(w_out, l, j),
                  _resident(g, l, n), _resident(b, l, n)],
        out_specs=pl.BlockSpec((tm, d), lambda i: (i, 0)),
        out_shape=jax.ShapeDtypeStruct((t, d), F32),
        scratch_shapes=[pltpu.VMEM((tm, d), F32)],
        compiler_params=_params("parallel"),
        name="ffn_ln",
    )(x, w_in, w_out, g, b)


def _rope_slab(t, cos, sin_lo, sin_hi):
    nxt = pltpu.roll(t, LANES - HEAD_DIM // 2, axis=1)
    prv = pltpu.roll(t, HEAD_DIM // 2, axis=1)
    return t * cos + nxt * sin_lo + prv * sin_hi


def _split3(x):
    hi = x.astype(BF16).astype(F32)
    mid = (x - hi).astype(BF16).astype(F32)
    return hi, mid, x - hi - mid


def _store_fox_operands(cum, q, k, v, qa_ref, ka_ref, va_ref, rows):
    lane = lax.broadcasted_iota(jnp.int32, (1, LANES), 1)
    is_head = lane < HEAD_DIM
    aug = lane - HEAD_DIM
    q_ones = jnp.where(jnp.logical_and(aug >= 3, aug < 6), 1.0, 0.0)
    k_ones = jnp.where(jnp.logical_and(aug >= 0, aug < 3), 1.0, 0.0)
    k_bias = jnp.logical_and(aug >= 3, aug < 6)
    for p in range(q.shape[-1] // LANES):
        slab = slice(p * LANES, (p + 1) * LANES)
        split = _split3(cum[:, slab])
        for a in range(HEADS_PER_SLAB):
            h = HEADS_PER_SLAB * p + a
            odd = a == 1
            head_lanes = lambda x: pltpu.roll(x[:, slab], HEAD_DIM, axis=1) if odd else x[:, slab]
            hi, mid, lo = split if odd else [pltpu.roll(t, HEAD_DIM, axis=1) for t in split]
            q_aug = jnp.where(aug == 0, hi, jnp.where(aug == 1, mid, jnp.where(aug == 2, lo, q_ones)))
            k_aug = jnp.where(k_bias, -pltpu.roll(q_aug, 3, axis=1), k_ones)
            qa_ref[h, rows, :] = jnp.where(is_head, head_lanes(q), q_aug).astype(BF16)
            ka_ref[h, rows, :] = jnp.where(is_head, head_lanes(k), k_aug).astype(BF16)
            va_ref[h, rows, :] = jnp.where(is_head, head_lanes(v), 1.0).astype(BF16)


def _stage_mixer_weights(wraw_ref, w_ref, *, split, heads, rows):
    lane = lax.broadcasted_iota(jnp.int32, (1, LANES), 1)
    tail = wraw_ref.shape[1] - split - heads
    for r0 in range(0, wraw_ref.shape[0], rows):
        rs = slice(r0, r0 + rows)
        w_ref[rs, 0:split] = wraw_ref[rs, 0:split].astype(BF16)
        gates = wraw_ref[rs, split:split + heads]
        for p in range(heads // HEADS_PER_SLAB):
            even = jnp.broadcast_to(gates[:, HEADS_PER_SLAB * p:HEADS_PER_SLAB * p + 1], (rows, LANES))
            odd = jnp.broadcast_to(gates[:, HEADS_PER_SLAB * p + 1:HEADS_PER_SLAB * p + 2], (rows, LANES))
            w_ref[rs, split + p * LANES:split + (p + 1) * LANES] = (
                jnp.where(lane < HEAD_DIM, even, odd).astype(BF16))
        lo = split + heads * HEAD_DIM
        w_ref[rs, lo:lo + tail] = wraw_ref[rs, split + heads:split + heads + tail].astype(BF16)


def _proj_kernel(h_ref, wraw_ref, fb_ref, cos_ref, slo_ref, shi_ref,
                 u_ref, qa_ref, ka_ref, va_ref, dq_ref, dk_ref, dv_ref, w_ref, carry_ref,
                 *, cw, fw, dw, nsb, blk):
    @pl.when(pl.program_id(0) == 0)
    def _():
        _stage_mixer_weights(wraw_ref, w_ref, split=2 * cw + 3 * fw, heads=fw // HEAD_DIM, rows=blk)

    @pl.when(pl.program_id(0) % nsb == 0)
    def _():
        carry_ref[...] = jnp.zeros_like(carry_ref)

    hb = h_ref[...].astype(BF16)

    def cols(lo, width):
        return jnp.dot(hb, w_ref[:, lo:lo + width], preferred_element_type=F32)

    scale = SCORE_SCALE
    val = cols(0, cw)
    gate = cols(cw, cw)
    u_ref[...] = val * jax.nn.sigmoid(gate)
    base = 2 * cw
    fq = cols(base, fw) * scale
    fk = cols(base + fw, fw)
    fv = cols(base + 2 * fw, fw)
    log_f = jax.nn.log_sigmoid(cols(base + 3 * fw, fw) + fb_ref[...])

    r = lax.broadcasted_iota(jnp.int32, (blk, blk), 0)
    c = lax.broadcasted_iota(jnp.int32, (blk, blk), 1)
    tri = (c <= r).astype(BF16)
    cum = carry_ref[...]
    for r0 in range(0, hb.shape[0], blk):
        rows = slice(r0, r0 + blk)
        cum = cum[-1:, :]
        for part in _split3(log_f[rows]):
            cum = cum + jnp.dot(tri, part.astype(BF16), preferred_element_type=F32)
        _store_fox_operands(cum * LOG2_E, fq[rows], fk[rows], fv[rows], qa_ref, ka_ref, va_ref, rows)
    carry_ref[...] = cum[-1:, :]

    base = base + 4 * fw
    cos, slo, shi = cos_ref[...], slo_ref[...], shi_ref[...]
    q, k = cols(base, dw), cols(base + dw, dw)
    for j in range(dw // LANES):
        sl = slice(j * LANES, (j + 1) * LANES)
        dq_ref[:, sl] = _rope_slab(q[:, sl], cos, slo, shi) * scale
        dk_ref[:, sl] = _rope_slab(k[:, sl], cos, slo, shi)
    dv_ref[...] = cols(base + 2 * dw, dw)


def _mixer_proj(h, w, fb, cos, slo, shi, l, *, bsz, cw, fw, dw, tm=512, blk=128):
    t, d = h.shape
    seq = t // bsz
    nsb = seq // tm
    heads = fw // HEAD_DIM
    row = lambda i: (i, 0)
    pos = lambda i: (i % nsb, 0)
    tok = lambda n: (pl.BlockSpec((tm, n), row), jax.ShapeDtypeStruct((t, n), F32))
    fox = (pl.BlockSpec((None, heads, tm, LANES), lambda i: (i // nsb, 0, i % nsb, 0)),
           jax.ShapeDtypeStruct((bsz, heads, seq, LANES), BF16))
    outs = [tok(cw), fox, fox, fox, tok(dw), tok(dw), tok(dw)]
    return pl.pallas_call(
        functools.partial(_proj_kernel, cw=cw, fw=fw, dw=dw, nsb=nsb, blk=blk),
        grid=(t // tm,),
        in_specs=[pl.BlockSpec((tm, d), row), _resident(w, l), _resident(fb, l),
                  pl.BlockSpec((tm, LANES), pos), pl.BlockSpec((tm, LANES), pos),
                  pl.BlockSpec((tm, LANES), pos)],
        out_specs=[spec for spec, _ in outs],
        out_shape=[shape for _, shape in outs],
        scratch_shapes=[pltpu.VMEM((d, 2 * cw + 4 * fw + 3 * dw), BF16), pltpu.VMEM((1, fw), F32)],
        compiler_params=_params("arbitrary"),
        name="mixer_proj",
    )(h, w, fb, cos, slo, shi)


def _conv_kernel(u_ref, w_ref, cb_ref, g_ref, b_ref, o_ref, pad_ref, *, rows, halo):
    seq, ch = u_ref.shape
    pad_ref[0:halo, :] = jnp.zeros((halo, ch), F32)
    pad_ref[halo:halo + seq, :] = u_ref[...]
    shift = halo - (CONV_K - 1)

    span = rows + halo
    sub = 8

    slabs = [slice(c0, c0 + LANES) for c0 in range(0, ch, LANES)]

    def body(i, _):
        t0 = pl.multiple_of(i * rows, rows)
        accs = []
        for cs in slabs:
            win = pad_ref[pl.ds(t0, span), cs]
            parts = []
            for b in range(sub):
                rolled = win if b == 0 else pltpu.roll(win, span - b, axis=0)
                taps = [a * sub + b - shift for a in range(halo // sub + 1)]
                prods = [w_ref[k:k + 1, cs] * rolled[k + shift - b:k + shift - b + rows, :]
                         for k in taps if 0 <= k < CONV_K]
                parts.append(functools.reduce(jnp.add, prods))
            while len(parts) > 1:
                parts = [parts[i] + parts[i + 1] for i in range(0, len(parts), 2)]
            accs.append(parts[0] + cb_ref[:, cs])
        mu = sum(jnp.sum(a, axis=-1, keepdims=True) for a in accs) / ch
        cen = [a - mu for a in accs]
        var = sum(jnp.sum(c * c, axis=-1, keepdims=True) for c in cen) / ch
        inv = lax.rsqrt(var + LN_EPS)
        for cs, c in zip(slabs, cen):
            y = c * inv * g_ref[:, cs] + b_ref[:, cs]
            o_ref[pl.ds(t0, rows), cs] = (y * jax.nn.sigmoid(y)).astype(o_ref.dtype)
        return 0

    lax.fori_loop(0, seq // rows, body, 0, unroll=4)


def _conv_module(u, w, cb, g, b, l, *, rows=64, halo=32):
    bsz, seq, ch = u.shape
    return pl.pallas_call(
        functools.partial(_conv_kernel, rows=rows, halo=halo),
        grid=(bsz,),
        in_specs=[pl.BlockSpec((None, seq, ch), lambda i: (i, 0, 0)),
                  _resident(w, l), _resident(cb, l), _resident(g, l), _resident(b, l)],
        out_specs=pl.BlockSpec((None, seq, ch), lambda i: (i, 0, 0)),
        out_shape=jax.ShapeDtypeStruct((bsz, seq, ch), BF16),
        scratch_shapes=[pltpu.VMEM((halo + seq, ch), F32)],
        compiler_params=_params("parallel"),
        name="conv_module",
    )(u, w, cb, g, b)


def _fox_kernel(q_ref, k_ref, v_ref, o_ref, m_ref, acc_ref, *, tq):
    i = pl.program_id(2)
    half = tq // 2
    dn = (((1,), (1,)), ((), ()))

    def update(a, rows, k0, width, mask):
        keys = pl.ds(pl.multiple_of(k0, tq), width)
        s = lax.dot_general(q_ref[a, rows, :], k_ref[a, keys, :], dn, preferred_element_type=F32)
        if mask is not None:
            s = jnp.where(mask, s, NEG_BIG)
        m_old = m_ref[a, rows, :]
        m_new = jnp.maximum(m_old, jnp.max(s, axis=-1, keepdims=True))
        alpha = jnp.exp2(m_old - m_new)
        p = jnp.exp2(s - pltpu.repeat(m_new, width // LANES, axis=1))
        m_ref[a, rows, :] = m_new
        acc_ref[a, rows, :] = alpha * acc_ref[a, rows, :] + jnp.dot(
            p.astype(BF16), v_ref[a, keys, :], preferred_element_type=F32)

    m_ref[...] = jnp.full(m_ref.shape, NEG_BIG, F32)
    acc_ref[...] = jnp.zeros(acc_ref.shape, F32)
    lower = (lax.broadcasted_iota(jnp.int32, (half, half), 1)
             <= lax.broadcasted_iota(jnp.int32, (half, half), 0))
    upper = (lax.broadcasted_iota(jnp.int32, (half, tq), 1)
             <= lax.broadcasted_iota(jnp.int32, (half, tq), 0) + half)
    for a in range(HEADS_PER_SLAB):
        update(a, slice(0, half), i * tq, half, lower)
        update(a, slice(half, tq), i * tq, tq, upper)

    def body(j, _):
        for a in range(HEADS_PER_SLAB):
            update(a, slice(0, tq), j * tq, tq, None)
        return 0

    lax.fori_loop(0, i, body, 0)

    lane = lax.broadcasted_iota(jnp.int32, (1, LANES), 1)
    acc0, acc1 = acc_ref[0], acc_ref[1]
    out = jnp.where(lane < HEAD_DIM,
                    acc0 / pltpu.roll(acc0, HEAD_DIM, axis=1),
                    pltpu.roll(acc1, HEAD_DIM, axis=1) / acc1)
    o_ref[...] = out.astype(o_ref.dtype)


def _fox_attention(qa, ka, va, *, tq=1024):
    bsz, heads, seq, _ = qa.shape
    qspec = pl.BlockSpec((None, HEADS_PER_SLAB, tq, LANES), lambda b, p, i: (b, p, i, 0))
    kvspec = pl.BlockSpec((None, HEADS_PER_SLAB, seq, LANES), lambda b, p, i: (b, p, 0, 0))
    return pl.pallas_call(
        functools.partial(_fox_kernel, tq=tq),
        grid=(bsz, heads // HEADS_PER_SLAB, seq // tq),
        in_specs=[qspec, kvspec, kvspec],
        out_specs=pl.BlockSpec((None, tq, LANES), lambda b, p, i: (b, i, p)),
        out_shape=jax.ShapeDtypeStruct((bsz, seq, heads * HEAD_DIM), BF16),
        scratch_shapes=[pltpu.VMEM((HEADS_PER_SLAB, tq, LANES), F32),
                        pltpu.VMEM((HEADS_PER_SLAB, tq, LANES), F32)],
        compiler_params=_params("parallel", "parallel", "arbitrary"),
        name="fox_attention",
    )(qa, ka, va)


def _dil_kernel(q_ref, k_ref, v_ref, o_ref, bias_ref, bias2_ref, m_ref, l_ref, acc_ref, *,
                blk, unroll, rows_out):
    seq = q_ref.shape[0]
    win = 2 * blk
    lane = lax.broadcasted_iota(jnp.int32, (1, LANES), 1)
    head0 = lane < HEAD_DIM
    dist0 = (lax.broadcasted_iota(jnp.int32, (win, win), 0) % blk
             - lax.broadcasted_iota(jnp.int32, (win, win), 1))
    row = lax.broadcasted_iota(jnp.int32, (2 * win, win), 0)
    col = lax.broadcasted_iota(jnp.int32, (2 * win, win), 1)
    bias_ref[0] = jnp.where(dist0 >= 0, 0.0, NEG_BIG)
    bias_ref[1] = jnp.where(jnp.logical_and(dist0 + blk >= 0, dist0 <= 0), 0.0, NEG_BIG)
    dist_all = (row % win) - col
    bias2_ref[...] = jnp.where(jnp.logical_and(dist_all >= 0, dist_all <= blk), 0.0, NEG_BIG)
    ones = jnp.ones((win, LANES), BF16)
    dn = (((1,), (1,)), ((), ()))

    def attend(bi, rows, wrows, bias):
        nq = rows.size
        qb = q_ref[rows, :].astype(BF16)
        zero = jnp.zeros_like(qb)
        qs = jnp.concatenate([jnp.where(head0, qb, zero), jnp.where(head0, zero, qb)], axis=0)
        kw = k_ref[wrows, :].astype(BF16)
        vw = jnp.concatenate([v_ref[wrows, :].astype(BF16), ones], axis=1)
        s = lax.dot_general(qs, kw, dn, preferred_element_type=F32) + bias
        m = jnp.max(s, axis=-1, keepdims=True)
        p = jnp.exp2(s - m).astype(BF16)
        pv = jnp.dot(p, vw, preferred_element_type=F32)
        mb = jnp.broadcast_to(m, (2 * nq, LANES))
        m_ref[bi, rows, :] = jnp.where(head0, mb[:nq], mb[nq:])
        acc_ref[bi, rows, :] = jnp.where(head0, pv[:nq, :LANES], pv[nq:, :LANES])
        l_ref[bi, rows, :] = jnp.where(head0, pv[:nq, LANES:], pv[nq:, LANES:])

    for bi, (_, dil) in enumerate(DIL_CONFIGS):
        nb = seq // (dil * blk)
        ds = (lambda start, size, dil=dil: pl.ds(start, size, stride=dil) if dil > 1 else pl.ds(start, size))

        if nb == 2:
            def body(r, _, bi=bi, ds=ds):
                attend(bi, ds(r, win), ds(r, win), bias2_ref[...])
                return 0

            lax.fori_loop(0, dil, body, 0, unroll=unroll // 2)
            continue

        def body(idx, _, bi=bi, dil=dil, nb=nb, ds=ds):
            r = idx // nb
            n = idx % nb
            start = r + n * (dil * blk)
            has_prev = (n > 0).astype(jnp.int32)
            attend(bi, ds(start, blk), ds(start - has_prev * (dil * blk), win), bias_ref[has_prev])
            return 0

        lax.fori_loop(0, seq // blk, body, 0, unroll=unroll)

    def merge(i, _):
        rows = pl.ds(pl.multiple_of(i * rows_out, rows_out), rows_out)
        ms = [m_ref[bi, rows, :] for bi in range(len(DIL_CONFIGS))]
        m_all = functools.reduce(jnp.maximum, ms)
        ws = [jnp.exp2(m - m_all) for m in ms]
        num = sum(w * acc_ref[bi, rows, :] for bi, w in enumerate(ws))
        den = sum(w * l_ref[bi, rows, :] for bi, w in enumerate(ws))
        o_ref[rows, :] = (num / den).astype(o_ref.dtype)
        return 0

    lax.fori_loop(0, seq // rows_out, merge, 0)


def _dilated_attention(q, k, v, *, blk=128, unroll=16, rows_out=256):
    bsz, seq, width = q.shape
    nbr = len(DIL_CONFIGS)
    spec = pl.BlockSpec((None, seq, LANES), lambda b, p: (b, 0, p))
    return pl.pallas_call(
        functools.partial(_dil_kernel, blk=blk, unroll=unroll, rows_out=rows_out),
        grid=(bsz, width // LANES),
        in_specs=[spec, spec, spec],
        out_specs=spec,
        out_shape=jax.ShapeDtypeStruct((bsz, seq, width), BF16),
        scratch_shapes=[pltpu.VMEM((2, 2 * blk, 2 * blk), F32), pltpu.VMEM((4 * blk, 2 * blk), F32)]
        + [pltpu.VMEM((nbr, seq, LANES), F32)] * 3,
        compiler_params=_params("parallel", "parallel"),
        name="dilated_attention",
    )(q, k, v)


def _out_kernel(h_ref, ya_ref, yb_ref, yc_ref, wraw_ref, g_ref, b_ref, o_ref, w_ref, *,
                alpha, cw, fw, parts):
    @pl.when(pl.program_id(0) == 0)
    def _():
        w_ref[...] = wraw_ref[...].astype(BF16)

    rows_per = h_ref.shape[0] // parts
    for part_idx in range(parts):
        rows = slice(part_idx * rows_per, (part_idx + 1) * rows_per)
        y = jnp.dot(ya_ref[rows, :], w_ref[0:cw, :], preferred_element_type=F32)
        y = y + jnp.dot(yb_ref[rows, :], w_ref[cw:cw + fw, :], preferred_element_type=F32)
        y = y + jnp.dot(yc_ref[rows, :], w_ref[cw + fw:, :], preferred_element_type=F32)
        o_ref[rows, :] = _layer_norm(alpha * h_ref[rows, :] + y, g_ref[...], b_ref[...])


def _out_proj_ln(h, ya, yb, yc, w, g, b, l, n, *, alpha, tm=1024, parts=2):
    t, d = h.shape
    row = lambda i: (i, 0)
    return pl.pallas_call(
        functools.partial(_out_kernel, alpha=alpha, cw=ya.shape[1], fw=yb.shape[1], parts=parts),
        grid=(t // tm,),
        in_specs=[pl.BlockSpec((tm, d), row), pl.BlockSpec((tm, ya.shape[1]), row),
                  pl.BlockSpec((tm, yb.shape[1]), row), pl.BlockSpec((tm, yc.shape[1]), row),
                  _resident(w, l), _resident(g, l, n), _resident(b, l, n)],
        out_specs=pl.BlockSpec((tm, d), row),
        out_shape=jax.ShapeDtypeStruct((t, d), F32),
        scratch_shapes=[pltpu.VMEM((d, d), BF16)],
        compiler_params=_params("arbitrary"),
        name="out_proj_ln",
    )(h, ya, yb, yc, w, g, b)


def _rope_tables(seq):
    inv = 1.0 / (ROPE_THETA ** (jnp.arange(0, HEAD_DIM, 2, dtype=F32) / HEAD_DIM))
    ang = jnp.arange(seq, dtype=F32)[:, None] * inv[None, :]
    ang = jnp.concatenate([ang, ang], axis=-1)
    cos, sin = jnp.cos(ang), jnp.sin(ang)
    first_half = (jnp.arange(HEAD_DIM) < HEAD_DIM // 2)[None, :]
    sin_lo = jnp.where(first_half, -sin, 0.0)
    sin_hi = jnp.where(first_half, 0.0, sin)
    rep = lambda a: jnp.tile(a, (1, HEADS_PER_SLAB))
    return rep(cos), rep(sin_lo), rep(sin_hi)


def kernel(x, w_in, w_o, forget_bias, conv_w, conv_b, conv_ln_g, conv_ln_b, ffn_w_in, ffn_w_out, ln_g, ln_b):
    bsz, seq, d = x.shape
    depth = w_in.shape[0]
    cw = conv_w.shape[-1]
    heads_f = forget_bias.shape[-1]
    fw = heads_f * HEAD_DIM
    dw = d - cw - fw
    alpha = (2 * depth) ** 0.25
    t = bsz * seq
    cos, slo, shi = _rope_tables(seq)

    ffn_in = ffn_w_in.astype(BF16)
    ffn_out = ffn_w_out.astype(BF16)
    fb =jnp.repeat(forget_bias, HEAD_DIM, axis=1)[:, None, :]
    row3 = lambda a: a[:, None, :]
    cb, cg, cbeta = row3(conv_b), row3(conv_ln_g), row3(conv_ln_b)
    g4, b4 = ln_g[:, :, None, :], ln_b[:, :, None, :]
    b3 = lambda a: a.reshape(bsz, seq, a.shape[-1])

    h = x.reshape(t, d)
    for l in range(depth):
        h = _ffn_ln(h, ffn_in, ffn_out, g4, b4, l, 0, 0, alpha=alpha)
        u, qa, ka, va, dq, dk, dv = _mixer_proj(h, w_in, fb, cos, slo, shi, l,
                                                bsz=bsz, cw=cw, fw=fw, dw=dw)
        ya = _conv_module(b3(u), conv_w, cb, cg, cbeta, l)
        yb = _fox_attention(qa, ka, va)
        yc = _dilated_attention(b3(dq), b3(dk), b3(dv))
        h = _out_proj_ln(h, ya.reshape(t, cw), yb.reshape(t, fw), yc.reshape(t, dw),
                         w_o, g4, b4, l, 1, alpha=alpha)
        h = _ffn_ln(h, ffn_in, ffn_out, g4, b4, l, 1, 2, alpha=alpha)
    return h.reshape(bsz, seq, d)
```

```python
import functools

import jax
import jax.numpy as jnp
from jax import lax
from jax.experimental import pallas as pl
from jax.experimental.pallas import tpu as pltpu

HEAD_DIM = 64
CONV_K = 31
DIL_CONFIGS = ((128, 1), (512, 4), (2048, 16))
ROPE_THETA = 10000.0
LN_EPS = 1e-5
LANES = 128
HEADS_PER_SLAB = LANES // HEAD_DIM
NEG_BIG = -1e30
LOG2_E = 1.4426950408889634
SCORE_SCALE = HEAD_DIM ** -0.5 * LOG2_E
VMEM_LIMIT = 56 * 1024 * 1024

F32 = jnp.float32
BF16 = jnp.bfloat16


def _layer_norm(y, g, b):
    mu = jnp.mean(y, axis=-1, keepdims=True)
    yc = y - mu
    var = jnp.mean(yc * yc, axis=-1, keepdims=True)
    return yc * lax.rsqrt(var + LN_EPS) * g + b


def _params(*sem):
    return pltpu.CompilerParams(dimension_semantics=sem, vmem_limit_bytes=VMEM_LIMIT)


def _resident(arr, *lead):
    rest = arr.shape[len(lead):]
    index = tuple(lead) + (0,) * len(rest)
    return pl.BlockSpec((None,) * len(lead) + rest, lambda *_: index, pipeline_mode=pl.Buffered(1))


def _ffn_kernel(x_ref, win_ref, wout_ref, g_ref, b_ref, o_ref, acc_ref, *, alpha, d_ff, chunk):
    x = x_ref[...]
    xb = x.astype(BF16)
    for c in range(d_ff // chunk):
        lo = c * chunk
        gate = jnp.dot(xb, win_ref[:, lo:lo + chunk], preferred_element_type=F32)
        up = jnp.dot(xb, win_ref[:, d_ff + lo:d_ff + lo + chunk], preferred_element_type=F32)
        hid = (gate * jax.nn.sigmoid(gate) * up).astype(BF16)
        part = jnp.dot(hid, wout_ref[lo:lo + chunk, :], preferred_element_type=F32)
        if c == 0:
            acc_ref[...] = part
        else:
            acc_ref[...] += part
    y = alpha * x + 0.5 * acc_ref[...]
    o_ref[...] = _layer_norm(y, g_ref[...], b_ref[...])


def _ffn_ln(x, w_in, w_out, g, b, l, j, n, *, alpha, tm=1024, chunk=256):
    t, d = x.shape
    d_ff = w_out.shape[-2]
    return pl.pallas_call(
        functools.partial(_ffn_kernel, alpha=alpha, d_ff=d_ff, chunk=chunk),
        grid=(t // tm,),
        in_specs=[pl.BlockSpec((tm, d), lambda i: (i, 0)),
                  _resident(w_in, l, j), _resident(w_out, l, j),
                  _resident(g, l, n), _resident(b, l, n)],
        out_specs=pl.BlockSpec((tm, d), lambda i: (i, 0)),
        out_shape=jax.ShapeDtypeStruct((t, d), F32),
        scratch_shapes=[pltpu.VMEM((tm, d), F32)],
        compiler_params=_params("parallel"),
        name="ffn_ln",
    )(x, w_in, w_out, g, b)


def _rope_slab(t, cos, sin_lo, sin_hi):
    nxt = pltpu.roll(t, LANES - HEAD_DIM // 2, axis=1)
    prv = pltpu.roll(t, HEAD_DIM // 2, axis=1)
    return t * cos + nxt * sin_lo + prv * sin_hi


def _split3(x):
    hi = x.astype(BF16).astype(F32)
    mid = (x - hi).astype(BF16).astype(F32)
    return hi, mid, x - hi - mid


def _store_fox_operands(cum, q, k, v, qa_ref, ka_ref, va_ref, rows):
    lane = lax.broadcasted_iota(jnp.int32, (1, LANES), 1)
    is_head = lane < HEAD_DIM
    aug = lane - HEAD_DIM
    q_ones = jnp.where(jnp.logical_and(aug >= 3, aug < 6), 1.0, 0.0)
    k_ones = jnp.where(jnp.logical_and(aug >= 0, aug < 3), 1.0, 0.0)
    k_bias = jnp.logical_and(aug >= 3, aug < 6)
    for p in range(q.shape[-1] // LANES):
        slab = slice(p * LANES, (p + 1) * LANES)
        split = _split3(cum[:, slab])
        for a in range(HEADS_PER_SLAB):
            h = HEADS_PER_SLAB * p + a
            odd = a == 1
            head_lanes = lambda x: pltpu.roll(x[:, slab], HEAD_DIM, axis=1) if odd else x[:, slab]
            hi, mid, lo = split if odd else [pltpu.roll(t, HEAD_DIM, axis=1) for t in split]
            q_aug = jnp.where(aug == 0, hi, jnp.where(aug == 1, mid, jnp.where(aug == 2, lo, q_ones)))
            k_aug = jnp.where(k_bias, -pltpu.roll(q_aug, 3, axis=1), k_ones)
            qa_ref[h, rows, :] = jnp.where(is_head, head_lanes(q), q_aug).astype(BF16)
            ka_ref[h, rows, :] = jnp.where(is_head, head_lanes(k), k_aug).astype(BF16)
            va_ref[h, rows, :] = jnp.where(is_head, head_lanes(v), 1.0).astype(BF16)


def _stage_mixer_weights(wraw_ref, w_ref, *, split, heads, rows):
    lane = lax.broadcasted_iota(jnp.int32, (1, LANES), 1)
    tail = wraw_ref.shape[1] - split - heads
    for r0 in range(0, wraw_ref.shape[0], rows):
        rs = slice(r0, r0 + rows)
        w_ref[rs, 0:split] = wraw_ref[rs, 0:split].astype(BF16)
        gates = wraw_ref[rs, split:split + heads]
        for p in range(heads // HEADS_PER_SLAB):
            even = jnp.broadcast_to(gates[:, HEADS_PER_SLAB * p:HEADS_PER_SLAB * p + 1], (rows, LANES))
            odd = jnp.broadcast_to(gates[:, HEADS_PER_SLAB * p + 1:HEADS_PER_SLAB * p + 2], (rows, LANES))
            w_ref[rs, split + p * LANES:split + (p + 1) * LANES] = (
                jnp.where(lane < HEAD_DIM, even, odd).astype(BF16))
        lo = split + heads * HEAD_DIM
        w_ref[rs, lo:lo + tail] = wraw_ref[rs, split + heads:split + heads + tail].astype(BF16)


def _conv_rows(pad_ref, t0, rows, halo, w_ref, cb_ref, g_ref, b_ref, o_ref):
    ch = pad_ref.shape[1]
    first = t0 + halo - (CONV_K - 1)
    slabs = [slice(c0, c0 + LANES) for c0 in range(0, ch, LANES)]
    accs = []
    sub = 8
    base = first // sub * sub
    span = rows + (first - base + CONV_K - 1 + sub - 1) // sub * sub
    for cs in slabs:
        win = pad_ref[base:base + span, cs]
        parts = []
        for b in range(sub):
            rolled = win if b == 0 else pltpu.roll(win, span - b, axis=0)
            offs = [o for o in range(b, span - rows + 1, sub) if 0 <= base + o - first < CONV_K]
            parts.append(functools.reduce(jnp.add, [
                w_ref[base + o - first:base + o - first + 1, cs] * rolled[o - b:o - b + rows, :]
                for o in offs]))
        while len(parts) > 1:
            parts = [parts[i] + parts[i + 1] for i in range(0, len(parts), 2)]
        accs.append(parts[0] + cb_ref[:, cs])
    mu = sum(jnp.sum(a, axis=-1, keepdims=True) for a in accs) / ch
    cen = [a - mu for a in accs]
    var = sum(jnp.sum(c * c, axis=-1, keepdims=True) for c in cen) / ch
    inv = lax.rsqrt(var + LN_EPS)
    for cs, c in zip(slabs, cen):
        y = c * inv * g_ref[:, cs] + b_ref[:, cs]
        o_ref[t0:t0 + rows, cs] = (y * jax.nn.sigmoid(y)).astype(o_ref.dtype)


def _proj_kernel(h_ref, wraw_ref, fb_ref, cos_ref, slo_ref, shi_ref, cw_ref, cb_ref, cg_ref, cbeta_ref,
                 ya_ref, qa_ref, ka_ref, va_ref, dq_ref, dk_ref, dv_ref, w_ref, carry_ref, pad_ref,
                 *, cw, fw, dw, nsb, blk, halo, conv_rows):
    tm = h_ref.shape[0]

    @pl.when(pl.program_id(0) == 0)
    def _():
        _stage_mixer_weights(wraw_ref, w_ref, split=2 * cw + 3 * fw, heads=fw // HEAD_DIM, rows=blk)

    @pl.when(pl.program_id(0) % nsb == 0)
    def _():
        carry_ref[...] = jnp.zeros_like(carry_ref)
        pad_ref[0:halo, :] = jnp.zeros((halo, cw), F32)

    @pl.when(pl.program_id(0) % nsb != 0)
    def _():
        pad_ref[0:halo, :] = pad_ref[tm:tm + halo, :]

    hb = h_ref[...].astype(BF16)

    def cols(lo, width):
        return jnp.dot(hb, w_ref[:, lo:lo + width], preferred_element_type=F32)

    scale = SCORE_SCALE
    val = cols(0, cw)
    gate = cols(cw, cw)
    pad_ref[halo:halo + tm, :] = val * jax.nn.sigmoid(gate)
    for t0 in range(0, tm, conv_rows):
        _conv_rows(pad_ref, t0, conv_rows, halo, cw_ref, cb_ref, cg_ref, cbeta_ref, ya_ref)
    base = 2 * cw
    fq = cols(base, fw) * scale
    fk = cols(base + fw, fw)
    fv = cols(base + 2 * fw, fw)
    log_f = jax.nn.log_sigmoid(cols(base + 3 * fw, fw) + fb_ref[...])

    r = lax.broadcasted_iota(jnp.int32, (blk, blk), 0)
    c = lax.broadcasted_iota(jnp.int32, (blk, blk), 1)
    tri = (c <= r).astype(BF16)
    cum = carry_ref[...]
    for r0 in range(0, hb.shape[0], blk):
        rows = slice(r0, r0 + blk)
        cum = cum[-1:, :]
        for part in _split3(log_f[rows]):
            cum = cum + jnp.dot(tri, part.astype(BF16), preferred_element_type=F32)
        _store_fox_operands(cum * LOG2_E, fq[rows], fk[rows], fv[rows], qa_ref, ka_ref, va_ref, rows)
    carry_ref[...] = cum[-1:, :]

    base = base + 4 * fw
    cos, slo, shi = cos_ref[...], slo_ref[...], shi_ref[...]
    q, k = cols(base, dw), cols(base + dw, dw)
    for j in range(dw // LANES):
        sl = slice(j * LANES, (j + 1) * LANES)
        dq_ref[:, sl] = _rope_slab(q[:, sl], cos, slo, shi) * scale
        dk_ref[:, sl] = _rope_slab(k[:, sl], cos, slo, shi)
    dv_ref[...] = cols(base + 2 * dw, dw)


def _mixer_proj(h, w, fb, cos, slo, shi, conv_w, conv_b, conv_g, conv_beta, l, *, bsz, fw, dw,
                tm=512, blk=128, halo=32, conv_rows=64):
    t, d = h.shape
    cw = conv_w.shape[-1]
    seq = t // bsz
    nsb = seq // tm
    heads = fw // HEAD_DIM
    row = lambda i: (i, 0)
    pos = lambda i: (i % nsb, 0)
    tok = lambda n, dt: (pl.BlockSpec((tm, n), row), jax.ShapeDtypeStruct((t, n), dt))
    fox = (pl.BlockSpec((None, heads, tm, LANES), lambda i: (i // nsb, 0, i % nsb, 0)),
           jax.ShapeDtypeStruct((bsz, heads, seq, LANES), BF16))
    outs = [tok(cw, BF16), fox, fox, fox, tok(dw, F32), tok(dw, F32), tok(dw, F32)]
    return pl.pallas_call(
        functools.partial(_proj_kernel, cw=cw, fw=fw, dw=dw, nsb=nsb, blk=blk, halo=halo, conv_rows=conv_rows),
        grid=(t // tm,),
        in_specs=[pl.BlockSpec((tm, d), row), _resident(w, l), _resident(fb, l),
                  pl.BlockSpec((tm, LANES), pos), pl.BlockSpec((tm, LANES), pos),
                  pl.BlockSpec((tm, LANES), pos),
                  _resident(conv_w, l), _resident(conv_b, l), _resident(conv_g, l), _resident(conv_beta, l)],
        out_specs=[spec for spec, _ in outs],
        out_shape=[shape for _, shape in outs],
        scratch_shapes=[pltpu.VMEM((d, 2 * cw + 4 * fw + 3 * dw), BF16), pltpu.VMEM((1, fw), F32),
                        pltpu.VMEM((halo + tm, cw), F32)],
        compiler_params=_params("arbitrary"),
        name="mixer_proj",
    )(h, w, fb, cos, slo, shi, conv_w, conv_b, conv_g, conv_beta)


def _fox_kernel(q_ref, k_ref, v_ref, o_ref, m_ref, acc_ref, *, tq):
    i = pl.program_id(2)
    half = tq // 2
    dn = (((1,), (1,)), ((), ()))

    def update(a, rows, k0, width, mask):
        keys = pl.ds(pl.multiple_of(k0, tq), width)
        s = lax.dot_general(q_ref[a, rows, :], k_ref[a, keys, :], dn, preferred_element_type=F32)
        if mask is not None:
            s = jnp.where(mask, s, NEG_BIG)
        m_old = m_ref[a, rows, :]
        m_new = jnp.maximum(m_old, jnp.max(s, axis=-1, keepdims=True))
        alpha = jnp.exp2(m_old - m_new)
        p = jnp.exp2(s - pltpu.repeat(m_new, width // LANES, axis=1))
        m_ref[a, rows, :] = m_new
        acc_ref[a, rows, :] = alpha * acc_ref[a, rows, :] + jnp.dot(
            p.astype(BF16), v_ref[a, keys, :], preferred_element_type=F32)

    m_ref[...] = jnp.full(m_ref.shape, NEG_BIG, F32)
    acc_ref[...] = jnp.zeros(acc_ref.shape, F32)
    lower = (lax.broadcasted_iota(jnp.int32, (half, half), 1)
             <= lax.broadcasted_iota(jnp.int32, (half, half), 0))
    upper = (lax.broadcasted_iota(jnp.int32, (half, tq), 1)
             <= lax.broadcasted_iota(jnp.int32, (half, tq), 0) + half)
    for a in range(HEADS_PER_SLAB):
        update(a, slice(0, half), i * tq, half, lower)
        update(a, slice(half, tq), i * tq, tq, upper)

    def body(j, _):
        for a in range(HEADS_PER_SLAB):
            update(a, slice(0, tq), j * tq, tq, None)
        return 0

    lax.fori_loop(0, i, body, 0)

    lane = lax.broadcasted_iota(jnp.int32, (1, LANES), 1)
    acc0, acc1 = acc_ref[0], acc_ref[1]
    out = jnp.where(lane < HEAD_DIM,
                    acc0 / pltpu.roll(acc0, HEAD_DIM, axis=1),
                    pltpu.roll(acc1, HEAD_DIM, axis=1) / acc1)
    o_ref[...] = out.astype(o_ref.dtype)


def _fox_attention(qa, ka, va, *, tq=1024):
    bsz, heads, seq, _ = qa.shape
    qspec = pl.BlockSpec((None, HEADS_PER_SLAB, tq, LANES), lambda b, p, i: (b, p, i, 0))
    kvspec = pl.BlockSpec((None, HEADS_PER_SLAB, seq, LANES), lambda b, p, i: (b, p, 0, 0))
    return pl.pallas_call(
        functools.partial(_fox_kernel, tq=tq),
        grid=(bsz, heads // HEADS_PER_SLAB, seq // tq),
        in_specs=[qspec, kvspec, kvspec],
        out_specs=pl.BlockSpec((None, tq, LANES), lambda b, p, i: (b, i, p)),
        out_shape=jax.ShapeDtypeStruct((bsz, seq, heads * HEAD_DIM), BF16),
        scratch_shapes=[pltpu.VMEM((HEADS_PER_SLAB, tq, LANES), F32),
                        pltpu.VMEM((HEADS_PER_SLAB, tq, LANES), F32)],
        compiler_params=_params("parallel", "parallel", "arbitrary"),
        name="fox_attention",
    )(qa, ka, va)


def _dil_kernel(q_ref, k_ref, v_ref, o_ref, bias_ref, bias2_ref, m_ref, l_ref, acc_ref, *,
                blk, unroll, rows_out):
    seq = q_ref.shape[0]
    win = 2 * blk
    lane = lax.broadcasted_iota(jnp.int32, (1, LANES), 1)
    head0 = lane < HEAD_DIM
    dist0 = (lax.broadcasted_iota(jnp.int32, (win, win), 0) % blk
             - lax.broadcasted_iota(jnp.int32, (win, win), 1))
    row = lax.broadcasted_iota(jnp.int32, (2 * win, win), 0)
    col = lax.broadcasted_iota(jnp.int32, (2 * win, win), 1)
    bias_ref[0] = jnp.where(dist0 >= 0, 0.0, NEG_BIG)
    bias_ref[1] = jnp.where(jnp.logical_and(dist0 + blk >= 0, dist0 <= 0), 0.0, NEG_BIG)
    dist_all = (row % win) - col
    bias2_ref[...] = jnp.where(jnp.logical_and(dist_all >= 0, dist_all <= blk), 0.0, NEG_BIG)
    ones = jnp.ones((win, LANES), BF16)
    dn = (((1,), (1,)), ((), ()))

    def attend(bi, rows, wrows, bias):
        nq = rows.size
        qb = q_ref[rows, :].astype(BF16)
        zero = jnp.zeros_like(qb)
        qs = jnp.concatenate([jnp.where(head0, qb, zero), jnp.where(head0, zero, qb)], axis=0)
        kw = k_ref[wrows, :].astype(BF16)
        vw = jnp.concatenate([v_ref[wrows, :].astype(BF16), ones], axis=1)
        s = lax.dot_general(qs, kw, dn, preferred_element_type=F32) + bias
        m = jnp.max(s, axis=-1, keepdims=True)
        p = jnp.exp2(s - m).astype(BF16)
        pv = jnp.dot(p, vw, preferred_element_type=F32)
        mb = jnp.broadcast_to(m, (2 * nq, LANES))
        m_ref[bi, rows, :] = jnp.where(head0, mb[:nq], mb[nq:])
        acc_ref[bi, rows, :] = jnp.where(head0, pv[:nq, :LANES], pv[nq:, :LANES])
        l_ref[bi, rows, :] = jnp.where(head0, pv[:nq, LANES:], pv[nq:, LANES:])

    for bi, (_, dil) in enumerate(DIL_CONFIGS):
        nb = seq // (dil * blk)
        ds = (lambda start, size, dil=dil: pl.ds(start, size, stride=dil) if dil > 1 else pl.ds(start, size))

        if nb == 2:
            def body(r, _, bi=bi, ds=ds):
                attend(bi, ds(r, win), ds(r, win), bias2_ref[...])
                return 0

            lax.fori_loop(0, dil, body, 0, unroll=unroll // 2)
            continue

        def body(idx, _, bi=bi, dil=dil, nb=nb, ds=ds):
            r = idx // nb
            n = idx % nb
            start = r + n * (dil * blk)
            has_prev = (n > 0).astype(jnp.int32)
            attend(bi, ds(start, blk), ds(start - has_prev * (dil * blk), win), bias_ref[has_prev])
            return 0

        lax.fori_loop(0, seq // blk, body, 0, unroll=unroll)

    def merge(i, _):
        rows = pl.ds(pl.multiple_of(i * rows_out, rows_out), rows_out)
        ms = [m_ref[bi, rows, :] for bi in range(len(DIL_CONFIGS))]
        m_all = functools.reduce(jnp.maximum, ms)
        ws = [jnp.exp2(m - m_all) for m in ms]
        num = sum(w * acc_ref[bi, rows, :] for bi, w in enumerate(ws))
        den = sum(w * l_ref[bi, rows, :] for bi, w in enumerate(ws))
        o_ref[rows, :] = (num / den).astype(o_ref.dtype)
        return 0

    lax.fori_loop(0, seq // rows_out, merge, 0)


def _dilated_attention(q, k, v, *, blk=128, unroll=16, rows_out=256):
    bsz, seq, width = q.shape
    nbr = len(DIL_CONFIGS)
    spec = pl.BlockSpec((None, seq, LANES), lambda b, p: (b, 0, p))
    return pl.pallas_call(
        functools.partial(_dil_kernel, blk=blk, unroll=unroll, rows_out=rows_out),
        grid=(bsz, width // LANES),
        in_specs=[spec, spec, spec],
        out_specs=spec,
        out_shape=jax.ShapeDtypeStruct((bsz, seq, width), BF16),
        scratch_shapes=[pltpu.VMEM((2, 2 * blk, 2 * blk), F32), pltpu.VMEM((4 * blk, 2 * blk), F32)]
        + [pltpu.VMEM((nbr, seq, LANES), F32)] * 3,
        compiler_params=_params("parallel", "parallel"),
        name="dilated_attention",
    )(q, k, v)


def _out_kernel(h_ref, ya_ref, yb_ref, yc_ref, wraw_ref, g_ref, b_ref, o_ref, w_ref, *,
                alpha, cw, fw, parts):
    @pl.when(pl.program_id(0) == 0)
    def _():
        w_ref[...] = wraw_ref[...].astype(BF16)

    rows_per = h_ref.shape[0] // parts
    for part_idx in range(parts):
        rows = slice(part_idx * rows_per, (part_idx + 1) * rows_per)
        y = jnp.dot(ya_ref[rows, :], w_ref[0:cw, :], preferred_element_type=F32)
        y = y + jnp.dot(yb_ref[rows, :], w_ref[cw:cw + fw, :], preferred_element_type=F32)
        y = y + jnp.dot(yc_ref[rows, :], w_ref[cw + fw:, :], preferred_element_type=F32)
        o_ref[rows, :] = _layer_norm(alpha * h_ref[rows, :] + y, g_ref[...], b_ref[...])


def _out_proj_ln(h, ya, yb, yc, w, g, b, l, n, *, alpha, tm=1024, parts=2):
    t, d = h.shape
    row = lambda i: (i, 0)
    return pl.pallas_call(
        functools.partial(_out_kernel, alpha=alpha, cw=ya.shape[1], fw=yb.shape[1], parts=parts),
        grid=(t // tm,),
        in_specs=[pl.BlockSpec((tm, d), row), pl.BlockSpec((tm, ya.shape[1]), row),
                  pl.BlockSpec((tm, yb.shape[1]), row), pl.BlockSpec((tm, yc.shape[1]), row),
                  _resident(w, l), _resident(g, l, n), _resident(b, l, n)],
        out_specs=pl.BlockSpec((tm, d), row),
        out_shape=jax.ShapeDtypeStruct((t, d), F32),
        scratch_shapes=[pltpu.VMEM((d, d), BF16)],
        compiler_params=_params("arbitrary"),
        name="out_proj_ln",
    )(h, ya, yb, yc, w, g, b)


def _rope_tables(seq):
    inv = 1.0 / (ROPE_THETA ** (jnp.arange(0, HEAD_DIM, 2, dtype=F32) / HEAD_DIM))
    ang = jnp.arange(seq, dtype=F32)[:, None] * inv[None, :]
    ang = jnp.concatenate([ang, ang], axis=-1)
    cos, sin = jnp.cos(ang), jnp.sin(ang)
    first_half = (jnp.arange(HEAD_DIM) < HEAD_DIM // 2)[None, :]
    sin_lo = jnp.where(first_half, -sin, 0.0)
    sin_hi = jnp.where(first_half, 0.0, sin)
    rep = lambda a: jnp.tile(a, (1, HEADS_PER_SLAB))
    return rep(cos), rep(sin_lo), rep(sin_hi)


def kernel(x, w_in, w_o, forget_bias, conv_w, conv_b, conv_ln_g, conv_ln_b, ffn_w_in, ffn_w_out, ln_g, ln_b):
    bsz, seq, d = x.shape
    depth = w_in.shape[0]
    cw = conv_w.shape[-1]
    heads_f = forget_bias.shape[-1]
    fw = heads_f * HEAD_DIM
    dw = d - cw - fw
    alpha = (2 * depth) ** 0.25
    t = bsz * seq
    cos, slo, shi = _rope_tables(seq)

    ffn_in = ffn_w_in.astype(BF16)
    ffn_out = ffn_w_out.astype(BF16)
    fb =jnp.repeat(forget_bias, HEAD_DIM, axis=1)[:, None, :]
    row3 = lambda a: a[:, None, :]
    cb, cg, cbeta = row3(conv_b), row3(conv_ln_g), row3(conv_ln_b)
    g4, b4 = ln_g[:, :, None, :], ln_b[:, :, None, :]
    b3 = lambda a: a.reshape(bsz, seq, a.shape[-1])

    h = x.reshape(t, d)
    for l in range(depth):
        h = _ffn_ln(h, ffn_in, ffn_out, g4, b4, l, 0, 0, alpha=alpha)
        ya, qa, ka, va, dq, dk, dv = _mixer_proj(h, w_in, fb, cos, slo, shi, conv_w, cb, cg, cbeta, l,
                                                 bsz=bsz, fw=fw, dw=dw)
        yb = _fox_attention(qa, ka, va)
        yc = _dilated_attention(b3(dq), b3(dk), b3(dv))
        h = _out_proj_ln(h, ya, yb.reshape(t, fw), yc.reshape(t, dw),
                         w_o, g4, b4, l, 1, alpha=alpha)
        h = _ffn_ln(h, ffn_in, ffn_out, g4, b4, l, 1, 2, alpha=alpha)
    return h.reshape(bsz, seq, d)
```

```python
import functools

import jax
import jax.numpy as jnp
from jax import lax
from jax.experimental import pallas as pl
from jax.experimental.pallas import tpu as pltpu

HEAD_DIM = 64
CONV_K = 31
DIL_CONFIGS = ((128, 1), (512, 4), (2048, 16))
ROPE_THETA = 10000.0
LN_EPS = 1e-5
LANES = 128
HEADS_PER_SLAB = LANES // HEAD_DIM
NEG_BIG = -1e30
LOG2_E = 1.4426950408889634
SCORE_SCALE = HEAD_DIM ** -0.5 * LOG2_E
VMEM_LIMIT = 56 * 1024 * 1024

F32 = jnp.float32
BF16 = jnp.bfloat16


def _layer_norm(y, g, b):
    mu = jnp.mean(y, axis=-1, keepdims=True)
    yc = y - mu
    var = jnp.mean(yc * yc, axis=-1, keepdims=True)
    return yc * lax.rsqrt(var + LN_EPS) * g + b


def _params(*sem):
    return pltpu.CompilerParams(dimension_semantics=sem, vmem_limit_bytes=VMEM_LIMIT)


def _resident(arr, *lead):
    rest = arr.shape[len(lead):]
    index = tuple(lead) + (0,) * len(rest)
    return pl.BlockSpec((None,) * len(lead) + rest, lambda *_: index, pipeline_mode=pl.Buffered(1))


def _ffn_kernel(x_ref, win_ref, wout_ref, g_ref, b_ref, o_ref, acc_ref, *, alpha, d_ff, chunk):
    x = x_ref[...]
    xb = x.astype(BF16)
    for c in range(d_ff // chunk):
        lo = c * chunk
        gate = jnp.dot(xb, win_ref[:, lo:lo + chunk], preferred_element_type=F32)
        up = jnp.dot(xb, win_ref[:, d_ff + lo:d_ff + lo + chunk], preferred_element_type=F32)
        hid = (gate * jax.nn.sigmoid(gate) * up).astype(BF16)
        part = jnp.dot(hid, wout_ref[lo:lo + chunk, :], preferred_element_type=F32)
        if c == 0:
            acc_ref[...] = part
        else:
            acc_ref[...] += part
    y = alpha * x + 0.5 * acc_ref[...]
    o_ref[...] = _layer_norm(y, g_ref[...], b_ref[...])


def _ffn_ln(x, w_in, w_out, g, b, l, j, n, *, alpha, tm=1024, chunk=256):
    t, d = x.shape
    d_ff = w_out.shape[-2]
    return pl.pallas_call(
        functools.partial(_ffn_kernel, alpha=alpha, d_ff=d_ff, chunk=chunk),
        grid=(t // tm,),
        in_specs=[pl.BlockSpec((tm, d), lambda i: (i, 0)),
                  _resident(w_in, l, j), _resident(w_out, l, j),
                  _resident(g, l, n), _resident(b, l, n)],
        out_specs=pl.BlockSpec((tm, d), lambda i: (i, 0)),
        out_shape=jax.ShapeDtypeStruct((t, d), F32),
        scratch_shapes=[pltpu.VMEM((tm, d), F32)],
        compiler_params=_params("parallel"),
        name="ffn_ln",
    )(x, w_in, w_out, g, b)


def _rope_slab(t, cos, sin_lo, sin_hi):
    nxt = pltpu.roll(t, LANES - HEAD_DIM // 2, axis=1)
    prv = pltpu.roll(t, HEAD_DIM // 2, axis=1)
    return t * cos + nxt * sin_lo + prv * sin_hi


def _split3(x):
    hi = x.astype(BF16).astype(F32)
    mid = (x - hi).astype(BF16).astype(F32)
    return hi, mid, x - hi - mid


def _store_fox_operands(cum, q, k, v, qa_ref, ka_ref, va_ref, rows):
    lane = lax.broadcasted_iota(jnp.int32, (1, LANES), 1)
    is_head = lane < HEAD_DIM
    aug = lane - HEAD_DIM
    q_ones = jnp.where(jnp.logical_and(aug >= 3, aug < 6), 1.0, 0.0)
    k_ones = jnp.where(jnp.logical_and(aug >= 0, aug < 3), 1.0, 0.0)
    k_bias = jnp.logical_and(aug >= 3, aug < 6)
    for p in range(q.shape[-1] // LANES):
        slab = slice(p * LANES, (p + 1) * LANES)
        split = _split3(cum[:, slab])
        for a in range(HEADS_PER_SLAB):
            h = HEADS_PER_SLAB * p + a
            odd = a == 1
            head_lanes = lambda x: pltpu.roll(x[:, slab], HEAD_DIM, axis=1) if odd else x[:, slab]
            hi, mid, lo = split if odd else [pltpu.roll(t, HEAD_DIM, axis=1) for t in split]
            q_aug = jnp.where(aug == 0, hi, jnp.where(aug == 1, mid, jnp.where(aug == 2, lo, q_ones)))
            k_aug = jnp.where(k_bias, -pltpu.roll(q_aug, 3, axis=1), k_ones)
            qa_ref[h, rows, :] = jnp.where(is_head, head_lanes(q), q_aug).astype(BF16)
            ka_ref[h, rows, :] = jnp.where(is_head, head_lanes(k), k_aug).astype(BF16)
            va_ref[h, rows, :] = jnp.where(is_head, head_lanes(v), 1.0).astype(BF16)


def _stage_mixer_weights(wraw_ref, w_ref, *, split, heads, rows):
    lane = lax.broadcasted_iota(jnp.int32, (1, LANES), 1)
    tail = wraw_ref.shape[1] - split - heads
    for r0 in range(0, wraw_ref.shape[0], rows):
        rs = slice(r0, r0 + rows)
        w_ref[rs, 0:split] = wraw_ref[rs, 0:split].astype(BF16)
        gates = wraw_ref[rs, split:split + heads]
        for p in range(heads // HEADS_PER_SLAB):
            even = jnp.broadcast_to(gates[:, HEADS_PER_SLAB * p:HEADS_PER_SLAB * p + 1], (rows, LANES))
            odd = jnp.broadcast_to(gates[:, HEADS_PER_SLAB * p + 1:HEADS_PER_SLAB * p + 2], (rows, LANES))
            w_ref[rs, split + p * LANES:split + (p + 1) * LANES] = (
                jnp.where(lane < HEAD_DIM, even, odd).astype(BF16))
        lo = split + heads * HEAD_DIM
        w_ref[rs, lo:lo + tail] = wraw_ref[rs, split + heads:split + heads + tail].astype(BF16)


def _conv_rows(pad_ref, t0, rows, halo, w_ref, cb_ref, g_ref, b_ref, o_ref):
    ch = pad_ref.shape[1]
    first = t0 + halo - (CONV_K - 1)
    slabs = [slice(c0, c0 + LANES) for c0 in range(0, ch, LANES)]
    accs = []
    sub = 8
    base = first // sub * sub
    span = rows + (first - base + CONV_K - 1 + sub - 1) // sub * sub
    for cs in slabs:
        win = pad_ref[base:base + span, cs]
        parts = []
        for b in range(sub):
            rolled = win if b == 0 else pltpu.roll(win, span - b, axis=0)
            offs = [o for o in range(b, span - rows + 1, sub) if 0 <= base + o - first < CONV_K]
            parts.append(functools.reduce(jnp.add, [
                w_ref[base + o - first:base + o - first + 1, cs] * rolled[o - b:o - b + rows, :]
                for o in offs]))
        while len(parts) > 1:
            parts = [parts[i] + parts[i + 1] for i in range(0, len(parts), 2)]
        accs.append(parts[0] + cb_ref[:, cs])
    mu = sum(jnp.sum(a, axis=-1, keepdims=True) for a in accs) / ch
    cen = [a - mu for a in accs]
    var = sum(jnp.sum(c * c, axis=-1, keepdims=True) for c in cen) / ch
    inv = lax.rsqrt(var + LN_EPS)
    for cs, c in zip(slabs, cen):
        y = c * inv * g_ref[:, cs] + b_ref[:, cs]
        o_ref[t0:t0 + rows, cs] = (y * jax.nn.sigmoid(y)).astype(o_ref.dtype)


def _proj_kernel(h_ref, wraw_ref, fb_ref, cos_ref, slo_ref, shi_ref, cw_ref, cb_ref, cg_ref, cbeta_ref,
                 ya_ref, qa_ref, ka_ref, va_ref, dq_ref, dk_ref, dv_ref, w_ref, carry_ref, pad_ref,
                 *, cw, fw, dw, nsb, blk, halo, conv_rows):
    tm = h_ref.shape[0]

    @pl.when(pl.program_id(0) == 0)
    def _():
        _stage_mixer_weights(wraw_ref, w_ref, split=2 * cw + 3 * fw, heads=fw // HEAD_DIM, rows=blk)

    @pl.when(pl.program_id(0) % nsb == 0)
    def _():
        carry_ref[...] = jnp.zeros_like(carry_ref)
        pad_ref[0:halo, :] = jnp.zeros((halo, cw), F32)

    @pl.when(pl.program_id(0) % nsb != 0)
    def _():
        pad_ref[0:halo, :] = pad_ref[tm:tm + halo, :]

    hb = h_ref[...].astype(BF16)

    def cols(lo, width):
        return jnp.dot(hb, w_ref[:, lo:lo + width], preferred_element_type=F32)

    scale = SCORE_SCALE
    val = cols(0, cw)
    gate = cols(cw, cw)
    pad_ref[halo:halo + tm, :] = val * jax.nn.sigmoid(gate)
    for t0 in range(0, tm, conv_rows):
        _conv_rows(pad_ref, t0, conv_rows, halo, cw_ref, cb_ref, cg_ref, cbeta_ref, ya_ref)
    base = 2 * cw
    fq = cols(base, fw) * scale
    fk = cols(base + fw, fw)
    fv = cols(base + 2 * fw, fw)
    log_f = jax.nn.log_sigmoid(cols(base + 3 * fw, fw) + fb_ref[...])

    r = lax.broadcasted_iota(jnp.int32, (blk, blk), 0)
    c = lax.broadcasted_iota(jnp.int32, (blk, blk), 1)
    tri = (c <= r).astype(BF16)
    cum = carry_ref[...]
    for r0 in range(0, hb.shape[0], blk):
        rows = slice(r0, r0 + blk)
        cum = cum[-1:, :]
        for part in _split3(log_f[rows]):
            cum = cum + jnp.dot(tri, part.astype(BF16), preferred_element_type=F32)
        _store_fox_operands(cum * LOG2_E, fq[rows], fk[rows], fv[rows], qa_ref, ka_ref, va_ref, rows)
    carry_ref[...] = cum[-1:, :]

    base = base + 4 * fw
    cos, slo, shi = cos_ref[...], slo_ref[...], shi_ref[...]
    q, k = cols(base, dw), cols(base + dw, dw)
    for j in range(dw // LANES):
        sl = slice(j * LANES, (j + 1) * LANES)
        dq_ref[:, sl] = _rope_slab(q[:, sl], cos, slo, shi) * scale
        dk_ref[:, sl] = _rope_slab(k[:, sl], cos, slo, shi)
    dv_ref[...] = cols(base + 2 * dw, dw)


def _mixer_proj(h, w, fb, cos, slo, shi, conv_w, conv_b, conv_g, conv_beta, l, *, bsz, fw, dw,
                tm=512, blk=128, halo=32, conv_rows=64):
    t, d = h.shape
    cw = conv_w.shape[-1]
    seq = t // bsz
    nsb = seq // tm
    heads = fw // HEAD_DIM
    row = lambda i: (i, 0)
    pos = lambda i: (i % nsb, 0)
    tok = lambda n, dt: (pl.BlockSpec((tm, n), row), jax.ShapeDtypeStruct((t, n), dt))
    fox = (pl.BlockSpec((None, heads, tm, LANES), lambda i: (i // nsb, 0, i % nsb, 0)),
           jax.ShapeDtypeStruct((bsz, heads, seq, LANES), BF16))
    outs = [tok(cw, BF16), fox, fox, fox, tok(dw, F32), tok(dw, F32), tok(dw, F32)]
    return pl.pallas_call(
        functools.partial(_proj_kernel, cw=cw, fw=fw, dw=dw, nsb=nsb, blk=blk, halo=halo, conv_rows=conv_rows),
        grid=(t // tm,),
        in_specs=[pl.BlockSpec((tm, d), row), _resident(w, l), _resident(fb, l),
                  pl.BlockSpec((tm, LANES), pos), pl.BlockSpec((tm, LANES), pos),
                  pl.BlockSpec((tm, LANES), pos),
                  _resident(conv_w, l), _resident(conv_b, l), _resident(conv_g, l), _resident(conv_beta, l)],
        out_specs=[spec for spec, _ in outs],
        out_shape=[shape for _, shape in outs],
        scratch_shapes=[pltpu.VMEM((d, 2 * cw + 4 * fw + 3 * dw), BF16), pltpu.VMEM((1, fw), F32),
                        pltpu.VMEM((halo + tm, cw), F32)],
        compiler_params=_params("arbitrary"),
        name="mixer_proj",
    )(h, w, fb, cos, slo, shi, conv_w, conv_b, conv_g, conv_beta)


def _fox_kernel(q_ref, k_ref, v_ref, o_ref, m_ref, acc_ref, *, tq):
    i = pl.program_id(2)
    half = tq // 2
    dn = (((1,), (1,)), ((), ()))

    def update(a, rows, k0, width, mask):
        keys = pl.ds(pl.multiple_of(k0, tq), width)
        s = lax.dot_general(q_ref[a, rows, :], k_ref[a, keys, :], dn, preferred_element_type=F32)
        if mask is not None:
            s = jnp.where(mask, s, NEG_BIG)
        m_old = m_ref[a, rows, :]
        m_new = jnp.maximum(m_old, jnp.max(s, axis=-1, keepdims=True))
        alpha = jnp.exp2(m_old - m_new)
        p = jnp.exp2(s - pltpu.repeat(m_new, width // LANES, axis=1))
        m_ref[a, rows, :] = m_new
        acc_ref[a, rows, :] = alpha * acc_ref[a, rows, :] + jnp.dot(
            p.astype(BF16), v_ref[a, keys, :], preferred_element_type=F32)

    m_ref[...] = jnp.full(m_ref.shape, NEG_BIG, F32)
    acc_ref[...] = jnp.zeros(acc_ref.shape, F32)
    lower = (lax.broadcasted_iota(jnp.int32, (half, half), 1)
             <= lax.broadcasted_iota(jnp.int32, (half, half), 0))
    upper = (lax.broadcasted_iota(jnp.int32, (half, tq), 1)
             <= lax.broadcasted_iota(jnp.int32, (half, tq), 0) + half)
    for a in range(HEADS_PER_SLAB):
        update(a, slice(0, half), i * tq, half, lower)
        update(a, slice(half, tq), i * tq, tq, upper)

    def body(j, _):
        for a in range(HEADS_PER_SLAB):
            update(a, slice(0, tq), j * tq, tq, None)
        return 0

    lax.fori_loop(0, i, body, 0)

    lane = lax.broadcasted_iota(jnp.int32, (1, LANES), 1)
    acc0, acc1 = acc_ref[0], acc_ref[1]
    out = jnp.where(lane < HEAD_DIM,
                    acc0 / pltpu.roll(acc0, HEAD_DIM, axis=1),
                    pltpu.roll(acc1, HEAD_DIM, axis=1) / acc1)
    o_ref[...] = out.astype(o_ref.dtype)


def _fox_attention(qa, ka, va, *, tq=1024):
    bsz, heads, seq, _ = qa.shape
    qspec = pl.BlockSpec((None, HEADS_PER_SLAB, tq, LANES), lambda b, p, i: (b, p, i, 0))
    kvspec = pl.BlockSpec((None, HEADS_PER_SLAB, seq, LANES), lambda b, p, i: (b, p, 0, 0))
    return pl.pallas_call(
        functools.partial(_fox_kernel, tq=tq),
        grid=(bsz, heads // HEADS_PER_SLAB, seq // tq),
        in_specs=[qspec, kvspec, kvspec],
        out_specs=pl.BlockSpec((None, tq, LANES), lambda b, p, i: (b, i, p)),
        out_shape=jax.ShapeDtypeStruct((bsz, seq, heads * HEAD_DIM), BF16),
        scratch_shapes=[pltpu.VMEM((HEADS_PER_SLAB, tq, LANES), F32),
                        pltpu.VMEM((HEADS_PER_SLAB, tq, LANES), F32)],
        compiler_params=_params("parallel", "parallel", "arbitrary"),
        name="fox_attention",
    )(qa, ka, va)


def _dil_kernel(q_ref, k_ref, v_ref, o_ref, bias_ref, bias2_ref, m_ref, l_ref, acc_ref, *,
                blk, unroll, rows_out):
    seq = q_ref.shape[0]
    win = 2 * blk
    lane = lax.broadcasted_iota(jnp.int32, (1, LANES), 1)
    head0 = lane < HEAD_DIM
    dist0 = (lax.broadcasted_iota(jnp.int32, (win, win), 0) % blk
             - lax.broadcasted_iota(jnp.int32, (win, win), 1))
    row = lax.broadcasted_iota(jnp.int32, (2 * win, win), 0)
    col = lax.broadcasted_iota(jnp.int32, (2 * win, win), 1)
    bias_ref[0] = jnp.where(dist0 >= 0, 0.0, NEG_BIG)
    bias_ref[1] = jnp.where(jnp.logical_and(dist0 + blk >= 0, dist0 <= 0), 0.0, NEG_BIG)
    dist_all = (row % win) - col
    bias2_ref[...] = jnp.where(jnp.logical_and(dist_all >= 0, dist_all <= blk), 0.0, NEG_BIG)
    ones = jnp.ones((win, LANES), BF16)
    dn = (((1,), (1,)), ((), ()))

    def attend(bi, rows, wrows, bias):
        nq = rows.size
        qb = q_ref[rows, :].astype(BF16)
        zero = jnp.zeros_like(qb)
        qs = jnp.concatenate([jnp.where(head0, qb, zero), jnp.where(head0, zero, qb)], axis=0)
        kw = k_ref[wrows, :].astype(BF16)
        vw = jnp.concatenate([v_ref[wrows, :].astype(BF16), ones], axis=1)
        s = lax.dot_general(qs, kw, dn, preferred_element_type=F32) + bias
        m = jnp.max(s, axis=-1, keepdims=True)
        p = jnp.exp2(s - m).astype(BF16)
        pv = jnp.dot(p, vw, preferred_element_type=F32)
        mb = jnp.broadcast_to(m, (2 * nq, LANES))
        m_ref[bi, rows, :] = jnp.where(head0, mb[:nq], mb[nq:])
        acc_ref[bi, rows, :] = jnp.where(head0, pv[:nq, :LANES], pv[nq:, :LANES])
        l_ref[bi, rows, :] = jnp.where(head0, pv[:nq, LANES:], pv[nq:, LANES:])

    for bi, (_, dil) in enumerate(DIL_CONFIGS):
        nb = seq // (dil * blk)
        ds = (lambda start, size, dil=dil: pl.ds(start, size, stride=dil) if dil > 1 else pl.ds(start, size))

        if nb == 2:
            def body(r, _, bi=bi, ds=ds):
                attend(bi, ds(r, win), ds(r, win), bias2_ref[...])
                return 0

            lax.fori_loop(0, dil, body, 0, unroll=unroll // 2)
            continue

        def body(idx, _, bi=bi, dil=dil, nb=nb, ds=ds):
            r = idx // nb
            n = idx % nb
            start = r + n * (dil * blk)
            has_prev = (n > 0).astype(jnp.int32)
            attend(bi, ds(start, blk), ds(start - has_prev * (dil * blk), win), bias_ref[has_prev])
            return 0

        lax.fori_loop(0, seq // blk, body, 0, unroll=unroll)

    def merge(i, _):
        rows = pl.ds(pl.multiple_of(i * rows_out, rows_out), rows_out)
        ms = [m_ref[bi, rows, :] for bi in range(len(DIL_CONFIGS))]
        m_all = functools.reduce(jnp.maximum, ms)
        ws = [jnp.exp2(m - m_all) for m in ms]
        num = sum(w * acc_ref[bi, rows, :] for bi, w in enumerate(ws))
        den = sum(w * l_ref[bi, rows, :] for bi, w in enumerate(ws))
        o_ref[rows, :] = (num / den).astype(o_ref.dtype)
        return 0

    lax.fori_loop(0, seq // rows_out, merge, 0)


def _dilated_attention(q, k, v, *, blk=128, unroll=16, rows_out=256):
    bsz, seq, width = q.shape
    nbr = len(DIL_CONFIGS)
    spec = pl.BlockSpec((None, seq, LANES), lambda b, p: (b, 0, p))
    return pl.pallas_call(
        functools.partial(_dil_kernel, blk=blk, unroll=unroll, rows_out=rows_out),
        grid=(bsz, width // LANES),
        in_specs=[spec, spec, spec],
        out_specs=spec,
        out_shape=jax.ShapeDtypeStruct((bsz, seq, width), BF16),
        scratch_shapes=[pltpu.VMEM((2, 2 * blk, 2 * blk), F32), pltpu.VMEM((4 * blk, 2 * blk), F32)]
        + [pltpu.VMEM((nbr, seq, LANES), F32)] * 3,
        compiler_params=_params("parallel", "parallel"),
        name="dilated_attention",
    )(q, k, v)


def _out_ffn_kernel(h_ref, ya_ref, yb_ref, yc_ref, wraw_ref, g1_ref, b1_ref, win_ref, wout_ref, g2_ref, b2_ref,
                    o_ref, w_ref, mid_ref, acc_ref, *, alpha, cw, fw, d_ff, chunk):
    @pl.when(pl.program_id(0) == 0)
    def _():
        w_ref[...] = wraw_ref[...].astype(BF16)

    y = jnp.dot(ya_ref[...], w_ref[0:cw, :], preferred_element_type=F32)
    y = y + jnp.dot(yb_ref[...], w_ref[cw:cw + fw, :], preferred_element_type=F32)
    y = y + jnp.dot(yc_ref[...], w_ref[cw + fw:, :], preferred_element_type=F32)
    mid_ref[...] = _layer_norm(alpha * h_ref[...] + y, g1_ref[...], b1_ref[...])
    _ffn_kernel(mid_ref, win_ref, wout_ref, g2_ref, b2_ref, o_ref, acc_ref, alpha=alpha, d_ff=d_ff, chunk=chunk)


def _out_proj_ffn_ln(h, ya, yb, yc, w, ffn_in, ffn_out, g, b, l, *, alpha, tm=512, chunk=256):
    t, d = h.shape
    d_ff = ffn_out.shape[-2]
    row = lambda i: (i, 0)
    return pl.pallas_call(
        functools.partial(_out_ffn_kernel, alpha=alpha, cw=ya.shape[1], fw=yb.shape[1], d_ff=d_ff, chunk=chunk),
        grid=(t // tm,),
        in_specs=[pl.BlockSpec((tm, d), row), pl.BlockSpec((tm, ya.shape[1]), row),
                  pl.BlockSpec((tm, yb.shape[1]), row), pl.BlockSpec((tm, yc.shape[1]), row),
                  _resident(w, l), _resident(g, l, 1), _resident(b, l, 1),
                  _resident(ffn_in, l, 1), _resident(ffn_out, l, 1), _resident(g, l, 2), _resident(b, l, 2)],
        out_specs=pl.BlockSpec((tm, d), row),
        out_shape=jax.ShapeDtypeStruct((t, d), F32),
        scratch_shapes=[pltpu.VMEM((d, d), BF16), pltpu.VMEM((tm, d), F32), pltpu.VMEM((tm, d), F32)],
        compiler_params=_params("arbitrary"),
        name="out_proj_ffn_ln",
    )(h, ya, yb, yc, w, g, b, ffn_in, ffn_out, g, b)


def _rope_tables(seq):
    inv = 1.0 / (ROPE_THETA ** (jnp.arange(0, HEAD_DIM, 2, dtype=F32) / HEAD_DIM))
    ang = jnp.arange(seq, dtype=F32)[:, None] * inv[None, :]
    ang = jnp.concatenate([ang, ang], axis=-1)
    cos, sin = jnp.cos(ang), jnp.sin(ang)
    first_half = (jnp.arange(HEAD_DIM) < HEAD_DIM // 2)[None, :]
    sin_lo = jnp.where(first_half, -sin, 0.0)
    sin_hi = jnp.where(first_half, 0.0, sin)
    rep = lambda a: jnp.tile(a, (1, HEADS_PER_SLAB))
    return rep(cos), rep(sin_lo), rep(sin_hi)


def kernel(x, w_in, w_o, forget_bias, conv_w, conv_b, conv_ln_g, conv_ln_b, ffn_w_in, ffn_w_out, ln_g, ln_b):
    bsz, seq, d = x.shape
    depth = w_in.shape[0]
    cw = conv_w.shape[-1]
    heads_f = forget_bias.shape[-1]
    fw = heads_f * HEAD_DIM
    dw = d - cw - fw
    alpha = (2 * depth) ** 0.25
    t = bsz * seq
    cos, slo, shi = _rope_tables(seq)

    ffn_in = ffn_w_in.astype(BF16)
    ffn_out = ffn_w_out.astype(BF16)
    fb =jnp.repeat(forget_bias, HEAD_DIM, axis=1)[:, None, :]
    row3 = lambda a: a[:, None, :]
    cb, cg, cbeta = row3(conv_b), row3(conv_ln_g), row3(conv_ln_b)
    g4, b4 = ln_g[:, :, None, :], ln_b[:, :, None, :]
    b3 = lambda a: a.reshape(bsz, seq, a.shape[-1])

    h = x.reshape(t, d)
    for l in range(depth):
        h = _ffn_ln(h, ffn_in, ffn_out, g4, b4, l, 0, 0, alpha=alpha)
        ya, qa, ka, va, dq, dk, dv = _mixer_proj(h, w_in, fb, cos, slo, shi, conv_w, cb, cg, cbeta, l,
                                                 bsz=bsz, fw=fw, dw=dw)
        yb = _fox_attention(qa, ka, va)
        yc = _dilated_attention(b3(dq), b3(dk), b3(dv))
        h = _out_proj_ffn_ln(h, ya, yb.reshape(t, fw), yc.reshape(t, dw), w_o, ffn_in, ffn_out, g4, b4, l,
                             alpha=alpha)
    return h.reshape(bsz, seq, d)
```

```python
import functools

import jax
import jax.numpy as jnp
from jax import lax
from jax.experimental import pallas as pl
from jax.experimental.pallas import tpu as pltpu

HEAD_DIM = 64
CONV_K = 31
DIL_CONFIGS = ((128, 1), (512, 4), (2048, 16))
ROPE_THETA = 10000.0
LN_EPS = 1e-5
LANES = 128
HEADS_PER_SLAB = LANES // HEAD_DIM
NEG_BIG = -1e30
LOG2_E = 1.4426950408889634
SCORE_SCALE = HEAD_DIM ** -0.5 * LOG2_E
VMEM_LIMIT = 56 * 1024 * 1024

F32 = jnp.float32
BF16 = jnp.bfloat16


def _layer_norm(y, g, b):
    mu = jnp.mean(y, axis=-1, keepdims=True)
    yc = y - mu
    var = jnp.mean(yc * yc, axis=-1, keepdims=True)
    return yc * lax.rsqrt(var + LN_EPS) * g + b


def _params(*sem):
    return pltpu.CompilerParams(dimension_semantics=sem, vmem_limit_bytes=VMEM_LIMIT)


def _resident(arr, *lead):
    rest = arr.shape[len(lead):]
    index = tuple(lead) + (0,) * len(rest)
    return pl.BlockSpec((None,) * len(lead) + rest, lambda *_: index, pipeline_mode=pl.Buffered(1))


def _ffn_kernel(x_ref, win_ref, wout_ref, g_ref, b_ref, o_ref, acc_ref, *, alpha, d_ff, chunk):
    x = x_ref[...]
    xb = x.astype(BF16)
    for c in range(d_ff // chunk):
        lo = c * chunk
        gate = jnp.dot(xb, win_ref[:, lo:lo + chunk], preferred_element_type=F32)
        up = jnp.dot(xb, win_ref[:, d_ff + lo:d_ff + lo + chunk], preferred_element_type=F32)
        hid = (gate * jax.nn.sigmoid(gate) * up).astype(BF16)
        part = jnp.dot(hid, wout_ref[lo:lo + chunk, :], preferred_element_type=F32)
        if c == 0:
            acc_ref[...] = part
        else:
            acc_ref[...] += part
    y = alpha * x + 0.5 * acc_ref[...]
    o_ref[...] = _layer_norm(y, g_ref[...], b_ref[...])


def _next_weight_cast(weights, nxt, steps):
    if nxt is None:
        return [], [], [], []
    l, j = nxt
    in_specs, out_specs, out_shapes = [], [], []
    for w in weights:
        rows, cols = w.shape[-2:]
        n = steps
        while rows % n or (rows // n) % 16:
            n //= 2
        per = steps // n
        in_specs.append(pl.BlockSpec((None, None, rows // n, cols), lambda i, per=per: (l, j, i // per, 0)))
        out_specs.append(pl.BlockSpec((rows // n, cols), lambda i, per=per: (i // per, 0)))
        out_shapes.append(jax.ShapeDtypeStruct((rows, cols), BF16))
    return list(weights), in_specs, out_specs, out_shapes


def _ffn_ln_kernel(x_ref, win_ref, wout_ref, g_ref, b_ref, *rest, n_cast, **kw):
    srcs, (o_ref, *dsts), acc_ref = rest[:n_cast], rest[n_cast:-1], rest[-1]
    for src, dst in zip(srcs, dsts):
        dst[...] = src[...].astype(BF16)
    _ffn_kernel(x_ref, win_ref, wout_ref, g_ref, b_ref, o_ref, acc_ref, **kw)


def _ffn_ln(x, w_in, w_out, g, b, l, n, next_f32, nxt, *, alpha, tm=1024, chunk=256):
    t, d = x.shape
    d_ff = w_out.shape[0]
    srcs, cast_in, cast_out, cast_shapes = _next_weight_cast(next_f32, nxt, t // tm)
    outs = pl.pallas_call(
        functools.partial(_ffn_ln_kernel, n_cast=len(srcs), alpha=alpha, d_ff=d_ff, chunk=chunk),
        grid=(t // tm,),
        in_specs=[pl.BlockSpec((tm, d), lambda i: (i, 0)), _resident(w_in), _resident(w_out),
                  _resident(g, l, n), _resident(b, l, n)] + cast_in,
        out_specs=[pl.BlockSpec((tm, d), lambda i: (i, 0))] + cast_out,
        out_shape=[jax.ShapeDtypeStruct((t, d), F32)] + cast_shapes,
        scratch_shapes=[pltpu.VMEM((tm, d), F32)],
        compiler_params=_params("arbitrary"),
        name="ffn_ln",
    )(x, w_in, w_out, g, b, *srcs)
    return outs[0], outs[1:]


def _rope_slab(t, cos, sin_lo, sin_hi):
    nxt = pltpu.roll(t, LANES - HEAD_DIM // 2, axis=1)
    prv = pltpu.roll(t, HEAD_DIM // 2, axis=1)
    return t * cos + nxt * sin_lo + prv * sin_hi


def _split3(x):
    hi = x.astype(BF16).astype(F32)
    mid = (x - hi).astype(BF16).astype(F32)
    return hi, mid, x - hi - mid


def _store_fox_operands(cum, q, k, v, qa_ref, ka_ref, va_ref, rows):
    lane = lax.broadcasted_iota(jnp.int32, (1, LANES), 1)
    is_head = lane < HEAD_DIM
    aug = lane - HEAD_DIM
    q_ones = jnp.where(jnp.logical_and(aug >= 3, aug < 6), 1.0, 0.0)
    k_ones = jnp.where(jnp.logical_and(aug >= 0, aug < 3), 1.0, 0.0)
    k_bias = jnp.logical_and(aug >= 3, aug < 6)
    for p in range(q.shape[-1] // LANES):
        slab = slice(p * LANES, (p + 1) * LANES)
        split = _split3(cum[:, slab])
        for a in range(HEADS_PER_SLAB):
            h = HEADS_PER_SLAB * p + a
            odd = a == 1
            head_lanes = lambda x: pltpu.roll(x[:, slab], HEAD_DIM, axis=1) if odd else x[:, slab]
            hi, mid, lo = split if odd else [pltpu.roll(t, HEAD_DIM, axis=1) for t in split]
            q_aug = jnp.where(aug == 0, hi, jnp.where(aug == 1, mid, jnp.where(aug == 2, lo, q_ones)))
            k_aug = jnp.where(k_bias, -pltpu.roll(q_aug, 3, axis=1), k_ones)
            qa_ref[h, rows, :] = jnp.where(is_head, head_lanes(q), q_aug).astype(BF16)
            ka_ref[h, rows, :] = jnp.where(is_head, head_lanes(k), k_aug).astype(BF16)
            va_ref[h, rows, :] = jnp.where(is_head, head_lanes(v), 1.0).astype(BF16)


def _stage_mixer_weights(wraw_ref, w_ref, *, split, heads, rows):
    lane = lax.broadcasted_iota(jnp.int32, (1, LANES), 1)
    tail = wraw_ref.shape[1] - split - heads
    for r0 in range(0, wraw_ref.shape[0], rows):
        rs = slice(r0, r0 + rows)
        w_ref[rs, 0:split] = wraw_ref[rs, 0:split].astype(BF16)
        gates = wraw_ref[rs, split:split + heads]
        for p in range(heads // HEADS_PER_SLAB):
            even = jnp.broadcast_to(gates[:, HEADS_PER_SLAB * p:HEADS_PER_SLAB * p + 1], (rows, LANES))
            odd = jnp.broadcast_to(gates[:, HEADS_PER_SLAB * p + 1:HEADS_PER_SLAB * p + 2], (rows, LANES))
            w_ref[rs, split + p * LANES:split + (p + 1) * LANES] = (
                jnp.where(lane < HEAD_DIM, even, odd).astype(BF16))
        lo = split + heads * HEAD_DIM
        w_ref[rs, lo:lo + tail] = wraw_ref[rs, split + heads:split + heads + tail].astype(BF16)


def _conv_rows(pad_ref, t0, rows, halo, w_ref, cb_ref, g_ref, b_ref, o_ref):
    ch = pad_ref.shape[1]
    first = t0 + halo - (CONV_K - 1)
    slabs = [slice(c0, c0 + LANES) for c0 in range(0, ch, LANES)]
    accs = []
    sub = 8
    base = first // sub * sub
    span = rows + (first - base + CONV_K - 1 + sub - 1) // sub * sub
    for cs in slabs:
        win = pad_ref[base:base + span, cs]
        parts = []
        for b in range(sub):
            rolled = win if b == 0 else pltpu.roll(win, span - b, axis=0)
            offs = [o for o in range(b, span - rows + 1, sub) if 0 <= base + o - first < CONV_K]
            parts.append(functools.reduce(jnp.add, [
                w_ref[base + o - first:base + o - first + 1, cs] * rolled[o - b:o - b + rows, :]
                for o in offs]))
        while len(parts) > 1:
            parts = [parts[i] + parts[i + 1] for i in range(0, len(parts), 2)]
        accs.append(parts[0] + cb_ref[:, cs])
    mu = sum(jnp.sum(a, axis=-1, keepdims=True) for a in accs) / ch
    cen = [a - mu for a in accs]
    var = sum(jnp.sum(c * c, axis=-1, keepdims=True) for c in cen) / ch
    inv = lax.rsqrt(var + LN_EPS)
    for cs, c in zip(slabs, cen):
        y = c * inv * g_ref[:, cs] + b_ref[:, cs]
        o_ref[t0:t0 + rows, cs] = (y * jax.nn.sigmoid(y)).astype(o_ref.dtype)


def _proj_kernel(h_ref, wraw_ref, fb_ref, cos_ref, slo_ref, shi_ref, cw_ref, cb_ref, cg_ref, cbeta_ref,
                 ya_ref, qa_ref, ka_ref, va_ref, dq_ref, dk_ref, dv_ref, w_ref, carry_ref, pad_ref,
                 *, cw, fw, dw, nsb, blk, halo, conv_rows):
    tm = h_ref.shape[0]

    @pl.when(pl.program_id(0) == 0)
    def _():
        _stage_mixer_weights(wraw_ref, w_ref, split=2 * cw + 3 * fw, heads=fw // HEAD_DIM, rows=blk)

    @pl.when(pl.program_id(0) % nsb == 0)
    def _():
        carry_ref[...] = jnp.zeros_like(carry_ref)
        pad_ref[0:halo, :] = jnp.zeros((halo, cw), F32)

    @pl.when(pl.program_id(0) % nsb != 0)
    def _():
        pad_ref[0:halo, :] = pad_ref[tm:tm + halo, :]

    hb = h_ref[...].astype(BF16)

    def cols(lo, width):
        return jnp.dot(hb, w_ref[:, lo:lo + width], preferred_element_type=F32)

    scale = SCORE_SCALE
    val = cols(0, cw)
    gate = cols(cw, cw)
    pad_ref[halo:halo + tm, :] = val * jax.nn.sigmoid(gate)
    for t0 in range(0, tm, conv_rows):
        _conv_rows(pad_ref, t0, conv_rows, halo, cw_ref, cb_ref, cg_ref, cbeta_ref, ya_ref)
    base = 2 * cw
    fq = cols(base, fw) * scale
    fk = cols(base + fw, fw)
    fv = cols(base + 2 * fw, fw)
    log_f = jax.nn.log_sigmoid(cols(base + 3 * fw, fw) + fb_ref[...])

    r = lax.broadcasted_iota(jnp.int32, (blk, blk), 0)
    c = lax.broadcasted_iota(jnp.int32, (blk, blk), 1)
    tri = (c <= r).astype(BF16)
    cum = carry_ref[...]
    for r0 in range(0, hb.shape[0], blk):
        rows = slice(r0, r0 + blk)
        cum = cum[-1:, :]
        for part in _split3(log_f[rows]):
            cum = cum + jnp.dot(tri, part.astype(BF16), preferred_element_type=F32)
        _store_fox_operands(cum * LOG2_E, fq[rows], fk[rows], fv[rows], qa_ref, ka_ref, va_ref, rows)
    carry_ref[...] = cum[-1:, :]

    base = base + 4 * fw
    cos, slo, shi = cos_ref[...], slo_ref[...], shi_ref[...]
    q, k = cols(base, dw), cols(base + dw, dw)
    for j in range(dw // LANES):
        sl = slice(j * LANES, (j + 1) * LANES)
        dq_ref[:, sl] = _rope_slab(q[:, sl], cos, slo, shi) * scale
        dk_ref[:, sl] = _rope_slab(k[:, sl], cos, slo, shi)
    dv_ref[...] = cols(base + 2 * dw, dw)


def _mixer_proj(h, w, fb, cos, slo, shi, conv_w, conv_b, conv_g, conv_beta, l, *, bsz, fw, dw,
                tm=512, blk=128, halo=32, conv_rows=64):
    t, d = h.shape
    cw = conv_w.shape[-1]
    seq = t // bsz
    nsb = seq // tm
    heads = fw // HEAD_DIM
    row = lambda i: (i, 0)
    pos = lambda i: (i % nsb, 0)
    tok = lambda n, dt: (pl.BlockSpec((tm, n), row), jax.ShapeDtypeStruct((t, n), dt))
    fox = (pl.BlockSpec((None, heads, tm, LANES), lambda i: (i // nsb, 0, i % nsb, 0)),
           jax.ShapeDtypeStruct((bsz, heads, seq, LANES), BF16))
    outs = [tok(cw, BF16), fox, fox, fox, tok(dw, F32), tok(dw, F32), tok(dw, F32)]
    return pl.pallas_call(
        functools.partial(_proj_kernel, cw=cw, fw=fw, dw=dw, nsb=nsb, blk=blk, halo=halo, conv_rows=conv_rows),
        grid=(t // tm,),
        in_specs=[pl.BlockSpec((tm, d), row), _resident(w, l), _resident(fb, l),
                  pl.BlockSpec((tm, LANES), pos), pl.BlockSpec((tm, LANES), pos),
                  pl.BlockSpec((tm, LANES), pos),
                  _resident(conv_w, l), _resident(conv_b, l), _resident(conv_g, l), _resident(conv_beta, l)],
        out_specs=[spec for spec, _ in outs],
        out_shape=[shape for _, shape in outs],
        scratch_shapes=[pltpu.VMEM((d, 2 * cw + 4 * fw + 3 * dw), BF16), pltpu.VMEM((1, fw), F32),
                        pltpu.VMEM((halo + tm, cw), F32)],
        compiler_params=_params("arbitrary"),
        name="mixer_proj",
    )(h, w, fb, cos, slo, shi, conv_w, conv_b, conv_g, conv_beta)


def _fox_kernel(q_ref, k_ref, v_ref, o_ref, m_ref, acc_ref, *, tq):
    i = pl.program_id(2)
    half = tq // 2
    dn = (((1,), (1,)), ((), ()))

    def update(a, rows, k0, width, mask):
        keys = pl.ds(pl.multiple_of(k0, tq), width)
        s = lax.dot_general(q_ref[a, rows, :], k_ref[a, keys, :], dn, preferred_element_type=F32)
        if mask is not None:
            s = jnp.where(mask, s, NEG_BIG)
        m_old = m_ref[a, rows, :]
        m_new = jnp.maximum(m_old, jnp.max(s, axis=-1, keepdims=True))
        alpha = jnp.exp2(m_old - m_new)
        p = jnp.exp2(s - pltpu.repeat(m_new, width // LANES, axis=1))
        m_ref[a, rows, :] = m_new
        acc_ref[a, rows, :] = alpha * acc_ref[a, rows, :] + jnp.dot(
            p.astype(BF16), v_ref[a, keys, :], preferred_element_type=F32)

    m_ref[...] = jnp.full(m_ref.shape, NEG_BIG, F32)
    acc_ref[...] = jnp.zeros(acc_ref.shape, F32)
    lower = (lax.broadcasted_iota(jnp.int32, (half, half), 1)
             <= lax.broadcasted_iota(jnp.int32, (half, half), 0))
    upper = (lax.broadcasted_iota(jnp.int32, (half, tq), 1)
             <= lax.broadcasted_iota(jnp.int32, (half, tq), 0) + half)
    for a in range(HEADS_PER_SLAB):
        update(a, slice(0, half), i * tq, half, lower)
        update(a, slice(half, tq), i * tq, tq, upper)

    def body(j, _):
        for a in range(HEADS_PER_SLAB):
            update(a, slice(0, tq), j * tq, tq, None)
        return 0

    lax.fori_loop(0, i, body, 0)

    lane = lax.broadcasted_iota(jnp.int32, (1, LANES), 1)
    acc0, acc1 = acc_ref[0], acc_ref[1]
    out = jnp.where(lane < HEAD_DIM,
                    acc0 / pltpu.roll(acc0, HEAD_DIM, axis=1),
                    pltpu.roll(acc1, HEAD_DIM, axis=1) / acc1)
    o_ref[...] = out.astype(o_ref.dtype)


def _fox_attention(qa, ka, va, *, tq=1024):
    bsz, heads, seq, _ = qa.shape
    qspec = pl.BlockSpec((None, HEADS_PER_SLAB, tq, LANES), lambda b, p, i: (b, p, i, 0))
    kvspec = pl.BlockSpec((None, HEADS_PER_SLAB, seq, LANES), lambda b, p, i: (b, p, 0, 0))
    return pl.pallas_call(
        functools.partial(_fox_kernel, tq=tq),
        grid=(bsz, heads // HEADS_PER_SLAB, seq // tq),
        in_specs=[qspec, kvspec, kvspec],
        out_specs=pl.BlockSpec((None, tq, LANES), lambda b, p, i: (b, i, p)),
        out_shape=jax.ShapeDtypeStruct((bsz, seq, heads * HEAD_DIM), BF16),
        scratch_shapes=[pltpu.VMEM((HEADS_PER_SLAB, tq, LANES), F32),
                        pltpu.VMEM((HEADS_PER_SLAB, tq, LANES), F32)],
        compiler_params=_params("parallel", "parallel", "arbitrary"),
        name="fox_attention",
    )(qa, ka, va)


def _dil_kernel(q_ref, k_ref, v_ref, o_ref, bias_ref, bias2_ref, m_ref, l_ref, acc_ref, *,
                blk, unroll, rows_out):
    seq = q_ref.shape[0]
    win = 2 * blk
    lane = lax.broadcasted_iota(jnp.int32, (1, LANES), 1)
    head0 = lane < HEAD_DIM
    dist0 = (lax.broadcasted_iota(jnp.int32, (win, win), 0) % blk
             - lax.broadcasted_iota(jnp.int32, (win, win), 1))
    row = lax.broadcasted_iota(jnp.int32, (2 * win, win), 0)
    col = lax.broadcasted_iota(jnp.int32, (2 * win, win), 1)
    bias_ref[0] = jnp.where(dist0 >= 0, 0.0, NEG_BIG)
    bias_ref[1] = jnp.where(jnp.logical_and(dist0 + blk >= 0, dist0 <= 0), 0.0, NEG_BIG)
    dist_all = (row % win) - col
    bias2_ref[...] = jnp.where(jnp.logical_and(dist_all >= 0, dist_all <= blk), 0.0, NEG_BIG)
    ones = jnp.ones((win, LANES), BF16)
    dn = (((1,), (1,)), ((), ()))

    def attend(bi, rows, wrows, bias):
        nq = rows.size
        qb = q_ref[rows, :].astype(BF16)
        zero = jnp.zeros_like(qb)
        qs = jnp.concatenate([jnp.where(head0, qb, zero), jnp.where(head0, zero, qb)], axis=0)
        kw = k_ref[wrows, :].astype(BF16)
        vw = jnp.concatenate([v_ref[wrows, :].astype(BF16), ones], axis=1)
        s = lax.dot_general(qs, kw, dn, preferred_element_type=F32) + bias
        m = jnp.max(s, axis=-1, keepdims=True)
        p = jnp.exp2(s - m).astype(BF16)
        pv = jnp.dot(p, vw, preferred_element_type=F32)
        mb = jnp.broadcast_to(m, (2 * nq, LANES))
        m_ref[bi, rows, :] = jnp.where(head0, mb[:nq], mb[nq:])
        acc_ref[bi, rows, :] = jnp.where(head0, pv[:nq, :LANES], pv[nq:, :LANES])
        l_ref[bi, rows, :] = jnp.where(head0, pv[:nq, LANES:], pv[nq:, LANES:])

    for bi, (_, dil) in enumerate(DIL_CONFIGS):
        nb = seq // (dil * blk)
        ds = (lambda start, size, dil=dil: pl.ds(start, size, stride=dil) if dil > 1 else pl.ds(start, size))

        if nb == 2:
            def body(r, _, bi=bi, ds=ds):
                attend(bi, ds(r, win), ds(r, win), bias2_ref[...])
                return 0

            lax.fori_loop(0, dil, body, 0, unroll=unroll // 2)
            continue

        def body(idx, _, bi=bi, dil=dil, nb=nb, ds=ds):
            r = idx // nb
            n = idx % nb
            start = r + n * (dil * blk)
            has_prev = (n > 0).astype(jnp.int32)
            attend(bi, ds(start, blk), ds(start - has_prev * (dil * blk), win), bias_ref[has_prev])
            return 0

        lax.fori_loop(0, seq // blk, body, 0, unroll=unroll)

    def merge(i, _):
        rows = pl.ds(pl.multiple_of(i * rows_out, rows_out), rows_out)
        ms = [m_ref[bi, rows, :] for bi in range(len(DIL_CONFIGS))]
        m_all = functools.reduce(jnp.maximum, ms)
        ws = [jnp.exp2(m - m_all) for m in ms]
        num = sum(w * acc_ref[bi, rows, :] for bi, w in enumerate(ws))
        den = sum(w * l_ref[bi, rows, :] for bi, w in enumerate(ws))
        o_ref[rows, :] = (num / den).astype(o_ref.dtype)
        return 0

    lax.fori_loop(0, seq // rows_out, merge, 0)


def _dilated_attention(q, k, v, *, blk=128, unroll=16, rows_out=256):
    bsz, seq, width = q.shape
    nbr = len(DIL_CONFIGS)
    spec = pl.BlockSpec((None, seq, LANES), lambda b, p: (b, 0, p))
    return pl.pallas_call(
        functools.partial(_dil_kernel, blk=blk, unroll=unroll, rows_out=rows_out),
        grid=(bsz, width // LANES),
        in_specs=[spec, spec, spec],
        out_specs=spec,
        out_shape=jax.ShapeDtypeStruct((bsz, seq, width), BF16),
        scratch_shapes=[pltpu.VMEM((2, 2 * blk, 2 * blk), F32), pltpu.VMEM((4 * blk, 2 * blk), F32)]
        + [pltpu.VMEM((nbr, seq, LANES), F32)] * 3,
        compiler_params=_params("parallel", "parallel"),
        name="dilated_attention",
    )(q, k, v)


def _out_ffn_kernel(h_ref, ya_ref, yb_ref, yc_ref, wraw_ref, g1_ref, b1_ref, win_ref, wout_ref, g2_ref, b2_ref,
                    *rest, n_cast, alpha, cw, fw, d_ff, chunk):
    srcs, (o_ref, *dsts), (w_ref, mid_ref, acc_ref) = rest[:n_cast], rest[n_cast:-3], rest[-3:]
    for src, dst in zip(srcs, dsts):
        dst[...] = src[...].astype(BF16)

    @pl.when(pl.program_id(0) == 0)
    def _():
        w_ref[...] = wraw_ref[...].astype(BF16)

    y = jnp.dot(ya_ref[...], w_ref[0:cw, :], preferred_element_type=F32)
    y = y + jnp.dot(yb_ref[...], w_ref[cw:cw + fw, :], preferred_element_type=F32)
    y = y + jnp.dot(yc_ref[...], w_ref[cw + fw:, :], preferred_element_type=F32)
    mid_ref[...] = _layer_norm(alpha * h_ref[...] + y, g1_ref[...], b1_ref[...])
    _ffn_kernel(mid_ref, win_ref, wout_ref, g2_ref, b2_ref, o_ref, acc_ref, alpha=alpha, d_ff=d_ff, chunk=chunk)


def _out_proj_ffn_ln(h, ya, yb, yc, w, w_in, w_out, g, b, l, next_f32, nxt, *, alpha, tm=512, chunk=256):
    t, d = h.shape
    d_ff = w_out.shape[0]
    row = lambda i: (i, 0)
    srcs, cast_in, cast_out, cast_shapes = _next_weight_cast(next_f32, nxt, t // tm)
    outs = pl.pallas_call(
        functools.partial(_out_ffn_kernel, n_cast=len(srcs), alpha=alpha, cw=ya.shape[1], fw=yb.shape[1],
                          d_ff=d_ff, chunk=chunk),
        grid=(t // tm,),
        in_specs=[pl.BlockSpec((tm, d), row), pl.BlockSpec((tm, ya.shape[1]), row),
                  pl.BlockSpec((tm, yb.shape[1]), row), pl.BlockSpec((tm, yc.shape[1]), row),
                  _resident(w, l), _resident(g, l, 1), _resident(b, l, 1),
                  _resident(w_in), _resident(w_out), _resident(g, l, 2), _resident(b, l, 2)] + cast_in,
        out_specs=[pl.BlockSpec((tm, d), row)] + cast_out,
        out_shape=[jax.ShapeDtypeStruct((t, d), F32)] + cast_shapes,
        scratch_shapes=[pltpu.VMEM((d, d), BF16), pltpu.VMEM((tm, d), F32), pltpu.VMEM((tm, d), F32)],
        compiler_params=_params("arbitrary"),
        name="out_proj_ffn_ln",
    )(h, ya, yb, yc, w, g, b, w_in, w_out, g, b, *srcs)
    return outs[0], outs[1:]


def _rope_tables(seq):
    lane = jnp.arange(LANES)
    inv = 1.0 / (ROPE_THETA ** ((2 * (lane % (HEAD_DIM // 2))).astype(F32) / HEAD_DIM))
    ang = jnp.arange(seq, dtype=F32)[:, None] * inv[None, :]
    cos, sin = jnp.cos(ang), jnp.sin(ang)
    first_half = (lane % HEAD_DIM < HEAD_DIM // 2)[None, :]
    sin_lo = jnp.where(first_half, -sin, 0.0)
    sin_hi = jnp.where(first_half, 0.0, sin)
    return cos, sin_lo, sin_hi


def kernel(x, w_in, w_o, forget_bias, conv_w, conv_b, conv_ln_g, conv_ln_b, ffn_w_in, ffn_w_out, ln_g, ln_b):
    bsz, seq, d = x.shape
    depth = w_in.shape[0]
    cw = conv_w.shape[-1]
    heads_f = forget_bias.shape[-1]
    fw = heads_f * HEAD_DIM
    dw = d - cw - fw
    alpha = (2 * depth) ** 0.25
    t = bsz * seq
    cos, slo, shi = _rope_tables(seq)

    ffn_f32 = (ffn_w_in, ffn_w_out)
    ffn_bf = (ffn_w_in[0, 0].astype(BF16), ffn_w_out[0, 0].astype(BF16))
    fb = jnp.repeat(forget_bias, HEAD_DIM, axis=1)[:, None, :]
    row3 = lambda a: a[:, None, :]
    cb, cg, cbeta = row3(conv_b), row3(conv_ln_g), row3(conv_ln_b)
    g4, b4 = ln_g[:, :, None, :], ln_b[:, :, None, :]
    b3 = lambda a: a.reshape(bsz, seq, a.shape[-1])

    h = x.reshape(t, d)
    for l in range(depth):
        h, ffn_bf = _ffn_ln(h, *ffn_bf, g4, b4, l, 0, ffn_f32, (l, 1), alpha=alpha)
        ya, qa, ka, va, dq, dk, dv = _mixer_proj(h, w_in, fb, cos, slo, shi, conv_w, cb, cg, cbeta, l,
                                                 bsz=bsz, fw=fw, dw=dw)
        yb = _fox_attention(qa, ka, va)
        yc = _dilated_attention(b3(dq), b3(dk), b3(dv))
        h, ffn_bf = _out_proj_ffn_ln(h, ya, yb.reshape(t, fw), yc.reshape(t, dw), w_o, *ffn_bf, g4, b4, l,
                                     ffn_f32, (l + 1, 0) if l + 1 < depth else None, alpha=alpha)
    return h.reshape(bsz, seq, d)
```

```python
import functools

import jax
import jax.numpy as jnp
from jax import lax
from jax.experimental import pallas as pl
from jax.experimental.pallas import tpu as pltpu

HEAD_DIM = 64
CONV_K = 31
DIL_CONFIGS = ((128, 1), (512, 4), (2048, 16))
ROPE_THETA = 10000.0
LN_EPS = 1e-5
LANES = 128
HEADS_PER_SLAB = LANES // HEAD_DIM
NEG_BIG = -1e30
LOG2_E = 1.4426950408889634
SCORE_SCALE = HEAD_DIM ** -0.5 * LOG2_E
VMEM_LIMIT = 56 * 1024 * 1024

F32 = jnp.float32
BF16 = jnp.bfloat16


def _layer_norm(y, g, b):
    mu = jnp.mean(y, axis=-1, keepdims=True)
    yc = y - mu
    var = jnp.mean(yc * yc, axis=-1, keepdims=True)
    return yc * lax.rsqrt(var + LN_EPS) * g + b


def _params(*sem):
    return pltpu.CompilerParams(dimension_semantics=sem, vmem_limit_bytes=VMEM_LIMIT)


def _resident(arr, *lead):
    rest = arr.shape[len(lead):]
    index = tuple(lead) + (0,) * len(rest)
    return pl.BlockSpec((None,) * len(lead) + rest, lambda *_: index, pipeline_mode=pl.Buffered(1))


def _ffn_kernel(x_ref, win_ref, wout_ref, g_ref, b_ref, o_ref, acc_ref, *, alpha, d_ff, chunk):
    x = x_ref[...]
    xb = x.astype(BF16)
    for c in range(d_ff // chunk):
        lo = c * chunk
        gate = jnp.dot(xb, win_ref[:, lo:lo + chunk], preferred_element_type=F32)
        up = jnp.dot(xb, win_ref[:, d_ff + lo:d_ff + lo + chunk], preferred_element_type=F32)
        hid = (gate * jax.nn.sigmoid(gate) * up).astype(BF16)
        part = jnp.dot(hid, wout_ref[lo:lo + chunk, :], preferred_element_type=F32)
        if c == 0:
            acc_ref[...] = part
        else:
            acc_ref[...] += part
    y = alpha * x + 0.5 * acc_ref[...]
    o_ref[...] = _layer_norm(y, g_ref[...], b_ref[...])


def _next_weight_cast(weights, nxt, steps):
    if nxt is None:
        return [], [], [], []
    l, j = nxt
    in_specs, out_specs, out_shapes = [], [], []
    for w in weights:
        rows, cols = w.shape[-2:]
        n = steps
        while rows % n or (rows // n) % 16:
            n //= 2
        per = steps // n
        in_specs.append(pl.BlockSpec((None, None, rows // n, cols), lambda i, per=per: (l, j, i // per, 0)))
        out_specs.append(pl.BlockSpec((rows // n, cols), lambda i, per=per: (i // per, 0)))
        out_shapes.append(jax.ShapeDtypeStruct((rows, cols), BF16))
    return list(weights), in_specs, out_specs, out_shapes


def _ffn_ln_kernel(x_ref, win_ref, wout_ref, g_ref, b_ref, *rest, n_cast, **kw):
    srcs, (o_ref, *dsts), acc_ref = rest[:n_cast], rest[n_cast:-1], rest[-1]
    for src, dst in zip(srcs, dsts):
        dst[...] = src[...].astype(BF16)
    _ffn_kernel(x_ref, win_ref, wout_ref, g_ref, b_ref, o_ref, acc_ref, **kw)


def _ffn_ln(x, w_in, w_out, g, b, l, n, next_f32, nxt, *, alpha, tm=1024, chunk=256):
    t, d = x.shape
    d_ff = w_out.shape[0]
    srcs, cast_in, cast_out, cast_shapes = _next_weight_cast(next_f32, nxt, t // tm)
    outs = pl.pallas_call(
        functools.partial(_ffn_ln_kernel, n_cast=len(srcs), alpha=alpha, d_ff=d_ff, chunk=chunk),
        grid=(t // tm,),
        in_specs=[pl.BlockSpec((tm, d), lambda i: (i, 0)), _resident(w_in), _resident(w_out),
                  _resident(g, l, n), _resident(b, l, n)] + cast_in,
        out_specs=[pl.BlockSpec((tm, d), lambda i: (i, 0))] + cast_out,
        out_shape=[jax.ShapeDtypeStruct((t, d), F32)] + cast_shapes,
        scratch_shapes=[pltpu.VMEM((tm, d), F32)],
        compiler_params=_params("arbitrary"),
        name="ffn_ln",
    )(x, w_in, w_out, g, b, *srcs)
    return outs[0], outs[1:]


def _rope_slab(t, cos, sin_lo, sin_hi):
    nxt = pltpu.roll(t, LANES - HEAD_DIM // 2, axis=1)
    prv = pltpu.roll(t, HEAD_DIM // 2, axis=1)
    return t * cos + nxt * sin_lo + prv * sin_hi


def _split3(x):
    hi = x.astype(BF16).astype(F32)
    mid = (x - hi).astype(BF16).astype(F32)
    return hi, mid, x - hi - mid


def _store_fox_operands(cum, q, k, v, qa_ref, ka_ref, va_ref, rows):
    lane = lax.broadcasted_iota(jnp.int32, (1, LANES), 1)
    is_head = lane < HEAD_DIM
    aug = lane - HEAD_DIM
    q_ones = jnp.where(jnp.logical_and(aug >= 3, aug < 6), 1.0, 0.0)
    k_ones = jnp.where(jnp.logical_and(aug >= 0, aug < 3), 1.0, 0.0)
    k_bias = jnp.logical_and(aug >= 3, aug < 6)
    for p in range(q.shape[-1] // LANES):
        slab = slice(p * LANES, (p + 1) * LANES)
        split = _split3(cum[:, slab])
        for a in range(HEADS_PER_SLAB):
            h = HEADS_PER_SLAB * p + a
            odd = a == 1
            head_lanes = lambda x: pltpu.roll(x[:, slab], HEAD_DIM, axis=1) if odd else x[:, slab]
            hi, mid, lo = split if odd else [pltpu.roll(t, HEAD_DIM, axis=1) for t in split]
            q_aug = jnp.where(aug == 0, hi, jnp.where(aug == 1, mid, jnp.where(aug == 2, lo, q_ones)))
            k_aug = jnp.where(k_bias, -pltpu.roll(q_aug, 3, axis=1), k_ones)
            qa_ref[h, rows, :] = jnp.where(is_head, head_lanes(q), q_aug).astype(BF16)
            ka_ref[h, rows, :] = jnp.where(is_head, head_lanes(k), k_aug).astype(BF16)
            va_ref[h, rows, :] = jnp.where(is_head, head_lanes(v), 1.0).astype(BF16)


def _stage_mixer_weights(wraw_ref, w_ref, *, split, heads, rows):
    lane = lax.broadcasted_iota(jnp.int32, (1, LANES), 1)
    tail = wraw_ref.shape[1] - split - heads
    for r0 in range(0, wraw_ref.shape[0], rows):
        rs = slice(r0, r0 + rows)
        w_ref[rs, 0:split] = wraw_ref[rs, 0:split].astype(BF16)
        gates = wraw_ref[rs, split:split + heads]
        for p in range(heads // HEADS_PER_SLAB):
            even = jnp.broadcast_to(gates[:, HEADS_PER_SLAB * p:HEADS_PER_SLAB * p + 1], (rows, LANES))
            odd = jnp.broadcast_to(gates[:, HEADS_PER_SLAB * p + 1:HEADS_PER_SLAB * p + 2], (rows, LANES))
            w_ref[rs, split + p * LANES:split + (p + 1) * LANES] = (
                jnp.where(lane < HEAD_DIM, even, odd).astype(BF16))
        lo = split + heads * HEAD_DIM
        w_ref[rs, lo:lo + tail] = wraw_ref[rs, split + heads:split + heads + tail].astype(BF16)


def _conv_rows(pad_ref, t0, rows, halo, w_ref, cb_ref, g_ref, b_ref, o_ref):
    ch = pad_ref.shape[1]
    first = t0 + halo - (CONV_K - 1)
    slabs = [slice(c0, c0 + LANES) for c0 in range(0, ch, LANES)]
    accs = []
    sub = 8
    base = first // sub * sub
    span = rows + (first - base + CONV_K - 1 + sub - 1) // sub * sub
    for cs in slabs:
        win = pad_ref[base:base + span, cs]
        parts = []
        for b in range(sub):
            rolled = win if b == 0 else pltpu.roll(win, span - b, axis=0)
            offs = [o for o in range(b, span - rows + 1, sub) if 0 <= base + o - first < CONV_K]
            parts.append(functools.reduce(jnp.add, [
                w_ref[base + o - first:base + o - first + 1, cs] * rolled[o - b:o - b + rows, :]
                for o in offs]))
        while len(parts) > 1:
            parts = [parts[i] + parts[i + 1] for i in range(0, len(parts), 2)]
        accs.append(parts[0] + cb_ref[:, cs])
    mu = sum(jnp.sum(a, axis=-1, keepdims=True) for a in accs) / ch
    cen = [a - mu for a in accs]
    var = sum(jnp.sum(c * c, axis=-1, keepdims=True) for c in cen) / ch
    inv = lax.rsqrt(var + LN_EPS)
    for cs, c in zip(slabs, cen):
        y = c * inv * g_ref[:, cs] + b_ref[:, cs]
        o_ref[t0:t0 + rows, cs] = (y * jax.nn.sigmoid(y)).astype(o_ref.dtype)


def _proj_kernel(h_ref, wraw_ref, fb_ref, cos_ref, slo_ref, shi_ref, cw_ref, cb_ref, cg_ref, cbeta_ref,
                 ya_ref, qa_ref, ka_ref, va_ref, dq_ref, dk_ref, dv_ref, w_ref, carry_ref, pad_ref,
                 *, cw, fw, dw, nsb, blk, halo, conv_rows):
    tm = h_ref.shape[0]

    @pl.when(pl.program_id(0) == 0)
    def _():
        _stage_mixer_weights(wraw_ref, w_ref, split=2 * cw + 3 * fw, heads=fw // HEAD_DIM, rows=blk)

    @pl.when(pl.program_id(0) % nsb == 0)
    def _():
        carry_ref[...] = jnp.zeros_like(carry_ref)
        pad_ref[0:halo, :] = jnp.zeros((halo, cw), F32)

    @pl.when(pl.program_id(0) % nsb != 0)
    def _():
        pad_ref[0:halo, :] = pad_ref[tm:tm + halo, :]

    hb = h_ref[...].astype(BF16)

    def cols(lo, width):
        return jnp.dot(hb, w_ref[:, lo:lo + width], preferred_element_type=F32)

    scale = SCORE_SCALE
    val = cols(0, cw)
    gate = cols(cw, cw)
    pad_ref[halo:halo + tm, :] = val * jax.nn.sigmoid(gate)
    for t0 in range(0, tm, conv_rows):
        _conv_rows(pad_ref, t0, conv_rows, halo, cw_ref, cb_ref, cg_ref, cbeta_ref, ya_ref)
    base = 2 * cw
    fq = cols(base, fw) * scale
    fk = cols(base + fw, fw)
    fv = cols(base + 2 * fw, fw)
    log_f = jax.nn.log_sigmoid(cols(base + 3 * fw, fw) + fb_ref[...])

    r = lax.broadcasted_iota(jnp.int32, (blk, blk), 0)
    c = lax.broadcasted_iota(jnp.int32, (blk, blk), 1)
    tri = (c <= r).astype(BF16)
    cum = carry_ref[...]
    for r0 in range(0, hb.shape[0], blk):
        rows = slice(r0, r0 + blk)
        cum = cum[-1:, :]
        for part in _split3(log_f[rows]):
            cum = cum + jnp.dot(tri, part.astype(BF16), preferred_element_type=F32)
        _store_fox_operands(cum * LOG2_E, fq[rows], fk[rows], fv[rows], qa_ref, ka_ref, va_ref, rows)
    carry_ref[...] = cum[-1:, :]

    base = base + 4 * fw
    cos, slo, shi = cos_ref[...], slo_ref[...], shi_ref[...]
    q, k = cols(base, dw), cols(base + dw, dw)
    for j in range(dw // LANES):
        sl = slice(j * LANES, (j + 1) * LANES)
        dq_ref[:, sl] = _rope_slab(q[:, sl], cos, slo, shi) * scale
        dk_ref[:, sl] = _rope_slab(k[:, sl], cos, slo, shi)
    dv_ref[...] = cols(base + 2 * dw, dw)


def _mixer_proj(h, w, fb, cos, slo, shi, conv_w, conv_b, conv_g, conv_beta, l, *, bsz, fw, dw,
                tm=512, blk=128, halo=32, conv_rows=64):
    t, d = h.shape
    cw = conv_w.shape[-1]
    seq = t // bsz
    nsb = seq // tm
    heads = fw // HEAD_DIM
    row = lambda i: (i, 0)
    pos = lambda i: (i % nsb, 0)
    tok = lambda n, dt: (pl.BlockSpec((tm, n), row), jax.ShapeDtypeStruct((t, n), dt))
    fox = (pl.BlockSpec((None, heads, tm, LANES), lambda i: (i // nsb, 0, i % nsb, 0)),
           jax.ShapeDtypeStruct((bsz, heads, seq, LANES), BF16))
    outs = [tok(cw, BF16), fox, fox, fox, tok(dw, F32), tok(dw, F32), tok(dw, F32)]
    return pl.pallas_call(
        functools.partial(_proj_kernel, cw=cw, fw=fw, dw=dw, nsb=nsb, blk=blk, halo=halo, conv_rows=conv_rows),
        grid=(t // tm,),
        in_specs=[pl.BlockSpec((tm, d), row), _resident(w, l), _resident(fb, l),
                  pl.BlockSpec((tm, LANES), pos), pl.BlockSpec((tm, LANES), pos),
                  pl.BlockSpec((tm, LANES), pos),
                  _resident(conv_w, l), _resident(conv_b, l), _resident(conv_g, l), _resident(conv_beta, l)],
        out_specs=[spec for spec, _ in outs],
        out_shape=[shape for _, shape in outs],
        scratch_shapes=[pltpu.VMEM((d, 2 * cw + 4 * fw + 3 * dw), BF16), pltpu.VMEM((1, fw), F32),
                        pltpu.VMEM((halo + tm, cw), F32)],
        compiler_params=_params("arbitrary"),
        name="mixer_proj",
    )(h, w, fb, cos, slo, shi, conv_w, conv_b, conv_g, conv_beta)


def _fox_kernel(q_ref, k_ref, v_ref, o_ref, m_ref, acc_ref, *, tq, nq):
    i = pl.program_id(2)
    half = tq // 2
    dn = (((1,), (1,)), ((), ()))

    def update(a, rows, k0, width, mask):
        keys = slice(k0, k0 + width)
        s = lax.dot_general(q_ref[a, rows, :], k_ref[a, keys, :], dn, preferred_element_type=F32)
        if mask is not None:
            s = jnp.where(mask, s, NEG_BIG)
        m_old = m_ref[a, rows, :]
        m_new = jnp.maximum(m_old, jnp.max(s, axis=-1, keepdims=True))
        alpha = jnp.exp2(m_old - m_new)
        p = jnp.exp2(s - pltpu.repeat(m_new, width // LANES, axis=1))
        m_ref[a, rows, :] = m_new
        acc_ref[a, rows, :] = alpha * acc_ref[a, rows, :] + jnp.dot(
            p.astype(BF16), v_ref[a, keys, :], preferred_element_type=F32)

    lower = (lax.broadcasted_iota(jnp.int32, (half, half), 1)
             <= lax.broadcasted_iota(jnp.int32, (half, half), 0))
    upper = (lax.broadcasted_iota(jnp.int32, (half, tq), 1)
             <= lax.broadcasted_iota(jnp.int32, (half, tq), 0) + half)
    lane = lax.broadcasted_iota(jnp.int32, (1, LANES), 1)

    def q_block(n_before):
        m_ref[...] = jnp.full(m_ref.shape, NEG_BIG, F32)
        acc_ref[...] = jnp.zeros(acc_ref.shape, F32)
        for a in range(HEADS_PER_SLAB):
            update(a, slice(0, half), n_before * tq, half, lower)
            update(a, slice(half, tq), n_before * tq, tq, upper)
        for j in range(n_before):
            for a in range(HEADS_PER_SLAB):
                update(a, slice(0, tq), j * tq, tq, None)
        acc0, acc1 = acc_ref[0], acc_ref[1]
        out = jnp.where(lane < HEAD_DIM,
                        acc0 / pltpu.roll(acc0, HEAD_DIM, axis=1),
                        pltpu.roll(acc1, HEAD_DIM, axis=1) / acc1)
        o_ref[...] = out.astype(o_ref.dtype)

    for n in range(nq):
        pl.when(i == n)(functools.partial(q_block, n))


def _fox_attention(qa, ka, va, *, tq=1024):
    bsz, heads, seq, _ = qa.shape
    qspec = pl.BlockSpec((None, HEADS_PER_SLAB, tq, LANES), lambda b, p, i: (b, p, i, 0))
    kvspec = pl.BlockSpec((None, HEADS_PER_SLAB, seq, LANES), lambda b, p, i: (b, p, 0, 0))
    return pl.pallas_call(
        functools.partial(_fox_kernel, tq=tq, nq=seq // tq),
        grid=(bsz, heads // HEADS_PER_SLAB, seq // tq),
        in_specs=[qspec, kvspec, kvspec],
        out_specs=pl.BlockSpec((None, tq, LANES), lambda b, p, i: (b, i, p)),
        out_shape=jax.ShapeDtypeStruct((bsz, seq, heads * HEAD_DIM), BF16),
        scratch_shapes=[pltpu.VMEM((HEADS_PER_SLAB, tq, LANES), F32),
                        pltpu.VMEM((HEADS_PER_SLAB, tq, LANES), F32)],
        compiler_params=_params("parallel", "parallel", "arbitrary"),
        name="fox_attention",
    )(qa, ka, va)


def _dil_kernel(q_ref, k_ref, v_ref, o_ref, bias_ref, bias2_ref, m_ref, l_ref, acc_ref, *,
                blk, unroll, rows_out):
    seq = q_ref.shape[0]
    win = 2 * blk
    lane = lax.broadcasted_iota(jnp.int32, (1, LANES), 1)
    head0 = lane < HEAD_DIM
    dist0 = (lax.broadcasted_iota(jnp.int32, (win, win), 0) % blk
             - lax.broadcasted_iota(jnp.int32, (win, win), 1))
    row = lax.broadcasted_iota(jnp.int32, (2 * win, win), 0)
    col = lax.broadcasted_iota(jnp.int32, (2 * win, win), 1)
    bias_ref[0] = jnp.where(dist0 >= 0, 0.0, NEG_BIG)
    bias_ref[1] = jnp.where(jnp.logical_and(dist0 + blk >= 0, dist0 <= 0), 0.0, NEG_BIG)
    dist_all = (row % win) - col
    bias2_ref[...] = jnp.where(jnp.logical_and(dist_all >= 0, dist_all <= blk), 0.0, NEG_BIG)
    ones = jnp.ones((win, LANES), BF16)
    dn = (((1,), (1,)), ((), ()))

    def attend(bi, rows, wrows, bias):
        nq = rows.size
        qb = q_ref[rows, :].astype(BF16)
        zero = jnp.zeros_like(qb)
        qs = jnp.concatenate([jnp.where(head0, qb, zero), jnp.where(head0, zero, qb)], axis=0)
        kw = k_ref[wrows, :].astype(BF16)
        vw = jnp.concatenate([v_ref[wrows, :].astype(BF16), ones], axis=1)
        s = lax.dot_general(qs, kw, dn, preferred_element_type=F32) + bias
        m = jnp.max(s, axis=-1, keepdims=True)
        p = jnp.exp2(s - m).astype(BF16)
        pv = jnp.dot(p, vw, preferred_element_type=F32)
        mb = jnp.broadcast_to(m, (2 * nq, LANES))
        m_ref[bi, rows, :] = jnp.where(head0, mb[:nq], mb[nq:])
        acc_ref[bi, rows, :] = jnp.where(head0, pv[:nq, :LANES], pv[nq:, :LANES])
        l_ref[bi, rows, :] = jnp.where(head0, pv[:nq, LANES:], pv[nq:, LANES:])

    for bi, (_, dil) in enumerate(DIL_CONFIGS):
        nb = seq // (dil * blk)
        ds = (lambda start, size, dil=dil: pl.ds(start, size, stride=dil) if dil > 1 else pl.ds(start, size))

        if nb == 2:
            def body(r, _, bi=bi, ds=ds):
                attend(bi, ds(r, win), ds(r, win), bias2_ref[...])
                return 0

            lax.fori_loop(0, dil, body, 0, unroll=unroll // 2)
            continue

        def body(idx, _, bi=bi, dil=dil, nb=nb, ds=ds):
            r = idx // nb
            n = idx % nb
            start = r + n * (dil * blk)
            has_prev = (n > 0).astype(jnp.int32)
            attend(bi, ds(start, blk), ds(start - has_prev * (dil * blk), win), bias_ref[has_prev])
            return 0

        lax.fori_loop(0, seq // blk, body, 0, unroll=unroll)

    def merge(i, _):
        rows = pl.ds(pl.multiple_of(i * rows_out, rows_out), rows_out)
        ms = [m_ref[bi, rows, :] for bi in range(len(DIL_CONFIGS))]
        m_all = functools.reduce(jnp.maximum, ms)
        ws = [jnp.exp2(m - m_all) for m in ms]
        num = sum(w * acc_ref[bi, rows, :] for bi, w in enumerate(ws))
        den = sum(w * l_ref[bi, rows, :] for bi, w in enumerate(ws))
        o_ref[rows, :] = (num / den).astype(o_ref.dtype)
        return 0

    lax.fori_loop(0, seq // rows_out, merge, 0)


def _dilated_attention(q, k, v, *, blk=128, unroll=32, rows_out=256):
    bsz, seq, width = q.shape
    nbr = len(DIL_CONFIGS)
    spec = pl.BlockSpec((None, seq, LANES), lambda b, p: (b, 0, p))
    return pl.pallas_call(
        functools.partial(_dil_kernel, blk=blk, unroll=unroll, rows_out=rows_out),
        grid=(bsz, width // LANES),
        in_specs=[spec, spec, spec],
        out_specs=spec,
        out_shape=jax.ShapeDtypeStruct((bsz, seq, width), BF16),
        scratch_shapes=[pltpu.VMEM((2, 2 * blk, 2 * blk), F32), pltpu.VMEM((4 * blk, 2 * blk), F32)]
        + [pltpu.VMEM((nbr, seq, LANES), F32)] * 3,
        compiler_params=_params("parallel", "parallel"),
        name="dilated_attention",
    )(q, k, v)


def _out_ffn_kernel(h_ref, ya_ref, yb_ref, yc_ref, wraw_ref, g1_ref, b1_ref, win_ref, wout_ref, g2_ref, b2_ref,
                    *rest, n_cast, alpha, cw, fw, d_ff, chunk):
    srcs, (o_ref, *dsts), (w_ref, mid_ref, acc_ref) = rest[:n_cast], rest[n_cast:-3], rest[-3:]
    for src, dst in zip(srcs, dsts):
        dst[...] = src[...].astype(BF16)

    @pl.when(pl.program_id(0) == 0)
    def _():
        w_ref[...] = wraw_ref[...].astype(BF16)

    y = jnp.dot(ya_ref[...], w_ref[0:cw, :], preferred_element_type=F32)
    y = y + jnp.dot(yb_ref[...], w_ref[cw:cw + fw, :], preferred_element_type=F32)
    y = y + jnp.dot(yc_ref[...], w_ref[cw + fw:, :], preferred_element_type=F32)
    mid_ref[...] = _layer_norm(alpha * h_ref[...] + y, g1_ref[...], b1_ref[...])
    _ffn_kernel(mid_ref, win_ref, wout_ref, g2_ref, b2_ref, o_ref, acc_ref, alpha=alpha, d_ff=d_ff, chunk=chunk)


def _out_proj_ffn_ln(h, ya, yb, yc, w, w_in, w_out, g, b, l, next_f32, nxt, *, alpha, tm=512, chunk=256):
    t, d = h.shape
    d_ff = w_out.shape[0]
    row = lambda i: (i, 0)
    srcs, cast_in, cast_out, cast_shapes = _next_weight_cast(next_f32, nxt, t // tm)
    outs = pl.pallas_call(
        functools.partial(_out_ffn_kernel, n_cast=len(srcs), alpha=alpha, cw=ya.shape[1], fw=yb.shape[1],
                          d_ff=d_ff, chunk=chunk),
        grid=(t // tm,),
        in_specs=[pl.BlockSpec((tm, d), row), pl.BlockSpec((tm, ya.shape[1]), row),
                  pl.BlockSpec((tm, yb.shape[1]), row), pl.BlockSpec((tm, yc.shape[1]), row),
                  _resident(w, l), _resident(g, l, 1), _resident(b, l, 1),
                  _resident(w_in), _resident(w_out), _resident(g, l, 2), _resident(b, l, 2)] + cast_in,
        out_specs=[pl.BlockSpec((tm, d), row)] + cast_out,
        out_shape=[jax.ShapeDtypeStruct((t, d), F32)] + cast_shapes,
        scratch_shapes=[pltpu.VMEM((d, d), BF16), pltpu.VMEM((tm, d), F32), pltpu.VMEM((tm, d), F32)],
        compiler_params=_params("arbitrary"),
        name="out_proj_ffn_ln",
    )(h, ya, yb, yc, w, g, b, w_in, w_out, g, b, *srcs)
    return outs[0], outs[1:]


def _rope_tables(seq):
    lane = jnp.arange(LANES)
    inv = 1.0 / (ROPE_THETA ** ((2 * (lane % (HEAD_DIM // 2))).astype(F32) / HEAD_DIM))
    ang = jnp.arange(seq, dtype=F32)[:, None] * inv[None, :]
    cos, sin = jnp.cos(ang), jnp.sin(ang)
    first_half = (lane % HEAD_DIM < HEAD_DIM // 2)[None, :]
    sin_lo = jnp.where(first_half, -sin, 0.0)
    sin_hi = jnp.where(first_half, 0.0, sin)
    return cos, sin_lo, sin_hi


def kernel(x, w_in, w_o, forget_bias, conv_w, conv_b, conv_ln_g, conv_ln_b, ffn_w_in, ffn_w_out, ln_g, ln_b):
    bsz, seq, d = x.shape
    depth = w_in.shape[0]
    cw = conv_w.shape[-1]
    heads_f = forget_bias.shape[-1]
    fw = heads_f * HEAD_DIM
    dw = d - cw - fw
    alpha = (2 * depth) ** 0.25
    t = bsz * seq
    cos, slo, shi = _rope_tables(seq)

    ffn_f32 = (ffn_w_in, ffn_w_out)
    ffn_bf = (ffn_w_in[0, 0].astype(BF16), ffn_w_out[0, 0].astype(BF16))
    fb = jnp.repeat(forget_bias, HEAD_DIM, axis=1)[:, None, :]
    row3 = lambda a: a[:, None, :]
    cb, cg, cbeta = row3(conv_b), row3(conv_ln_g), row3(conv_ln_b)
    g4, b4 = ln_g[:, :, None, :], ln_b[:, :, None, :]
    b3 = lambda a: a.reshape(bsz, seq, a.shape[-1])

    h = x.reshape(t, d)
    for l in range(depth):
        h, ffn_bf = _ffn_ln(h, *ffn_bf, g4, b4, l, 0, ffn_f32, (l, 1), alpha=alpha)
        ya, qa, ka, va, dq, dk, dv = _mixer_proj(h, w_in, fb, cos, slo, shi, conv_w, cb, cg, cbeta, l,
                                                 bsz=bsz, fw=fw, dw=dw)
        yb = _fox_attention(qa, ka, va)
        yc = _dilated_attention(b3(dq), b3(dk), b3(dv))
        h, ffn_bf = _out_proj_ffn_ln(h, ya, yb.reshape(t, fw), yc.reshape(t, dw), w_o, *ffn_bf, g4, b4, l,
                                     ffn_f32, (l + 1, 0) if l + 1 < depth else None, alpha=alpha)
    return h.reshape(bsz, seq, d)
```

```python
import functools

import jax
import jax.numpy as jnp
from jax import lax
from jax.experimental import pallas as pl
from jax.experimental.pallas import tpu as pltpu

HEAD_DIM = 64
CONV_K = 31
DIL_CONFIGS = ((128, 1), (512, 4), (2048, 16))
ROPE_THETA = 10000.0
LN_EPS = 1e-5
LANES = 128
HEADS_PER_SLAB = LANES // HEAD_DIM
NEG_BIG = -1e30
LOG2_E = 1.4426950408889634
SCORE_SCALE = HEAD_DIM ** -0.5 * LOG2_E
VMEM_LIMIT = 56 * 1024 * 1024

F32 = jnp.float32
BF16 = jnp.bfloat16


def _layer_norm(y, g, b):
    mu = jnp.mean(y, axis=-1, keepdims=True)
    yc = y - mu
    var = jnp.mean(yc * yc, axis=-1, keepdims=True)
    return yc * lax.rsqrt(var + LN_EPS) * g + b


def _params(*sem):
    return pltpu.CompilerParams(dimension_semantics=sem, vmem_limit_bytes=VMEM_LIMIT)


def _resident(arr, *lead):
    rest = arr.shape[len(lead):]
    index = tuple(lead) + (0,) * len(rest)
    return pl.BlockSpec((None,) * len(lead) + rest, lambda *_: index, pipeline_mode=pl.Buffered(1))


def _ffn_kernel(x_ref, win_ref, wout_ref, g_ref, b_ref, o_ref, acc_ref, *, alpha, d_ff, chunk):
    x = x_ref[...]
    xb = x.astype(BF16)
    for c in range(d_ff // chunk):
        lo = c * chunk
        gate = jnp.dot(xb, win_ref[:, lo:lo + chunk], preferred_element_type=F32)
        up = jnp.dot(xb, win_ref[:, d_ff + lo:d_ff + lo + chunk], preferred_element_type=F32)
        hid = (gate * jax.nn.sigmoid(gate) * up).astype(BF16)
        part = jnp.dot(hid, wout_ref[lo:lo + chunk, :], preferred_element_type=F32)
        if c == 0:
            acc_ref[...] = part
        else:
            acc_ref[...] += part
    y = alpha * x + 0.5 * acc_ref[...]
    o_ref[...] = _layer_norm(y, g_ref[...], b_ref[...])


def _next_weight_cast(weights, nxt, steps):
    if nxt is None:
        return [], [], [], []
    l, j = nxt
    in_specs, out_specs, out_shapes = [], [], []
    for w in weights:
        rows, cols = w.shape[-2:]
        n = steps
        while rows % n or (rows // n) % 16:
            n //= 2
        per = steps // n
        in_specs.append(pl.BlockSpec((None, None, rows // n, cols), lambda i, per=per: (l, j, i // per, 0)))
        out_specs.append(pl.BlockSpec((rows // n, cols), lambda i, per=per: (i // per, 0)))
        out_shapes.append(jax.ShapeDtypeStruct((rows, cols), BF16))
    return list(weights), in_specs, out_specs, out_shapes


def _ffn_ln_kernel(x_ref, win_ref, wout_ref, g_ref, b_ref, *rest, n_cast, **kw):
    srcs, (o_ref, *dsts), acc_ref = rest[:n_cast], rest[n_cast:-1], rest[-1]
    for src, dst in zip(srcs, dsts):
        dst[...] = src[...].astype(BF16)
    _ffn_kernel(x_ref, win_ref, wout_ref, g_ref, b_ref, o_ref, acc_ref, **kw)


def _ffn_ln(x, w_in, w_out, g, b, l, n, next_f32, nxt, *, alpha, tm=1024, chunk=256):
    t, d = x.shape
    d_ff = w_out.shape[0]
    srcs, cast_in, cast_out, cast_shapes = _next_weight_cast(next_f32, nxt, t // tm)
    outs = pl.pallas_call(
        functools.partial(_ffn_ln_kernel, n_cast=len(srcs), alpha=alpha, d_ff=d_ff, chunk=chunk),
        grid=(t // tm,),
        in_specs=[pl.BlockSpec((tm, d), lambda i: (i, 0)), _resident(w_in), _resident(w_out),
                  _resident(g, l, n), _resident(b, l, n)] + cast_in,
        out_specs=[pl.BlockSpec((tm, d), lambda i: (i, 0))] + cast_out,
        out_shape=[jax.ShapeDtypeStruct((t, d), F32)] + cast_shapes,
        scratch_shapes=[pltpu.VMEM((tm, d), F32)],
        compiler_params=_params("arbitrary"),
        name="ffn_ln",
    )(x, w_in, w_out, g, b, *srcs)
    return outs[0], outs[1:]


def _rope_slab(t, cos, sin_lo, sin_hi):
    nxt = pltpu.roll(t, LANES - HEAD_DIM // 2, axis=1)
    prv = pltpu.roll(t, HEAD_DIM // 2, axis=1)
    return t * cos + nxt * sin_lo + prv * sin_hi


def _split3(x):
    hi = x.astype(BF16).astype(F32)
    mid = (x - hi).astype(BF16).astype(F32)
    return hi, mid, x - hi - mid


def _store_fox_operands(cum, q, k, v, qa_ref, ka_ref, va_ref, rows):
    lane = lax.broadcasted_iota(jnp.int32, (1, LANES), 1)
    is_head = lane < HEAD_DIM
    aug = lane - HEAD_DIM
    q_ones = jnp.where(jnp.logical_and(aug >= 3, aug < 6), 1.0, 0.0)
    k_ones = jnp.where(jnp.logical_and(aug >= 0, aug < 3), 1.0, 0.0)
    k_bias = jnp.logical_and(aug >= 3, aug < 6)
    for p in range(q.shape[-1] // LANES):
        slab = slice(p * LANES, (p + 1) * LANES)
        split = _split3(cum[:, slab])
        for a in range(HEADS_PER_SLAB):
            h = HEADS_PER_SLAB * p + a
            odd = a == 1
            head_lanes = lambda x: pltpu.roll(x[:, slab], HEAD_DIM, axis=1) if odd else x[:, slab]
            hi, mid, lo = split if odd else [pltpu.roll(t, HEAD_DIM, axis=1) for t in split]
            q_aug = jnp.where(aug == 0, hi, jnp.where(aug == 1, mid, jnp.where(aug == 2, lo, q_ones)))
            k_aug = jnp.where(k_bias, -pltpu.roll(q_aug, 3, axis=1), k_ones)
            qa_ref[h, rows, :] = jnp.where(is_head, head_lanes(q), q_aug).astype(BF16)
            ka_ref[h, rows, :] = jnp.where(is_head, head_lanes(k), k_aug).astype(BF16)
            va_ref[h, rows, :] = jnp.where(is_head, head_lanes(v), 1.0).astype(BF16)


def _stage_mixer_weights(wraw_ref, w_ref, *, split, heads, rows):
    lane = lax.broadcasted_iota(jnp.int32, (1, LANES), 1)
    tail = wraw_ref.shape[1] - split - heads
    for r0 in range(0, wraw_ref.shape[0], rows):
        rs = slice(r0, r0 + rows)
        w_ref[rs, 0:split] = wraw_ref[rs, 0:split].astype(BF16)
        gates = wraw_ref[rs, split:split + heads]
        for p in range(heads // HEADS_PER_SLAB):
            even = jnp.broadcast_to(gates[:, HEADS_PER_SLAB * p:HEADS_PER_SLAB * p + 1], (rows, LANES))
            odd = jnp.broadcast_to(gates[:, HEADS_PER_SLAB * p + 1:HEADS_PER_SLAB * p + 2], (rows, LANES))
            w_ref[rs, split + p * LANES:split + (p + 1) * LANES] = (
                jnp.where(lane < HEAD_DIM, even, odd).astype(BF16))
        lo = split + heads * HEAD_DIM
        w_ref[rs, lo:lo + tail] = wraw_ref[rs, split + heads:split + heads + tail].astype(BF16)


def _conv_rows(pad_ref, t0, rows, halo, w_ref, cb_ref, g_ref, b_ref, o_ref):
    ch = pad_ref.shape[1]
    first = t0 + halo - (CONV_K - 1)
    slabs = [slice(c0, c0 + LANES) for c0 in range(0, ch, LANES)]
    accs = []
    sub = 8
    base = first // sub * sub
    span = rows + (first - base + CONV_K - 1 + sub - 1) // sub * sub
    for cs in slabs:
        win = pad_ref[base:base + span, cs]
        parts = []
        for b in range(sub):
            rolled = win if b == 0 else pltpu.roll(win, span - b, axis=0)
            offs = [o for o in range(b, span - rows + 1, sub) if 0 <= base + o - first < CONV_K]
            parts.append(functools.reduce(jnp.add, [
                w_ref[base + o - first:base + o - first + 1, cs] * rolled[o - b:o - b + rows, :]
                for o in offs]))
        while len(parts) > 1:
            parts = [parts[i] + parts[i + 1] for i in range(0, len(parts), 2)]
        accs.append(parts[0] + cb_ref[:, cs])
    mu = sum(jnp.sum(a, axis=-1, keepdims=True) for a in accs) / ch
    cen = [a - mu for a in accs]
    var = sum(jnp.sum(c * c, axis=-1, keepdims=True) for c in cen) / ch
    inv = lax.rsqrt(var + LN_EPS)
    for cs, c in zip(slabs, cen):
        y = c * inv * g_ref[:, cs] + b_ref[:, cs]
        o_ref[t0:t0 + rows, cs] = (y * jax.nn.sigmoid(y)).astype(o_ref.dtype)


def _proj_kernel(h_ref, wraw_ref, fb_ref, cos_ref, slo_ref, shi_ref, cw_ref, cb_ref, cg_ref, cbeta_ref,
                 ya_ref, qa_ref, ka_ref, va_ref, dq_ref, dk_ref, dv_ref, w_ref, carry_ref, pad_ref,
                 *, cw, fw, dw, nsb, blk, halo, conv_rows):
    tm = h_ref.shape[0]

    @pl.when(pl.program_id(0) == 0)
    def _():
        _stage_mixer_weights(wraw_ref, w_ref, split=2 * cw + 3 * fw, heads=fw // HEAD_DIM, rows=blk)

    @pl.when(pl.program_id(0) % nsb == 0)
    def _():
        carry_ref[...] = jnp.zeros_like(carry_ref)
        pad_ref[0:halo, :] = jnp.zeros((halo, cw), F32)

    @pl.when(pl.program_id(0) % nsb != 0)
    def _():
        pad_ref[0:halo, :] = pad_ref[tm:tm + halo, :]

    hb = h_ref[...].astype(BF16)

    def cols(lo, width):
        return jnp.dot(hb, w_ref[:, lo:lo + width], preferred_element_type=F32)

    scale = SCORE_SCALE
    val = cols(0, cw)
    gate = cols(cw, cw)
    pad_ref[halo:halo + tm, :] = val * jax.nn.sigmoid(gate)
    for t0 in range(0, tm, conv_rows):
        _conv_rows(pad_ref, t0, conv_rows, halo, cw_ref, cb_ref, cg_ref, cbeta_ref, ya_ref)
    base = 2 * cw
    fq = cols(base, fw) * scale
    fk = cols(base + fw, fw)
    fv = cols(base + 2 * fw, fw)
    log_f = jax.nn.log_sigmoid(cols(base + 3 * fw, fw) + fb_ref[...])

    r = lax.broadcasted_iota(jnp.int32, (blk, blk), 0)
    c = lax.broadcasted_iota(jnp.int32, (blk, blk), 1)
    tri = (c <= r).astype(BF16)
    cum = carry_ref[...]
    for r0 in range(0, hb.shape[0], blk):
        rows = slice(r0, r0 + blk)
        cum = cum[-1:, :]
        for part in _split3(log_f[rows]):
            cum = cum + jnp.dot(tri, part.astype(BF16), preferred_element_type=F32)
        _store_fox_operands(cum * LOG2_E, fq[rows], fk[rows], fv[rows], qa_ref, ka_ref, va_ref, rows)
    carry_ref[...] = cum[-1:, :]

    base = base + 4 * fw
    cos, slo, shi = cos_ref[...], slo_ref[...], shi_ref[...]
    q, k = cols(base, dw), cols(base + dw, dw)
    for j in range(dw // LANES):
        sl = slice(j * LANES, (j + 1) * LANES)
        dq_ref[:, sl] = _rope_slab(q[:, sl], cos, slo, shi) * scale
        dk_ref[:, sl] = _rope_slab(k[:, sl], cos, slo, shi)
    dv_ref[...] = cols(base + 2 * dw, dw)


def _mixer_proj(h, w, fb, cos, slo, shi, conv_w, conv_b, conv_g, conv_beta, l, *, bsz, fw, dw,
                tm=512, blk=128, halo=32, conv_rows=64):
    t, d = h.shape
    cw = conv_w.shape[-1]
    seq = t // bsz
    nsb = seq // tm
    heads = fw // HEAD_DIM
    row = lambda i: (i, 0)
    pos = lambda i: (i % nsb, 0)
    tok = lambda n, dt: (pl.BlockSpec((tm, n), row), jax.ShapeDtypeStruct((t, n), dt))
    fox = (pl.BlockSpec((None, heads, tm, LANES), lambda i: (i // nsb, 0, i % nsb, 0)),
           jax.ShapeDtypeStruct((bsz, heads, seq, LANES), BF16))
    outs = [tok(cw, BF16), fox, fox, fox, tok(dw, F32), tok(dw, F32), tok(dw, F32)]
    return pl.pallas_call(
        functools.partial(_proj_kernel, cw=cw, fw=fw, dw=dw, nsb=nsb, blk=blk, halo=halo, conv_rows=conv_rows),
        grid=(t // tm,),
        in_specs=[pl.BlockSpec((tm, d), row), _resident(w, l), _resident(fb, l),
                  pl.BlockSpec((tm, LANES), pos), pl.BlockSpec((tm, LANES), pos),
                  pl.BlockSpec((tm, LANES), pos),
                  _resident(conv_w, l), _resident(conv_b, l), _resident(conv_g, l), _resident(conv_beta, l)],
        out_specs=[spec for spec, _ in outs],
        out_shape=[shape for _, shape in outs],
        scratch_shapes=[pltpu.VMEM((d, 2 * cw + 4 * fw + 3 * dw), BF16), pltpu.VMEM((1, fw), F32),
                        pltpu.VMEM((halo + tm, cw), F32)],
        compiler_params=_params("arbitrary"),
        name="mixer_proj",
    )(h, w, fb, cos, slo, shi, conv_w, conv_b, conv_g, conv_beta)


def _fox_kernel(q_ref, k_ref, v_ref, o_ref, m_ref, acc_ref, *, tq, nq):
    i = pl.program_id(2)
    half = tq // 2
    dn = (((1,), (1,)), ((), ()))

    def update(a, rows, k0, width, mask):
        keys = slice(k0, k0 + width)
        s = lax.dot_general(q_ref[a, rows, :], k_ref[a, keys, :], dn, preferred_element_type=F32)
        if mask is not None:
            s = jnp.where(mask, s, NEG_BIG)
        m_old = m_ref[a, rows, :]
        m_new = jnp.maximum(m_old, jnp.max(s, axis=-1, keepdims=True))
        alpha = jnp.exp2(m_old - m_new)
        p = jnp.exp2(s - pltpu.repeat(m_new, width // LANES, axis=1))
        m_ref[a, rows, :] = m_new
        acc_ref[a, rows, :] = alpha * acc_ref[a, rows, :] + jnp.dot(
            p.astype(BF16), v_ref[a, keys, :], preferred_element_type=F32)

    lower = (lax.broadcasted_iota(jnp.int32, (half, half), 1)
             <= lax.broadcasted_iota(jnp.int32, (half, half), 0))
    upper = (lax.broadcasted_iota(jnp.int32, (half, tq), 1)
             <= lax.broadcasted_iota(jnp.int32, (half, tq), 0) + half)
    lane = lax.broadcasted_iota(jnp.int32, (1, LANES), 1)

    def q_block(n_before):
        m_ref[...] = jnp.full(m_ref.shape, NEG_BIG, F32)
        acc_ref[...] = jnp.zeros(acc_ref.shape, F32)
        for a in range(HEADS_PER_SLAB):
            update(a, slice(0, half), n_before * tq, half, lower)
            update(a, slice(half, tq), n_before * tq, tq, upper)
        for j in range(n_before):
            for a in range(HEADS_PER_SLAB):
                update(a, slice(0, tq), j * tq, tq, None)
        acc0, acc1 = acc_ref[0], acc_ref[1]
        out = jnp.where(lane < HEAD_DIM,
                        acc0 / pltpu.roll(acc0, HEAD_DIM, axis=1),
                        pltpu.roll(acc1, HEAD_DIM, axis=1) / acc1)
        o_ref[...] = out.astype(o_ref.dtype)

    for n in range(nq):
        pl.when(i == n)(functools.partial(q_block, n))


def _fox_attention(qa, ka, va, *, tq=1024):
    bsz, heads, seq, _ = qa.shape
    qspec = pl.BlockSpec((None, HEADS_PER_SLAB, tq, LANES), lambda b, p, i: (b, p, i, 0))
    kvspec = pl.BlockSpec((None, HEADS_PER_SLAB, seq, LANES), lambda b, p, i: (b, p, 0, 0))
    return pl.pallas_call(
        functools.partial(_fox_kernel, tq=tq, nq=seq // tq),
        grid=(bsz, heads // HEADS_PER_SLAB, seq // tq),
        in_specs=[qspec, kvspec, kvspec],
        out_specs=pl.BlockSpec((None, tq, LANES), lambda b, p, i: (b, i, p)),
        out_shape=jax.ShapeDtypeStruct((bsz, seq, heads * HEAD_DIM), BF16),
        scratch_shapes=[pltpu.VMEM((HEADS_PER_SLAB, tq, LANES), F32),
                        pltpu.VMEM((HEADS_PER_SLAB, tq, LANES), F32)],
        compiler_params=_params("parallel", "parallel", "arbitrary"),
        name="fox_attention",
    )(qa, ka, va)


def _dil_kernel(q_ref, k_ref, v_ref, o_ref, bias_ref, bias2_ref, m_ref, l_ref, acc_ref, *,
                blk, rows_out):
    seq = q_ref.shape[0]
    win = 2 * blk
    lane = lax.broadcasted_iota(jnp.int32, (1, LANES), 1)
    head0 = lane < HEAD_DIM
    dist0 = (lax.broadcasted_iota(jnp.int32, (win, win), 0) % blk
             - lax.broadcasted_iota(jnp.int32, (win, win), 1))
    row = lax.broadcasted_iota(jnp.int32, (2 * win, win), 0)
    col = lax.broadcasted_iota(jnp.int32, (2 * win, win), 1)
    bias_ref[0] = jnp.where(dist0 >= 0, 0.0, NEG_BIG)
    bias_ref[1] = jnp.where(jnp.logical_and(dist0 + blk >= 0, dist0 <= 0), 0.0, NEG_BIG)
    dist_all = (row % win) - col
    bias2_ref[...] = jnp.where(jnp.logical_and(dist_all >= 0, dist_all <= blk), 0.0, NEG_BIG)
    ones = jnp.ones((win, LANES), BF16)
    dn = (((1,), (1,)), ((), ()))

    def attend(bi, rows, wrows, bias):
        nq = rows.size
        qb = q_ref[rows, :].astype(BF16)
        zero = jnp.zeros_like(qb)
        qs = jnp.concatenate([jnp.where(head0, qb, zero), jnp.where(head0, zero, qb)], axis=0)
        kw = k_ref[wrows, :].astype(BF16)
        vw = jnp.concatenate([v_ref[wrows, :].astype(BF16), ones], axis=1)
        s = lax.dot_general(qs, kw, dn, preferred_element_type=F32) + bias
        m = jnp.max(s, axis=-1, keepdims=True)
        p = jnp.exp2(s - m).astype(BF16)
        pv = jnp.dot(p, vw, preferred_element_type=F32)
        mb = jnp.broadcast_to(m, (2 * nq, LANES))
        m_ref[bi, rows, :] = jnp.where(head0, mb[:nq], mb[nq:])
        acc_ref[bi, rows, :] = jnp.where(head0, pv[:nq, :LANES], pv[nq:, :LANES])
        l_ref[bi, rows, :] = jnp.where(head0, pv[:nq, LANES:], pv[nq:, LANES:])

    def merge(r0):
        rows = slice(r0, r0 + rows_out)
        ms = [m_ref[bi, rows, :] for bi in range(len(DIL_CONFIGS))]
        m_all = functools.reduce(jnp.maximum, ms)
        ws = [jnp.exp2(m - m_all) for m in ms]
        num = sum(w * acc_ref[bi, rows, :] for bi, w in enumerate(ws))
        den = sum(w * l_ref[bi, rows, :] for bi, w in enumerate(ws))
        o_ref[rows, :] = (num / den).astype(o_ref.dtype)

    assert DIL_CONFIGS[0][1] == 1 and rows_out % blk == 0
    for bi, (_, dil) in reversed(list(enumerate(DIL_CONFIGS))):
        nb = seq // (dil * blk)
        ds = (lambda start, size, dil=dil: pl.ds(start, size, stride=dil) if dil > 1 else pl.ds(start, size))
        for r in range(dil):
            if nb == 2:
                attend(bi, ds(r, win), ds(r, win), bias2_ref[...])
                continue
            for n in range(nb):
                start = r + n * (dil * blk)
                attend(bi, ds(start, blk), ds(start - (dil * blk if n else 0), win), bias_ref[1 if n else 0])
                if dil == 1 and (start + blk) % rows_out == 0:
                    merge(start + blk - rows_out)


def _dilated_attention(q, k, v, *, blk=128, rows_out=256):
    bsz, seq, width = q.shape
    nbr = len(DIL_CONFIGS)
    spec = pl.BlockSpec((None, seq, LANES), lambda b, p: (b, 0, p))
    return pl.pallas_call(
        functools.partial(_dil_kernel, blk=blk, rows_out=rows_out),
        grid=(bsz, width // LANES),
        in_specs=[spec, spec, spec],
        out_specs=spec,
        out_shape=jax.ShapeDtypeStruct((bsz, seq, width), BF16),
        scratch_shapes=[pltpu.VMEM((2, 2 * blk, 2 * blk), F32), pltpu.VMEM((4 * blk, 2 * blk), F32)]
        + [pltpu.VMEM((nbr, seq, LANES), F32)] * 3,
        compiler_params=_params("parallel", "parallel"),
        name="dilated_attention",
    )(q, k, v)


def _out_ffn_kernel(h_ref, ya_ref, yb_ref, yc_ref, wraw_ref, g1_ref, b1_ref, win_ref, wout_ref, g2_ref, b2_ref,
                    *rest, n_cast, alpha, cw, fw, d_ff, chunk):
    srcs, (o_ref, *dsts), (w_ref, mid_ref, acc_ref) = rest[:n_cast], rest[n_cast:-3], rest[-3:]
    for src, dst in zip(srcs, dsts):
        dst[...] = src[...].astype(BF16)

    @pl.when(pl.program_id(0) == 0)
    def _():
        w_ref[...] = wraw_ref[...].astype(BF16)

    y = jnp.dot(ya_ref[...], w_ref[0:cw, :], preferred_element_type=F32)
    y = y + jnp.dot(yb_ref[...], w_ref[cw:cw + fw, :], preferred_element_type=F32)
    y = y + jnp.dot(yc_ref[...], w_ref[cw + fw:, :], preferred_element_type=F32)
    mid_ref[...] = _layer_norm(alpha * h_ref[...] + y, g1_ref[...], b1_ref[...])
    _ffn_kernel(mid_ref, win_ref, wout_ref, g2_ref, b2_ref, o_ref, acc_ref, alpha=alpha, d_ff=d_ff, chunk=chunk)


def _out_proj_ffn_ln(h, ya, yb, yc, w, w_in, w_out, g, b, l, next_f32, nxt, *, alpha, tm=512, chunk=256):
    t, d = h.shape
    d_ff = w_out.shape[0]
    row = lambda i: (i, 0)
    srcs, cast_in, cast_out, cast_shapes = _next_weight_cast(next_f32, nxt, t // tm)
    outs = pl.pallas_call(
        functools.partial(_out_ffn_kernel, n_cast=len(srcs), alpha=alpha, cw=ya.shape[1], fw=yb.shape[1],
                          d_ff=d_ff, chunk=chunk),
        grid=(t // tm,),
        in_specs=[pl.BlockSpec((tm, d), row), pl.BlockSpec((tm, ya.shape[1]), row),
                  pl.BlockSpec((tm, yb.shape[1]), row), pl.BlockSpec((tm, yc.shape[1]), row),
                  _resident(w, l), _resident(g, l, 1), _resident(b, l, 1),
                  _resident(w_in), _resident(w_out), _resident(g, l, 2), _resident(b, l, 2)] + cast_in,
        out_specs=[pl.BlockSpec((tm, d), row)] + cast_out,
        out_shape=[jax.ShapeDtypeStruct((t, d), F32)] + cast_shapes,
        scratch_shapes=[pltpu.VMEM((d, d), BF16), pltpu.VMEM((tm, d), F32), pltpu.VMEM((tm, d), F32)],
        compiler_params=_params("arbitrary"),
        name="out_proj_ffn_ln",
    )(h, ya, yb, yc, w, g, b, w_in, w_out, g, b, *srcs)
    return outs[0], outs[1:]


def _rope_tables(seq):
    lane = jnp.arange(LANES)
    inv = 1.0 / (ROPE_THETA ** ((2 * (lane % (HEAD_DIM // 2))).astype(F32) / HEAD_DIM))
    ang = jnp.arange(seq, dtype=F32)[:, None] * inv[None, :]
    cos, sin = jnp.cos(ang), jnp.sin(ang)
    first_half = (lane % HEAD_DIM < HEAD_DIM // 2)[None, :]
    sin_lo = jnp.where(first_half, -sin, 0.0)
    sin_hi = jnp.where(first_half, 0.0, sin)
    return cos, sin_lo, sin_hi


def kernel(x, w_in, w_o, forget_bias, conv_w, conv_b, conv_ln_g, conv_ln_b, ffn_w_in, ffn_w_out, ln_g, ln_b):
    bsz, seq, d = x.shape
    depth = w_in.shape[0]
    cw = conv_w.shape[-1]
    heads_f = forget_bias.shape[-1]
    fw = heads_f * HEAD_DIM
    dw = d - cw - fw
    alpha = (2 * depth) ** 0.25
    t = bsz * seq
    cos, slo, shi = _rope_tables(seq)

    ffn_f32 = (ffn_w_in, ffn_w_out)
    ffn_bf = (ffn_w_in[0, 0].astype(BF16), ffn_w_out[0, 0].astype(BF16))
    fb = jnp.repeat(forget_bias, HEAD_DIM, axis=1)[:, None, :]
    row3 = lambda a: a[:, None, :]
    cb, cg, cbeta = row3(conv_b), row3(conv_ln_g), row3(conv_ln_b)
    g4, b4 = ln_g[:, :, None, :], ln_b[:, :, None, :]
    b3 = lambda a: a.reshape(bsz, seq, a.shape[-1])

    h = x.reshape(t, d)
    for l in range(depth):
        h, ffn_bf = _ffn_ln(h, *ffn_bf, g4, b4, l, 0, ffn_f32, (l, 1), alpha=alpha)
        ya, qa, ka, va, dq, dk, dv = _mixer_proj(h, w_in, fb, cos, slo, shi, conv_w, cb, cg, cbeta, l,
                                                 bsz=bsz, fw=fw, dw=dw)
        yb = _fox_attention(qa, ka, va)
        yc = _dilated_attention(b3(dq), b3(dk), b3(dv))
        h, ffn_bf = _out_proj_ffn_ln(h, ya, yb.reshape(t, fw), yc.reshape(t, dw), w_o, *ffn_bf, g4, b4, l,
                                     ffn_f32, (l + 1, 0) if l + 1 < depth else None, alpha=alpha)
    return h.reshape(bsz, seq, d)
```

```python
import functools

import jax
import jax.numpy as jnp
from jax import lax
from jax.experimental import pallas as pl
from jax.experimental.pallas import tpu as pltpu

HEAD_DIM = 64
CONV_K = 31
DIL_CONFIGS = ((128, 1), (512, 4), (2048, 16))
ROPE_THETA = 10000.0
LN_EPS = 1e-5
LANES = 128
F32_SUBLANES = 8
BF16_SUBLANES = 16
HEADS_PER_SLAB = LANES // HEAD_DIM
NEG_BIG = -1e30
LOG2_E = 1.4426950408889634
SCORE_SCALE = HEAD_DIM ** -0.5 * LOG2_E
VMEM_LIMIT = 56 * 1024 * 1024

F32 = jnp.float32
BF16 = jnp.bfloat16


def _layer_norm(y, g, b):
    mu = jnp.mean(y, axis=-1, keepdims=True)
    yc = y - mu
    var = jnp.mean(yc * yc, axis=-1, keepdims=True)
    return yc * lax.rsqrt(var + LN_EPS) * g + b


def _params(*sem):
    return pltpu.CompilerParams(dimension_semantics=sem, vmem_limit_bytes=VMEM_LIMIT)


def _resident(arr, *lead):
    rest = arr.shape[len(lead):]
    index = tuple(lead) + (0,) * len(rest)
    return pl.BlockSpec((None,) * len(lead) + rest, lambda *_: index, pipeline_mode=pl.Buffered(1))


def _ffn_kernel(x_ref, win_ref, wout_ref, g_ref, b_ref, o_ref, acc_ref, *, alpha, d_ff, chunk):
    x = x_ref[...]
    xb = x.astype(BF16)
    for c in range(d_ff // chunk):
        lo = c * chunk
        gate = jnp.dot(xb, win_ref[:, lo:lo + chunk], preferred_element_type=F32)
        up = jnp.dot(xb, win_ref[:, d_ff + lo:d_ff + lo + chunk], preferred_element_type=F32)
        hid = (gate * jax.nn.sigmoid(gate) * up).astype(BF16)
        part = jnp.dot(hid, wout_ref[lo:lo + chunk, :], preferred_element_type=F32)
        if c == 0:
            acc_ref[...] = part
        else:
            acc_ref[...] += part
    y = alpha * x + 0.5 * acc_ref[...]
    o_ref[...] = _layer_norm(y, g_ref[...], b_ref[...])


def _next_weight_cast(weights, nxt, steps):
    if nxt is None:
        return [], [], [], []
    l, j = nxt
    in_specs, out_specs, out_shapes = [], [], []
    for w in weights:
        rows, cols = w.shape[-2:]
        n = steps
        while rows % n or (rows // n) % BF16_SUBLANES:
            n //= 2
        per = steps // n
        in_specs.append(pl.BlockSpec((None, None, rows // n, cols), lambda i, per=per: (l, j, i // per, 0)))
        out_specs.append(pl.BlockSpec((rows // n, cols), lambda i, per=per: (i // per, 0)))
        out_shapes.append(jax.ShapeDtypeStruct((rows, cols), BF16))
    return list(weights), in_specs, out_specs, out_shapes


def _ffn_ln_kernel(x_ref, win_ref, wout_ref, g_ref, b_ref, *rest, n_cast, **kw):
    srcs, (o_ref, *dsts), acc_ref = rest[:n_cast], rest[n_cast:-1], rest[-1]
    for src, dst in zip(srcs, dsts):
        dst[...] = src[...].astype(BF16)
    _ffn_kernel(x_ref, win_ref, wout_ref, g_ref, b_ref, o_ref, acc_ref, **kw)


def _ffn_ln(x, w_in, w_out, g, b, l, n, next_f32, nxt, *, alpha, tm=1024, chunk=256):
    t, d = x.shape
    d_ff = w_out.shape[0]
    assert t % tm == 0 and d_ff % chunk == 0 and chunk % LANES == 0
    srcs, cast_in, cast_out, cast_shapes = _next_weight_cast(next_f32, nxt, t // tm)
    outs = pl.pallas_call(
        functools.partial(_ffn_ln_kernel, n_cast=len(srcs), alpha=alpha, d_ff=d_ff, chunk=chunk),
        grid=(t // tm,),
        in_specs=[pl.BlockSpec((tm, d), lambda i: (i, 0)), _resident(w_in), _resident(w_out),
                  _resident(g, l, n), _resident(b, l, n)] + cast_in,
        out_specs=[pl.BlockSpec((tm, d), lambda i: (i, 0))] + cast_out,
        out_shape=[jax.ShapeDtypeStruct((t, d), F32)] + cast_shapes,
        scratch_shapes=[pltpu.VMEM((tm, d), F32)],
        compiler_params=_params("arbitrary"),
        name="ffn_ln",
    )(x, w_in, w_out, g, b, *srcs)
    return outs[0], outs[1:]


def _rope_slab(t, cos, sin_lo, sin_hi):
    nxt = pltpu.roll(t, LANES - HEAD_DIM // 2, axis=1)
    prv = pltpu.roll(t, HEAD_DIM // 2, axis=1)
    return t * cos + nxt * sin_lo + prv * sin_hi


def _split3(x):
    hi = x.astype(BF16).astype(F32)
    mid = (x - hi).astype(BF16).astype(F32)
    return hi, mid, x - hi - mid


def _store_fox_operands(cum, q, k, v, qa_ref, ka_ref, va_ref, rows):
    lane = lax.broadcasted_iota(jnp.int32, (1, LANES), 1)
    is_head = lane < HEAD_DIM
    aug = lane - HEAD_DIM
    q_ones = jnp.where(jnp.logical_and(aug >= 3, aug < 6), 1.0, 0.0)
    k_ones = jnp.where(jnp.logical_and(aug >= 0, aug < 3), 1.0, 0.0)
    k_bias = jnp.logical_and(aug >= 3, aug < 6)
    for p in range(q.shape[-1] // LANES):
        slab = slice(p * LANES, (p + 1) * LANES)
        split = _split3(cum[:, slab])
        for a in range(HEADS_PER_SLAB):
            h = HEADS_PER_SLAB * p + a
            odd = a == 1
            head_lanes = lambda x: pltpu.roll(x[:, slab], HEAD_DIM, axis=1) if odd else x[:, slab]
            hi, mid, lo = split if odd else [pltpu.roll(t, HEAD_DIM, axis=1) for t in split]
            q_aug = jnp.where(aug == 0, hi, jnp.where(aug == 1, mid, jnp.where(aug == 2, lo, q_ones)))
            k_aug = jnp.where(k_bias, -pltpu.roll(q_aug, 3, axis=1), k_ones)
            qa_ref[h, rows, :] = jnp.where(is_head, head_lanes(q), q_aug).astype(BF16)
            ka_ref[h, rows, :] = jnp.where(is_head, head_lanes(k), k_aug).astype(BF16)
            va_ref[h, rows, :] = jnp.where(is_head, head_lanes(v), 1.0).astype(BF16)


def _stage_mixer_weights(wraw_ref, w_ref, *, split, heads, rows):
    lane = lax.broadcasted_iota(jnp.int32, (1, LANES), 1)
    tail = wraw_ref.shape[1] - split - heads
    for r0 in range(0, wraw_ref.shape[0], rows):
        rs = slice(r0, r0 + rows)
        w_ref[rs, 0:split] = wraw_ref[rs, 0:split].astype(BF16)
        gates = wraw_ref[rs, split:split + heads]
        for p in range(heads // HEADS_PER_SLAB):
            even = jnp.broadcast_to(gates[:, HEADS_PER_SLAB * p:HEADS_PER_SLAB * p + 1], (rows, LANES))
            odd = jnp.broadcast_to(gates[:, HEADS_PER_SLAB * p + 1:HEADS_PER_SLAB * p + 2], (rows, LANES))
            w_ref[rs, split + p * LANES:split + (p + 1) * LANES] = (
                jnp.where(lane < HEAD_DIM, even, odd).astype(BF16))
        lo = split + heads * HEAD_DIM
        w_ref[rs, lo:lo + tail] = wraw_ref[rs, split + heads:split + heads + tail].astype(BF16)


def _conv_rows(pad_ref, t0, rows, halo, w_ref, cb_ref, g_ref, b_ref, o_ref):
    ch = pad_ref.shape[1]
    first = t0 + halo - (CONV_K - 1)
    slabs = [slice(c0, c0 + LANES) for c0 in range(0, ch, LANES)]
    accs = []
    sub = F32_SUBLANES
    base = first // sub * sub
    span = rows + (first - base + CONV_K - 1 + sub - 1) // sub * sub
    for cs in slabs:
        win = pad_ref[base:base + span, cs]
        parts = []
        for b in range(sub):
            rolled = win if b == 0 else pltpu.roll(win, span - b, axis=0)
            offs = [o for o in range(b, span - rows + 1, sub) if 0 <= base + o - first < CONV_K]
            parts.append(functools.reduce(jnp.add, [
                w_ref[base + o - first:base + o - first + 1, cs] * rolled[o - b:o - b + rows, :]
                for o in offs]))
        while len(parts) > 1:
            parts = [parts[i] + parts[i + 1] for i in range(0, len(parts), 2)]
        accs.append(parts[0] + cb_ref[:, cs])
    mu = sum(jnp.sum(a, axis=-1, keepdims=True) for a in accs) / ch
    cen = [a - mu for a in accs]
    var = sum(jnp.sum(c * c, axis=-1, keepdims=True) for c in cen) / ch
    inv = lax.rsqrt(var + LN_EPS)
    for cs, c in zip(slabs, cen):
        y = c * inv * g_ref[:, cs] + b_ref[:, cs]
        o_ref[t0:t0 + rows, cs] = (y * jax.nn.sigmoid(y)).astype(o_ref.dtype)


def _proj_kernel(h_ref, wraw_ref, fb_ref, cos_ref, slo_ref, shi_ref, cw_ref, cb_ref, cg_ref, cbeta_ref,
                 ya_ref, qa_ref, ka_ref, va_ref, dq_ref, dk_ref, dv_ref, w_ref, carry_ref, pad_ref,
                 *, cw, fw, dw, nsb, blk, halo, conv_rows):
    tm = h_ref.shape[0]

    @pl.when(pl.program_id(0) == 0)
    def _():
        _stage_mixer_weights(wraw_ref, w_ref, split=2 * cw + 3 * fw, heads=fw // HEAD_DIM, rows=blk)

    @pl.when(pl.program_id(0) % nsb == 0)
    def _():
        carry_ref[...] = jnp.zeros_like(carry_ref)
        pad_ref[0:halo, :] = jnp.zeros((halo, cw), F32)

    @pl.when(pl.program_id(0) % nsb != 0)
    def _():
        pad_ref[0:halo, :] = pad_ref[tm:tm + halo, :]

    hb = h_ref[...].astype(BF16)

    def cols(lo, width):
        return jnp.dot(hb, w_ref[:, lo:lo + width], preferred_element_type=F32)

    scale = SCORE_SCALE
    val = cols(0, cw)
    gate = cols(cw, cw)
    pad_ref[halo:halo + tm, :] = val * jax.nn.sigmoid(gate)
    for t0 in range(0, tm, conv_rows):
        _conv_rows(pad_ref, t0, conv_rows, halo, cw_ref, cb_ref, cg_ref, cbeta_ref, ya_ref)
    base = 2 * cw
    fq = cols(base, fw) * scale
    fk = cols(base + fw, fw)
    fv = cols(base + 2 * fw, fw)
    log_f = jax.nn.log_sigmoid(cols(base + 3 * fw, fw) + fb_ref[...])

    r = lax.broadcasted_iota(jnp.int32, (blk, blk), 0)
    c = lax.broadcasted_iota(jnp.int32, (blk, blk), 1)
    tri = (c <= r).astype(BF16)
    cum = carry_ref[...]
    for r0 in range(0, hb.shape[0], blk):
        rows = slice(r0, r0 + blk)
        cum = cum[-1:, :]
        for part in _split3(log_f[rows]):
            cum = cum + jnp.dot(tri, part.astype(BF16), preferred_element_type=F32)
        _store_fox_operands(cum * LOG2_E, fq[rows], fk[rows], fv[rows], qa_ref, ka_ref, va_ref, rows)
    carry_ref[...] = cum[-1:, :]

    base = base + 4 * fw
    cos, slo, shi = cos_ref[...], slo_ref[...], shi_ref[...]
    q, k = cols(base, dw), cols(base + dw, dw)
    for j in range(dw // LANES):
        sl = slice(j * LANES, (j + 1) * LANES)
        dq_ref[:, sl] = _rope_slab(q[:, sl], cos, slo, shi) * scale
        dk_ref[:, sl] = _rope_slab(k[:, sl], cos, slo, shi)
    dv_ref[...] = cols(base + 2 * dw, dw)


def _mixer_proj(h, w, fb, cos, slo, shi, conv_w, conv_b, conv_g, conv_beta, l, *, bsz, fw, dw,
                tm=512, blk=128, halo=32, conv_rows=64):
    t, d = h.shape
    cw = conv_w.shape[-1]
    seq = t // bsz
    nsb = seq // tm
    heads = fw // HEAD_DIM
    assert seq % tm == 0 and tm % blk == 0 and tm % conv_rows == 0 and heads % HEADS_PER_SLAB == 0
    assert CONV_K - 1 <= halo <= tm and halo % F32_SUBLANES == 0 and w.shape[-1] == 2 * cw + 3 * fw + heads + 3 * dw
    row = lambda i: (i, 0)
    pos = lambda i: (i % nsb, 0)
    tok = lambda n, dt: (pl.BlockSpec((tm, n), row), jax.ShapeDtypeStruct((t, n), dt))
    fox = (pl.BlockSpec((None, heads, tm, LANES), lambda i: (i // nsb, 0, i % nsb, 0)),
           jax.ShapeDtypeStruct((bsz, heads, seq, LANES), BF16))
    outs = [tok(cw, BF16), fox, fox, fox, tok(dw, F32), tok(dw, F32), tok(dw, F32)]
    return pl.pallas_call(
        functools.partial(_proj_kernel, cw=cw, fw=fw, dw=dw, nsb=nsb, blk=blk, halo=halo, conv_rows=conv_rows),
        grid=(t // tm,),
        in_specs=[pl.BlockSpec((tm, d), row), _resident(w, l), _resident(fb, l),
                  pl.BlockSpec((tm, LANES), pos), pl.BlockSpec((tm, LANES), pos),
                  pl.BlockSpec((tm, LANES), pos),
                  _resident(conv_w, l), _resident(conv_b, l), _resident(conv_g, l), _resident(conv_beta, l)],
        out_specs=[spec for spec, _ in outs],
        out_shape=[shape for _, shape in outs],
        scratch_shapes=[pltpu.VMEM((d, 2 * cw + 4 * fw + 3 * dw), BF16), pltpu.VMEM((1, fw), F32),
                        pltpu.VMEM((halo + tm, cw), F32)],
        compiler_params=_params("arbitrary"),
        name="mixer_proj",
    )(h, w, fb, cos, slo, shi, conv_w, conv_b, conv_g, conv_beta)


def _fox_kernel(q_ref, k_ref, v_ref, o_ref, m_ref, acc_ref, *, tq, nq):
    i = pl.program_id(2)
    half = tq // 2
    dn = (((1,), (1,)), ((), ()))

    def update(a, rows, k0, width, mask):
        keys = slice(k0, k0 + width)
        s = lax.dot_general(q_ref[a, rows, :], k_ref[a, keys, :], dn, preferred_element_type=F32)
        if mask is not None:
            s = jnp.where(mask, s, NEG_BIG)
        m_old = m_ref[a, rows, :]
        m_new = jnp.maximum(m_old, jnp.max(s, axis=-1, keepdims=True))
        alpha = jnp.exp2(m_old - m_new)
        p = jnp.exp2(s - jnp.concatenate([m_new] * (width // LANES), axis=1))
        m_ref[a, rows, :] = m_new
        acc_ref[a, rows, :] = alpha * acc_ref[a, rows, :] + jnp.dot(
            p.astype(BF16), v_ref[a, keys, :], preferred_element_type=F32)

    lower = (lax.broadcasted_iota(jnp.int32, (half, half), 1)
             <= lax.broadcasted_iota(jnp.int32, (half, half), 0))
    upper = (lax.broadcasted_iota(jnp.int32, (half, tq), 1)
             <= lax.broadcasted_iota(jnp.int32, (half, tq), 0) + half)
    lane = lax.broadcasted_iota(jnp.int32, (1, LANES), 1)

    def q_block(n_before):
        m_ref[...] = jnp.full(m_ref.shape, NEG_BIG, F32)
        acc_ref[...] = jnp.zeros(acc_ref.shape, F32)
        for a in range(HEADS_PER_SLAB):
            update(a, slice(0, half), n_before * tq, half, lower)
            update(a, slice(half, tq), n_before * tq, tq, upper)
        for j in range(n_before):
            for a in range(HEADS_PER_SLAB):
                update(a, slice(0, tq), j * tq, tq, None)
        acc0, acc1 = acc_ref[0], acc_ref[1]
        out = jnp.where(lane < HEAD_DIM,
                        acc0 / pltpu.roll(acc0, HEAD_DIM, axis=1),
                        pltpu.roll(acc1, HEAD_DIM, axis=1) / acc1)
        o_ref[...] = out.astype(o_ref.dtype)

    for n in range(nq):
        pl.when(i == n)(functools.partial(q_block, n))


def _fox_attention(qa, ka, va, *, tq=1024):
    bsz, heads, seq, _ = qa.shape
    assert seq % tq == 0 and heads % HEADS_PER_SLAB == 0
    qspec = pl.BlockSpec((None, HEADS_PER_SLAB, tq, LANES), lambda b, p, i: (b, p, i, 0))
    kvspec = pl.BlockSpec((None, HEADS_PER_SLAB, seq, LANES), lambda b, p, i: (b, p, 0, 0))
    return pl.pallas_call(
        functools.partial(_fox_kernel, tq=tq, nq=seq // tq),
        grid=(bsz, heads // HEADS_PER_SLAB, seq // tq),
        in_specs=[qspec, kvspec, kvspec],
        out_specs=pl.BlockSpec((None, tq, LANES), lambda b, p, i: (b, i, p)),
        out_shape=jax.ShapeDtypeStruct((bsz, seq, heads * HEAD_DIM), BF16),
        scratch_shapes=[pltpu.VMEM((HEADS_PER_SLAB, tq, LANES), F32),
                        pltpu.VMEM((HEADS_PER_SLAB, tq, LANES), F32)],
        compiler_params=_params("parallel", "parallel", "arbitrary"),
        name="fox_attention",
    )(qa, ka, va)


def _dil_kernel(q_ref, k_ref, v_ref, o_ref, bias_ref, bias2_ref, m_ref, l_ref, acc_ref, *,
                blk, rows_out):
    seq = q_ref.shape[0]
    win = 2 * blk
    lane = lax.broadcasted_iota(jnp.int32, (1, LANES), 1)
    head0 = lane < HEAD_DIM
    dist0 = (lax.broadcasted_iota(jnp.int32, (win, win), 0) % blk
             - lax.broadcasted_iota(jnp.int32, (win, win), 1))
    row = lax.broadcasted_iota(jnp.int32, (2 * win, win), 0)
    col = lax.broadcasted_iota(jnp.int32, (2 * win, win), 1)
    bias_ref[0] = jnp.where(dist0 >= 0, 0.0, NEG_BIG)
    bias_ref[1] = jnp.where(jnp.logical_and(dist0 + blk >= 0, dist0 <= 0), 0.0, NEG_BIG)
    dist_all = (row % win) - col
    bias2_ref[...] = jnp.where(jnp.logical_and(dist_all >= 0, dist_all <= blk), 0.0, NEG_BIG)
    ones = jnp.ones((win, LANES), BF16)
    dn = (((1,), (1,)), ((), ()))

    def attend(bi, rows, wrows, bias):
        nq = rows.size
        qb = q_ref[rows, :].astype(BF16)
        zero = jnp.zeros_like(qb)
        qs = jnp.concatenate([jnp.where(head0, qb, zero), jnp.where(head0, zero, qb)], axis=0)
        kw = k_ref[wrows, :].astype(BF16)
        vw = jnp.concatenate([v_ref[wrows, :].astype(BF16), ones], axis=1)
        s = lax.dot_general(qs, kw, dn, preferred_element_type=F32) + bias
        m = jnp.max(s, axis=-1, keepdims=True)
        p = jnp.exp2(s - m).astype(BF16)
        pv = jnp.dot(p, vw, preferred_element_type=F32)
        mb = jnp.broadcast_to(m, (2 * nq, LANES))
        m_ref[bi, rows, :] = jnp.where(head0, mb[:nq], mb[nq:])
        acc_ref[bi, rows, :] = jnp.where(head0, pv[:nq, :LANES], pv[nq:, :LANES])
        l_ref[bi, rows, :] = jnp.where(head0, pv[:nq, LANES:], pv[nq:, LANES:])

    def merge(r0):
        rows = slice(r0, r0 + rows_out)
        ms = [m_ref[bi, rows, :] for bi in range(len(DIL_CONFIGS))]
        m_all = functools.reduce(jnp.maximum, ms)
        ws = [jnp.exp2(m - m_all) for m in ms]
        num = sum(w * acc_ref[bi, rows, :] for bi, w in enumerate(ws))
        den = sum(w * l_ref[bi, rows, :] for bi, w in enumerate(ws))
        o_ref[rows, :] = (num / den).astype(o_ref.dtype)

    assert DIL_CONFIGS[0][1] == 1 and rows_out % blk == 0
    for bi, (_, dil) in reversed(list(enumerate(DIL_CONFIGS))):
        nb = seq // (dil * blk)
        ds = (lambda start, size, dil=dil: pl.ds(start, size, stride=dil) if dil > 1 else pl.ds(start, size))
        for r in range(dil):
            if nb == 2:
                attend(bi, ds(r, win), ds(r, win), bias2_ref[...])
                continue
            for n in range(nb):
                start = r + n * (dil * blk)
                attend(bi, ds(start, blk), ds(start - (dil * blk if n else 0), win), bias_ref[1 if n else 0])
                if dil == 1 and (start + blk) % rows_out == 0:
                    merge(start + blk - rows_out)


def _dilated_attention(q, k, v, *, blk=128, rows_out=256):
    bsz, seq, width = q.shape
    nbr = len(DIL_CONFIGS)
    assert all(seq % (dil * blk) == 0 and seq // (dil * blk) >= 2 and win == dil * blk for win, dil in DIL_CONFIGS)
    assert width % LANES == 0 and seq % rows_out == 0
    spec = pl.BlockSpec((None, seq, LANES), lambda b, p: (b, 0, p))
    return pl.pallas_call(
        functools.partial(_dil_kernel, blk=blk, rows_out=rows_out),
        grid=(bsz, width // LANES),
        in_specs=[spec, spec, spec],
        out_specs=spec,
        out_shape=jax.ShapeDtypeStruct((bsz, seq, width), BF16),
        scratch_shapes=[pltpu.VMEM((2, 2 * blk, 2 * blk), F32), pltpu.VMEM((4 * blk, 2 * blk), F32)]
        + [pltpu.VMEM((nbr, seq, LANES), F32)] * 3,
        compiler_params=_params("parallel", "parallel"),
        name="dilated_attention",
    )(q, k, v)


def _out_ffn_kernel(h_ref, ya_ref, yb_ref, yc_ref, wraw_ref, g1_ref, b1_ref, win_ref, wout_ref, g2_ref, b2_ref,
                    *rest, n_cast, alpha, cw, fw, d_ff, chunk):
    srcs, (o_ref, *dsts), (w_ref, mid_ref, acc_ref) = rest[:n_cast], rest[n_cast:-3], rest[-3:]
    for src, dst in zip(srcs, dsts):
        dst[...] = src[...].astype(BF16)

    @pl.when(pl.program_id(0) == 0)
    def _():
        w_ref[...] = wraw_ref[...].astype(BF16)

    y = jnp.dot(ya_ref[...], w_ref[0:cw, :], preferred_element_type=F32)
    y = y + jnp.dot(yb_ref[...], w_ref[cw:cw + fw, :], preferred_element_type=F32)
    y = y + jnp.dot(yc_ref[...], w_ref[cw + fw:, :], preferred_element_type=F32)
    mid_ref[...] = _layer_norm(alpha * h_ref[...] + y, g1_ref[...], b1_ref[...])
    _ffn_kernel(mid_ref, win_ref, wout_ref, g2_ref, b2_ref, o_ref, acc_ref, alpha=alpha, d_ff=d_ff, chunk=chunk)


def _out_proj_ffn_ln(h, ya, yb, yc, w, w_in, w_out, g, b, l, next_f32, nxt, *, alpha, tm=512, chunk=256):
    t, d = h.shape
    d_ff = w_out.shape[0]
    assert t % tm == 0 and d_ff % chunk == 0 and ya.shape[1] + yb.shape[1] + yc.shape[1] == d
    row = lambda i: (i, 0)
    srcs, cast_in, cast_out, cast_shapes = _next_weight_cast(next_f32, nxt, t // tm)
    outs = pl.pallas_call(
        functools.partial(_out_ffn_kernel, n_cast=len(srcs), alpha=alpha, cw=ya.shape[1], fw=yb.shape[1],
                          d_ff=d_ff, chunk=chunk),
        grid=(t // tm,),
        in_specs=[pl.BlockSpec((tm, d), row), pl.BlockSpec((tm, ya.shape[1]), row),
                  pl.BlockSpec((tm, yb.shape[1]), row), pl.BlockSpec((tm, yc.shape[1]), row),
                  _resident(w, l), _resident(g, l, 1), _resident(b, l, 1),
                  _resident(w_in), _resident(w_out), _resident(g, l, 2), _resident(b, l, 2)] + cast_in,
        out_specs=[pl.BlockSpec((tm, d), row)] + cast_out,
        out_shape=[jax.ShapeDtypeStruct((t, d), F32)] + cast_shapes,
        scratch_shapes=[pltpu.VMEM((d, d), BF16), pltpu.VMEM((tm, d), F32), pltpu.VMEM((tm, d), F32)],
        compiler_params=_params("arbitrary"),
        name="out_proj_ffn_ln",
    )(h, ya, yb, yc, w, g, b, w_in, w_out, g, b, *srcs)
    return outs[0], outs[1:]


def _rope_tables(seq):
    lane = jnp.arange(LANES)
    inv = 1.0 / (ROPE_THETA ** ((2 * (lane % (HEAD_DIM // 2))).astype(F32) / HEAD_DIM))
    ang = jnp.arange(seq, dtype=F32)[:, None] * inv[None, :]
    cos, sin = jnp.cos(ang), jnp.sin(ang)
    first_half = (lane % HEAD_DIM < HEAD_DIM // 2)[None, :]
    sin_lo = jnp.where(first_half, -sin, 0.0)
    sin_hi = jnp.where(first_half, 0.0, sin)
    return cos, sin_lo, sin_hi


def kernel(x, w_in, w_o, forget_bias, conv_w, conv_b, conv_ln_g, conv_ln_b, ffn_w_in, ffn_w_out, ln_g, ln_b):
    bsz, seq, d = x.shape
    depth = w_in.shape[0]
    cw = conv_w.shape[-1]
    heads_f = forget_bias.shape[-1]
    fw = heads_f * HEAD_DIM
    dw = d - cw - fw
    alpha = (2 * depth) ** 0.25
    t = bsz * seq
    cos, slo, shi = _rope_tables(seq)

    ffn_f32 = (ffn_w_in, ffn_w_out)
    ffn_bf = (ffn_w_in[0, 0].astype(BF16), ffn_w_out[0, 0].astype(BF16))
    fb = jnp.repeat(forget_bias, HEAD_DIM, axis=1)[:, None, :]
    row3 = lambda a: a[:, None, :]
    cb, cg, cbeta = row3(conv_b), row3(conv_ln_g), row3(conv_ln_b)
    g4, b4 = ln_g[:, :, None, :], ln_b[:, :, None, :]
    b3 = lambda a: a.reshape(bsz, seq, a.shape[-1])

    h = x.reshape(t, d)
    for l in range(depth):
        h, ffn_bf = _ffn_ln(h, *ffn_bf, g4, b4, l, 0, ffn_f32, (l, 1), alpha=alpha)
        ya, qa, ka, va, dq, dk, dv = _mixer_proj(h, w_in, fb, cos, slo, shi, conv_w, cb, cg, cbeta, l,
                                                 bsz=bsz, fw=fw, dw=dw)
        yb = _fox_attention(qa, ka, va)
        yc = _dilated_attention(b3(dq), b3(dk), b3(dv))
        h, ffn_bf = _out_proj_ffn_ln(h, ya, yb.reshape(t, fw), yc.reshape(t, dw), w_o, *ffn_bf, g4, b4, l,
                                     ffn_f32, (l + 1, 0) if l + 1 < depth else None, alpha=alpha)
    return h.reshape(bsz, seq, d)
```

```python
import functools

import jax
import jax.numpy as jnp
from jax import lax
from jax.experimental import pallas as pl
from jax.experimental.pallas import tpu as pltpu

HEAD_DIM = 64
CONV_K = 31
DIL_CONFIGS = ((128, 1), (512, 4), (2048, 16))
ROPE_THETA = 10000.0
LN_EPS = 1e-5
LANES = 128
F32_SUBLANES = 8
BF16_SUBLANES = 16
HEADS_PER_SLAB = LANES // HEAD_DIM
NEG_BIG = -1e30
LOG2_E = 1.4426950408889634
SCORE_SCALE = HEAD_DIM ** -0.5 * LOG2_E
VMEM_LIMIT = 56 * 1024 * 1024

F32 = jnp.float32
BF16 = jnp.bfloat16


def _layer_norm(y, g, b):
    mu = jnp.mean(y, axis=-1, keepdims=True)
    yc = y - mu
    var = jnp.mean(yc * yc, axis=-1, keepdims=True)
    return yc * lax.rsqrt(var + LN_EPS) * g + b


def _params(*sem):
    return pltpu.CompilerParams(dimension_semantics=sem, vmem_limit_bytes=VMEM_LIMIT)


def _resident(arr, *lead):
    rest = arr.shape[len(lead):]
    index = tuple(lead) + (0,) * len(rest)
    return pl.BlockSpec((None,) * len(lead) + rest, lambda *_: index, pipeline_mode=pl.Buffered(1))


def _ffn_kernel(x_ref, win_ref, wout_ref, g_ref, b_ref, o_ref, acc_ref, *, alpha, d_ff, chunk):
    x = x_ref[...]
    xb = x.astype(BF16)
    for c in range(d_ff // chunk):
        lo = c * chunk
        gate = jnp.dot(xb, win_ref[:, lo:lo + chunk], preferred_element_type=F32)
        up = jnp.dot(xb, win_ref[:, d_ff + lo:d_ff + lo + chunk], preferred_element_type=F32)
        hid = (gate * jax.nn.sigmoid(gate) * up).astype(BF16)
        part = jnp.dot(hid, wout_ref[lo:lo + chunk, :], preferred_element_type=F32)
        if c == 0:
            acc_ref[...] = part
        else:
            acc_ref[...] += part
    y = alpha * x + 0.5 * acc_ref[...]
    o_ref[...] = _layer_norm(y, g_ref[...], b_ref[...])


def _next_weight_cast(weights, nxt, steps):
    if nxt is None:
        return [], [], [], []
    l, j = nxt
    in_specs, out_specs, out_shapes = [], [], []
    for w in weights:
        rows, cols = w.shape[-2:]
        n = steps
        while rows % n or (rows // n) % BF16_SUBLANES:
            n //= 2
        per = steps // n
        in_specs.append(pl.BlockSpec((None, None, rows // n, cols), lambda i, per=per: (l, j, i // per, 0)))
        out_specs.append(pl.BlockSpec((rows // n, cols), lambda i, per=per: (i // per, 0)))
        out_shapes.append(jax.ShapeDtypeStruct((rows, cols), BF16))
    return list(weights), in_specs, out_specs, out_shapes


def _ffn_ln_kernel(x_ref, win_ref, wout_ref, g_ref, b_ref, *rest, n_cast, **kw):
    srcs, (o_ref, *dsts), acc_ref = rest[:n_cast], rest[n_cast:-1], rest[-1]
    for src, dst in zip(srcs, dsts):
        dst[...] = src[...].astype(BF16)
    _ffn_kernel(x_ref, win_ref, wout_ref, g_ref, b_ref, o_ref, acc_ref, **kw)


def _ffn_ln(x, w_in, w_out, g, b, l, n, next_f32, nxt, *, alpha, tm=1024, chunk=256):
    t, d = x.shape
    d_ff = w_out.shape[0]
    assert t % tm == 0 and d_ff % chunk == 0 and chunk % LANES == 0
    srcs, cast_in, cast_out, cast_shapes = _next_weight_cast(next_f32, nxt, t // tm)
    outs = pl.pallas_call(
        functools.partial(_ffn_ln_kernel, n_cast=len(srcs), alpha=alpha, d_ff=d_ff, chunk=chunk),
        grid=(t // tm,),
        in_specs=[pl.BlockSpec((tm, d), lambda i: (i, 0)), _resident(w_in), _resident(w_out),
                  _resident(g, l, n), _resident(b, l, n)] + cast_in,
        out_specs=[pl.BlockSpec((tm, d), lambda i: (i, 0))] + cast_out,
        out_shape=[jax.ShapeDtypeStruct((t, d), F32)] + cast_shapes,
        scratch_shapes=[pltpu.VMEM((tm, d), F32)],
        compiler_params=_params("arbitrary"),
        name="ffn_ln",
    )(x, w_in, w_out, g, b, *srcs)
    return outs[0], outs[1:]


def _rope_slab(t, cos, sin_lo, sin_hi):
    nxt = pltpu.roll(t, LANES - HEAD_DIM // 2, axis=1)
    prv = pltpu.roll(t, HEAD_DIM // 2, axis=1)
    return t * cos + nxt * sin_lo + prv * sin_hi


def _split3(x):
    hi = x.astype(BF16).astype(F32)
    mid = (x - hi).astype(BF16).astype(F32)
    return hi, mid, x - hi - mid


def _store_fox_operands(cum, q, k, v, qa_ref, ka_ref, va_ref, rows):
    lane = lax.broadcasted_iota(jnp.int32, (1, LANES), 1)
    is_head = lane < HEAD_DIM
    aug = lane - HEAD_DIM
    q_ones = jnp.where(jnp.logical_and(aug >= 3, aug < 6), 1.0, 0.0)
    k_ones = jnp.where(jnp.logical_and(aug >= 0, aug < 3), 1.0, 0.0)
    k_bias = jnp.logical_and(aug >= 3, aug < 6)
    for p in range(q.shape[-1] // LANES):
        slab = slice(p * LANES, (p + 1) * LANES)
        split = _split3(cum[:, slab])
        for a in range(HEADS_PER_SLAB):
            h = HEADS_PER_SLAB * p + a
            odd = a == 1
            head_lanes = lambda x: pltpu.roll(x[:, slab], HEAD_DIM, axis=1) if odd else x[:, slab]
            hi, mid, lo = split if odd else [pltpu.roll(t, HEAD_DIM, axis=1) for t in split]
            q_aug = jnp.where(aug == 0, hi, jnp.where(aug == 1, mid, jnp.where(aug == 2, lo, q_ones)))
            k_aug = jnp.where(k_bias, -pltpu.roll(q_aug, 3, axis=1), k_ones)
            qa_ref[h, rows, :] = jnp.where(is_head, head_lanes(q), q_aug).astype(BF16)
            ka_ref[h, rows, :] = jnp.where(is_head, head_lanes(k), k_aug).astype(BF16)
            va_ref[h, rows, :] = jnp.where(is_head, head_lanes(v), 1.0).astype(BF16)


def _stage_mixer_weights(wraw_ref, w_ref, *, split, heads, rows):
    lane = lax.broadcasted_iota(jnp.int32, (1, LANES), 1)
    tail = wraw_ref.shape[1] - split - heads
    for r0 in range(0, wraw_ref.shape[0], rows):
        rs = slice(r0, r0 + rows)
        w_ref[rs, 0:split] = wraw_ref[rs, 0:split].astype(BF16)
        gates = wraw_ref[rs, split:split + heads].astype(F32)
        for p in range(heads // HEADS_PER_SLAB):
            even = jnp.broadcast_to(gates[:, HEADS_PER_SLAB * p:HEADS_PER_SLAB * p + 1], (rows, LANES))
            odd = jnp.broadcast_to(gates[:, HEADS_PER_SLAB * p + 1:HEADS_PER_SLAB * p + 2], (rows, LANES))
            w_ref[rs, split + p * LANES:split + (p + 1) * LANES] = (
                jnp.where(lane < HEAD_DIM, even, odd).astype(BF16))
        lo = split + heads * HEAD_DIM
        w_ref[rs, lo:lo + tail] = wraw_ref[rs, split + heads:split + heads + tail].astype(BF16)


def _conv_rows(pad_ref, t0, rows, halo, w_ref, cb_ref, g_ref, b_ref, o_ref):
    ch = pad_ref.shape[1]
    first = t0 + halo - (CONV_K - 1)
    slabs = [slice(c0, c0 + LANES) for c0 in range(0, ch, LANES)]
    accs = []
    sub = F32_SUBLANES
    base = first // sub * sub
    span = rows + (first - base + CONV_K - 1 + sub - 1) // sub * sub
    for cs in slabs:
        win = pad_ref[base:base + span, cs]
        parts = []
        for b in range(sub):
            rolled = win if b == 0 else pltpu.roll(win, span - b, axis=0)
            offs = [o for o in range(b, span - rows + 1, sub) if 0 <= base + o - first < CONV_K]
            parts.append(functools.reduce(jnp.add, [
                w_ref[base + o - first:base + o - first + 1, cs] * rolled[o - b:o - b + rows, :]
                for o in offs]))
        while len(parts) > 1:
            parts = [parts[i] + parts[i + 1] for i in range(0, len(parts), 2)]
        accs.append(parts[0] + cb_ref[:, cs])
    mu = sum(jnp.sum(a, axis=-1, keepdims=True) for a in accs) / ch
    cen = [a - mu for a in accs]
    var = sum(jnp.sum(c * c, axis=-1, keepdims=True) for c in cen) / ch
    inv = lax.rsqrt(var + LN_EPS)
    for cs, c in zip(slabs, cen):
        y = c * inv * g_ref[:, cs] + b_ref[:, cs]
        o_ref[t0:t0 + rows, cs] = (y * jax.nn.sigmoid(y)).astype(o_ref.dtype)


def _proj_kernel(h_ref, wraw_ref, fb_ref, cos_ref, slo_ref, shi_ref, cw_ref, cb_ref, cg_ref, cbeta_ref,
                 ya_ref, qa_ref, ka_ref, va_ref, dq_ref, dk_ref, dv_ref, w_ref, carry_ref, pad_ref,
                 *, cw, fw, dw, nsb, blk, halo, conv_rows):
    tm = h_ref.shape[0]

    @pl.when(pl.program_id(0) == 0)
    def _():
        _stage_mixer_weights(wraw_ref, w_ref, split=2 * cw + 3 * fw, heads=fw // HEAD_DIM, rows=blk)

    @pl.when(pl.program_id(0) % nsb == 0)
    def _():
        carry_ref[...] = jnp.zeros_like(carry_ref)
        pad_ref[0:halo, :] = jnp.zeros((halo, cw), F32)

    @pl.when(pl.program_id(0) % nsb != 0)
    def _():
        pad_ref[0:halo, :] = pad_ref[tm:tm + halo, :]

    hb = h_ref[...].astype(BF16)

    def cols(lo, width):
        return jnp.dot(hb, w_ref[:, lo:lo + width], preferred_element_type=F32)

    scale = SCORE_SCALE
    val = cols(0, cw)
    gate = cols(cw, cw)
    pad_ref[halo:halo + tm, :] = val * jax.nn.sigmoid(gate)
    for t0 in range(0, tm, conv_rows):
        _conv_rows(pad_ref, t0, conv_rows, halo, cw_ref, cb_ref, cg_ref, cbeta_ref, ya_ref)
    base = 2 * cw
    fq = cols(base, fw) * scale
    fk = cols(base + fw, fw)
    fv = cols(base + 2 * fw, fw)
    log_f = jax.nn.log_sigmoid(cols(base + 3 * fw, fw) + fb_ref[...])

    r = lax.broadcasted_iota(jnp.int32, (blk, blk), 0)
    c = lax.broadcasted_iota(jnp.int32, (blk, blk), 1)
    tri = (c <= r).astype(BF16)
    cum = carry_ref[...]
    for r0 in range(0, hb.shape[0], blk):
        rows = slice(r0, r0 + blk)
        cum = cum[-1:, :]
        for part in _split3(log_f[rows]):
            cum = cum + jnp.dot(tri, part.astype(BF16), preferred_element_type=F32)
        _store_fox_operands(cum * LOG2_E, fq[rows], fk[rows], fv[rows], qa_ref, ka_ref, va_ref, rows)
    carry_ref[...] = cum[-1:, :]

    base = base + 4 * fw
    cos, slo, shi = cos_ref[...], slo_ref[...], shi_ref[...]
    q, k = cols(base, dw), cols(base + dw, dw)
    for j in range(dw // LANES):
        sl = slice(j * LANES, (j + 1) * LANES)
        dq_ref[:, sl] = _rope_slab(q[:, sl], cos, slo, shi) * scale
        dk_ref[:, sl] = _rope_slab(k[:, sl], cos, slo, shi)
    dv_ref[...] = cols(base + 2 * dw, dw)


def _mixer_proj(h, w, fb, cos, slo, shi, conv_w, conv_b, conv_g, conv_beta, l, *, bsz, fw, dw,
                tm=512, blk=128, halo=32, conv_rows=64):
    t, d = h.shape
    cw = conv_w.shape[-1]
    seq = t // bsz
    nsb = seq // tm
    heads = fw // HEAD_DIM
    assert seq % tm == 0 and tm % blk == 0 and tm % conv_rows == 0 and heads % HEADS_PER_SLAB == 0
    assert CONV_K - 1 <= halo <= tm and halo % F32_SUBLANES == 0 and w.shape[-1] == 2 * cw + 3 * fw + heads + 3 * dw
    row = lambda i: (i, 0)
    pos = lambda i: (i % nsb, 0)
    tok = lambda n, dt: (pl.BlockSpec((tm, n), row), jax.ShapeDtypeStruct((t, n), dt))
    fox = (pl.BlockSpec((None, heads, tm, LANES), lambda i: (i // nsb, 0, i % nsb, 0)),
           jax.ShapeDtypeStruct((bsz, heads, seq, LANES), BF16))
    outs = [tok(cw, BF16), fox, fox, fox, tok(dw, F32), tok(dw, F32), tok(dw, F32)]
    return pl.pallas_call(
        functools.partial(_proj_kernel, cw=cw, fw=fw, dw=dw, nsb=nsb, blk=blk, halo=halo, conv_rows=conv_rows),
        grid=(t // tm,),
        in_specs=[pl.BlockSpec((tm, d), row), _resident(w, l), _resident(fb, l),
                  pl.BlockSpec((tm, LANES), pos), pl.BlockSpec((tm, LANES), pos),
                  pl.BlockSpec((tm, LANES), pos),
                  _resident(conv_w, l), _resident(conv_b, l), _resident(conv_g, l), _resident(conv_beta, l)],
        out_specs=[spec for spec, _ in outs],
        out_shape=[shape for _, shape in outs],
        scratch_shapes=[pltpu.VMEM((d, 2 * cw + 4 * fw + 3 * dw), BF16), pltpu.VMEM((1, fw), F32),
                        pltpu.VMEM((halo + tm, cw), F32)],
        compiler_params=_params("arbitrary"),
        name="mixer_proj",
    )(h, w, fb, cos, slo, shi, conv_w, conv_b, conv_g, conv_beta)


def _fox_kernel(q_ref, k_ref, v_ref, o_ref, m_ref, acc_ref, *, tq, nq):
    i = pl.program_id(2)
    half = tq // 2
    dn = (((1,), (1,)), ((), ()))

    def update(a, rows, k0, width, mask):
        keys = slice(k0, k0 + width)
        s = lax.dot_general(q_ref[a, rows, :], k_ref[a, keys, :], dn, preferred_element_type=F32)
        if mask is not None:
            s = jnp.where(mask, s, NEG_BIG)
        m_old = m_ref[a, rows, :]
        m_new = jnp.maximum(m_old, jnp.max(s, axis=-1, keepdims=True))
        alpha = jnp.exp2(m_old - m_new)
        p = jnp.exp2(s - jnp.concatenate([m_new] * (width // LANES), axis=1))
        m_ref[a, rows, :] = m_new
        acc_ref[a, rows, :] = alpha * acc_ref[a, rows, :] + jnp.dot(
            p.astype(BF16), v_ref[a, keys, :], preferred_element_type=F32)

    lower = (lax.broadcasted_iota(jnp.int32, (half, half), 1)
             <= lax.broadcasted_iota(jnp.int32, (half, half), 0))
    upper = (lax.broadcasted_iota(jnp.int32, (half, tq), 1)
             <= lax.broadcasted_iota(jnp.int32, (half, tq), 0) + half)
    lane = lax.broadcasted_iota(jnp.int32, (1, LANES), 1)

    def q_block(n_before):
        m_ref[...] = jnp.full(m_ref.shape, NEG_BIG, F32)
        acc_ref[...] = jnp.zeros(acc_ref.shape, F32)
        for a in range(HEADS_PER_SLAB):
            update(a, slice(0, half), n_before * tq, half, lower)
            update(a, slice(half, tq), n_before * tq, tq, upper)
        for j in range(n_before):
            for a in range(HEADS_PER_SLAB):
                update(a, slice(0, tq), j * tq, tq, None)
        acc0, acc1 = acc_ref[0], acc_ref[1]
        out = jnp.where(lane < HEAD_DIM,
                        acc0 / pltpu.roll(acc0, HEAD_DIM, axis=1),
                        pltpu.roll(acc1, HEAD_DIM, axis=1) / acc1)
        o_ref[...] = out.astype(o_ref.dtype)

    for n in range(nq):
        pl.when(i == n)(functools.partial(q_block, n))


def _fox_attention(qa, ka, va, *, tq=1024):
    bsz, heads, seq, _ = qa.shape
    assert seq % tq == 0 and heads % HEADS_PER_SLAB == 0
    qspec = pl.BlockSpec((None, HEADS_PER_SLAB, tq, LANES), lambda b, p, i: (b, p, i, 0))
    kvspec = pl.BlockSpec((None, HEADS_PER_SLAB, seq, LANES), lambda b, p, i: (b, p, 0, 0))
    return pl.pallas_call(
        functools.partial(_fox_kernel, tq=tq, nq=seq // tq),
        grid=(bsz, heads // HEADS_PER_SLAB, seq // tq),
        in_specs=[qspec, kvspec, kvspec],
        out_specs=pl.BlockSpec((None, tq, LANES), lambda b, p, i: (b, i, p)),
        out_shape=jax.ShapeDtypeStruct((bsz, seq, heads * HEAD_DIM), BF16),
        scratch_shapes=[pltpu.VMEM((HEADS_PER_SLAB, tq, LANES), F32),
                        pltpu.VMEM((HEADS_PER_SLAB, tq, LANES), F32)],
        compiler_params=_params("parallel", "parallel", "arbitrary"),
        name="fox_attention",
    )(qa, ka, va)


def _dil_kernel(q_ref, k_ref, v_ref, o_ref, bias_ref, bias2_ref, m_ref, l_ref, acc_ref, *,
                blk, rows_out):
    seq = q_ref.shape[0]
    win = 2 * blk
    lane = lax.broadcasted_iota(jnp.int32, (1, LANES), 1)
    head0 = lane < HEAD_DIM
    dist0 = (lax.broadcasted_iota(jnp.int32, (win, win), 0) % blk
             - lax.broadcasted_iota(jnp.int32, (win, win), 1))
    row = lax.broadcasted_iota(jnp.int32, (2 * win, win), 0)
    col = lax.broadcasted_iota(jnp.int32, (2 * win, win), 1)
    bias_ref[0] = jnp.where(dist0 >= 0, 0.0, NEG_BIG)
    bias_ref[1] = jnp.where(jnp.logical_and(dist0 + blk >= 0, dist0 <= 0), 0.0, NEG_BIG)
    dist_all = (row % win) - col
    bias2_ref[...] = jnp.where(jnp.logical_and(dist_all >= 0, dist_all <= blk), 0.0, NEG_BIG)
    ones = jnp.ones((win, LANES), BF16)
    dn = (((1,), (1,)), ((), ()))

    def attend(bi, rows, wrows, bias):
        nq = rows.size
        qb = q_ref[rows, :].astype(BF16)
        zero = jnp.zeros_like(qb)
        qs = jnp.concatenate([jnp.where(head0, qb, zero), jnp.where(head0, zero, qb)], axis=0)
        kw = k_ref[wrows, :].astype(BF16)
        vw = jnp.concatenate([v_ref[wrows, :].astype(BF16), ones], axis=1)
        s = lax.dot_general(qs, kw, dn, preferred_element_type=F32) + bias
        m = jnp.max(s, axis=-1, keepdims=True)
        p = jnp.exp2(s - m).astype(BF16)
        pv = jnp.dot(p, vw, preferred_element_type=F32)
        mb = jnp.broadcast_to(m, (2 * nq, LANES))
        m_ref[bi, rows, :] = jnp.where(head0, mb[:nq], mb[nq:])
        acc_ref[bi, rows, :] = jnp.where(head0, pv[:nq, :LANES], pv[nq:, :LANES])
        l_ref[bi, rows, :] = jnp.where(head0, pv[:nq, LANES:], pv[nq:, LANES:])

    def merge(r0):
        rows = slice(r0, r0 + rows_out)
        ms = [m_ref[bi, rows, :] for bi in range(len(DIL_CONFIGS))]
        m_all = functools.reduce(jnp.maximum, ms)
        ws = [jnp.exp2(m - m_all) for m in ms]
        num = sum(w * acc_ref[bi, rows, :] for bi, w in enumerate(ws))
        den = sum(w * l_ref[bi, rows, :] for bi, w in enumerate(ws))
        o_ref[rows, :] = (num / den).astype(o_ref.dtype)

    assert DIL_CONFIGS[0][1] == 1 and rows_out % blk == 0
    for bi, (_, dil) in reversed(list(enumerate(DIL_CONFIGS))):
        nb = seq // (dil * blk)
        ds = (lambda start, size, dil=dil: pl.ds(start, size, stride=dil) if dil > 1 else pl.ds(start, size))
        for r in range(dil):
            if nb == 2:
                attend(bi, ds(r, win), ds(r, win), bias2_ref[...])
                continue
            for n in range(nb):
                start = r + n * (dil * blk)
                attend(bi, ds(start, blk), ds(start - (dil * blk if n else 0), win), bias_ref[1 if n else 0])
                if dil == 1 and (start + blk) % rows_out == 0:
                    merge(start + blk - rows_out)


def _dilated_attention(q, k, v, *, blk=128, rows_out=256):
    bsz, seq, width = q.shape
    nbr = len(DIL_CONFIGS)
    assert all(seq % (dil * blk) == 0 and seq // (dil * blk) >= 2 and win == dil * blk for win, dil in DIL_CONFIGS)
    assert width % LANES == 0 and seq % rows_out == 0
    spec = pl.BlockSpec((None, seq, LANES), lambda b, p: (b, 0, p))
    return pl.pallas_call(
        functools.partial(_dil_kernel, blk=blk, rows_out=rows_out),
        grid=(bsz, width // LANES),
        in_specs=[spec, spec, spec],
        out_specs=spec,
        out_shape=jax.ShapeDtypeStruct((bsz, seq, width), BF16),
        scratch_shapes=[pltpu.VMEM((2, 2 * blk, 2 * blk), F32), pltpu.VMEM((4 * blk, 2 * blk), F32)]
        + [pltpu.VMEM((nbr, seq, LANES), F32)] * 3,
        compiler_params=_params("parallel", "parallel"),
        name="dilated_attention",
    )(q, k, v)


def _out_ffn_kernel(h_ref, ya_ref, yb_ref, yc_ref, wraw_ref, g1_ref, b1_ref, win_ref, wout_ref, g2_ref, b2_ref,
                    *rest, n_cast, alpha, cw, fw, d_ff, chunk):
    srcs, (o_ref, *dsts), (w_ref, mid_ref, acc_ref) = rest[:n_cast], rest[n_cast:-3], rest[-3:]
    for src, dst in zip(srcs, dsts):
        dst[...] = src[...].astype(BF16)

    @pl.when(pl.program_id(0) == 0)
    def _():
        w_ref[...] = wraw_ref[...].astype(BF16)

    y = jnp.dot(ya_ref[...], w_ref[0:cw, :], preferred_element_type=F32)
    y = y + jnp.dot(yb_ref[...], w_ref[cw:cw + fw, :], preferred_element_type=F32)
    y = y + jnp.dot(yc_ref[...], w_ref[cw + fw:, :], preferred_element_type=F32)
    mid_ref[...] = _layer_norm(alpha * h_ref[...] + y, g1_ref[...], b1_ref[...])
    _ffn_kernel(mid_ref, win_ref, wout_ref, g2_ref, b2_ref, o_ref, acc_ref, alpha=alpha, d_ff=d_ff, chunk=chunk)


def _out_proj_ffn_ln(h, ya, yb, yc, w, w_in, w_out, g, b, l, next_f32, nxt, *, alpha, tm=512, chunk=256):
    t, d = h.shape
    d_ff = w_out.shape[0]
    assert t % tm == 0 and d_ff % chunk == 0 and ya.shape[1] + yb.shape[1] + yc.shape[1] == d
    row = lambda i: (i, 0)
    srcs, cast_in, cast_out, cast_shapes = _next_weight_cast(next_f32, nxt, t // tm)
    outs = pl.pallas_call(
        functools.partial(_out_ffn_kernel, n_cast=len(srcs), alpha=alpha, cw=ya.shape[1], fw=yb.shape[1],
                          d_ff=d_ff, chunk=chunk),
        grid=(t // tm,),
        in_specs=[pl.BlockSpec((tm, d), row), pl.BlockSpec((tm, ya.shape[1]), row),
                  pl.BlockSpec((tm, yb.shape[1]), row), pl.BlockSpec((tm, yc.shape[1]), row),
                  _resident(w, l), _resident(g, l, 1), _resident(b, l, 1),
                  _resident(w_in), _resident(w_out), _resident(g, l, 2), _resident(b, l, 2)] + cast_in,
        out_specs=[pl.BlockSpec((tm, d), row)] + cast_out,
        out_shape=[jax.ShapeDtypeStruct((t, d), F32)] + cast_shapes,
        scratch_shapes=[pltpu.VMEM((d, d), BF16), pltpu.VMEM((tm, d), F32), pltpu.VMEM((tm, d), F32)],
        compiler_params=_params("arbitrary"),
        name="out_proj_ffn_ln",
    )(h, ya, yb, yc, w, g, b, w_in, w_out, g, b, *srcs)
    return outs[0], outs[1:]


def _rope_tables(seq):
    lane = jnp.arange(LANES)
    inv = 1.0 / (ROPE_THETA ** ((2 * (lane % (HEAD_DIM // 2))).astype(F32) / HEAD_DIM))
    ang = jnp.arange(seq, dtype=F32)[:, None] * inv[None, :]
    cos, sin = jnp.cos(ang), jnp.sin(ang)
    first_half = (lane % HEAD_DIM < HEAD_DIM // 2)[None, :]
    sin_lo = jnp.where(first_half, -sin, 0.0)
    sin_hi = jnp.where(first_half, 0.0, sin)
    return cos, sin_lo, sin_hi


def kernel(x, w_in, w_o, forget_bias, conv_w, conv_b, conv_ln_g, conv_ln_b, ffn_w_in, ffn_w_out, ln_g, ln_b):
    bsz, seq, d = x.shape
    depth = w_in.shape[0]
    cw = conv_w.shape[-1]
    heads_f = forget_bias.shape[-1]
    fw = heads_f * HEAD_DIM
    dw = d - cw - fw
    alpha = (2 * depth) ** 0.25
    t = bsz * seq
    cos, slo, shi = _rope_tables(seq)

    ffn_f32 = (ffn_w_in, ffn_w_out)
    ffn_bf = (ffn_w_in[0, 0].astype(BF16), ffn_w_out[0, 0].astype(BF16))
    w_mix = w_in.astype(BF16)
    fb = jnp.repeat(forget_bias, HEAD_DIM, axis=1)[:, None, :]
    row3 = lambda a: a[:, None, :]
    cb, cg, cbeta = row3(conv_b), row3(conv_ln_g), row3(conv_ln_b)
    g4, b4 = ln_g[:, :, None, :], ln_b[:, :, None, :]
    b3 = lambda a: a.reshape(bsz, seq, a.shape[-1])

    h = x.reshape(t, d)
    for l in range(depth):
        h, ffn_bf = _ffn_ln(h, *ffn_bf, g4, b4, l, 0, ffn_f32, (l, 1), alpha=alpha)
        ya, qa, ka, va, dq, dk, dv = _mixer_proj(h, w_mix, fb, cos, slo, shi, conv_w, cb, cg, cbeta, l,
                                                 bsz=bsz, fw=fw, dw=dw)
        yb = _fox_attention(qa, ka, va)
        yc = _dilated_attention(b3(dq), b3(dk), b3(dv))
        h, ffn_bf = _out_proj_ffn_ln(h, ya, yb.reshape(t, fw), yc.reshape(t, dw), w_o, *ffn_bf, g4, b4, l,
                                     ffn_f32, (l + 1, 0) if l + 1 < depth else None, alpha=alpha)
    return h.reshape(bsz, seq, d)
```

```python
import functools

import jax
import jax.numpy as jnp
from jax import lax
from jax.experimental import pallas as pl
from jax.experimental.pallas import tpu as pltpu

HEAD_DIM = 64
CONV_K = 31
DIL_CONFIGS = ((128, 1), (512, 4), (2048, 16))
ROPE_THETA = 10000.0
LN_EPS = 1e-5
LANES = 128
F32_SUBLANES = 8
BF16_SUBLANES = 16
HEADS_PER_SLAB = LANES // HEAD_DIM
NEG_BIG = -1e30
LOG2_E = 1.4426950408889634
SCORE_SCALE = HEAD_DIM ** -0.5 * LOG2_E
VMEM_LIMIT = 56 * 1024 * 1024
FFN_PART_ROWS = 512

F32 = jnp.float32
BF16 = jnp.bfloat16


def _layer_norm(y, g, b):
    mu = jnp.mean(y, axis=-1, keepdims=True)
    yc = y - mu
    var = jnp.mean(yc * yc, axis=-1, keepdims=True)
    return yc * lax.rsqrt(var + LN_EPS) * g + b


def _params(*sem):
    return pltpu.CompilerParams(dimension_semantics=sem, vmem_limit_bytes=VMEM_LIMIT)


def _resident(arr, *lead):
    rest = arr.shape[len(lead):]
    index = tuple(lead) + (0,) * len(rest)
    return pl.BlockSpec((None,) * len(lead) + rest, lambda *_: index, pipeline_mode=pl.Buffered(1))


def _ffn_kernel(x_ref, win_ref, wout_ref, g_ref, b_ref, o_ref, acc_ref, *, alpha, d_ff, chunk):
    part_rows = min(FFN_PART_ROWS, x_ref.shape[0])
    for r0 in range(0, x_ref.shape[0], part_rows):
        rows = slice(r0, r0 + part_rows)
        x = x_ref[rows, :]
        xb = x.astype(BF16)
        for c in range(d_ff // chunk):
            lo = c * chunk
            gate = jnp.dot(xb, win_ref[:, lo:lo + chunk], preferred_element_type=F32)
            up = jnp.dot(xb, win_ref[:, d_ff + lo:d_ff + lo + chunk], preferred_element_type=F32)
            hid = (gate * jax.nn.sigmoid(gate) * up).astype(BF16)
            part = jnp.dot(hid, wout_ref[lo:lo + chunk, :], preferred_element_type=F32)
            if c == 0:
                acc_ref[rows, :] = part
            else:
                acc_ref[rows, :] += part
        y = alpha * x + 0.5 * acc_ref[rows, :]
        o_ref[rows, :] = _layer_norm(y, g_ref[...], b_ref[...])


def _next_weight_cast(weights, nxt, steps):
    if nxt is None:
        return [], [], [], []
    l, j = nxt
    in_specs, out_specs, out_shapes = [], [], []
    for w in weights:
        rows, cols = w.shape[-2:]
        n = steps
        while rows % n or (rows // n) % BF16_SUBLANES:
            n //= 2
        per = steps // n
        in_specs.append(pl.BlockSpec((None, None, rows // n, cols), lambda i, per=per: (l, j, i // per, 0)))
        out_specs.append(pl.BlockSpec((rows // n, cols), lambda i, per=per: (i // per, 0)))
        out_shapes.append(jax.ShapeDtypeStruct((rows, cols), BF16))
    return list(weights), in_specs, out_specs, out_shapes


def _ffn_ln_kernel(x_ref, win_ref, wout_ref, g_ref, b_ref, *rest, n_cast, **kw):
    srcs, (o_ref, *dsts), acc_ref = rest[:n_cast], rest[n_cast:-1], rest[-1]
    for src, dst in zip(srcs, dsts):
        dst[...] = src[...].astype(BF16)
    _ffn_kernel(x_ref, win_ref, wout_ref, g_ref, b_ref, o_ref, acc_ref, **kw)


def _ffn_ln(x, w_in, w_out, g, b, l, n, next_f32, nxt, *, alpha, tm=1024, chunk=256):
    t, d = x.shape
    d_ff = w_out.shape[0]
    assert t % tm == 0 and d_ff % chunk == 0 and chunk % LANES == 0
    srcs, cast_in, cast_out, cast_shapes = _next_weight_cast(next_f32, nxt, t // tm)
    outs = pl.pallas_call(
        functools.partial(_ffn_ln_kernel, n_cast=len(srcs), alpha=alpha, d_ff=d_ff, chunk=chunk),
        grid=(t // tm,),
        in_specs=[pl.BlockSpec((tm, d), lambda i: (i, 0)), _resident(w_in), _resident(w_out),
                  _resident(g, l, n), _resident(b, l, n)] + cast_in,
        out_specs=[pl.BlockSpec((tm, d), lambda i: (i, 0))] + cast_out,
        out_shape=[jax.ShapeDtypeStruct((t, d), F32)] + cast_shapes,
        scratch_shapes=[pltpu.VMEM((tm, d), F32)],
        compiler_params=_params("arbitrary"),
        name="ffn_ln",
    )(x, w_in, w_out, g, b, *srcs)
    return outs[0], outs[1:]


def _rope_slab(t, cos, sin_lo, sin_hi):
    nxt = pltpu.roll(t, LANES - HEAD_DIM // 2, axis=1)
    prv = pltpu.roll(t, HEAD_DIM // 2, axis=1)
    return t * cos + nxt * sin_lo + prv * sin_hi


def _split3(x):
    hi = x.astype(BF16).astype(F32)
    mid = (x - hi).astype(BF16).astype(F32)
    return hi, mid, x - hi - mid


def _store_fox_operands(cum, q, k, v, qa_ref, ka_ref, va_ref, rows):
    lane = lax.broadcasted_iota(jnp.int32, (1, LANES), 1)
    is_head = lane < HEAD_DIM
    aug = lane - HEAD_DIM
    q_ones = jnp.where(jnp.logical_and(aug >= 3, aug < 6), 1.0, 0.0)
    k_ones = jnp.where(jnp.logical_and(aug >= 0, aug < 3), 1.0, 0.0)
    k_bias = jnp.logical_and(aug >= 3, aug < 6)
    for p in range(q.shape[-1] // LANES):
        slab = slice(p * LANES, (p + 1) * LANES)
        split = _split3(cum[:, slab])
        for a in range(HEADS_PER_SLAB):
            h = HEADS_PER_SLAB * p + a
            odd = a == 1
            head_lanes = lambda x: pltpu.roll(x[:, slab], HEAD_DIM, axis=1) if odd else x[:, slab]
            hi, mid, lo = split if odd else [pltpu.roll(t, HEAD_DIM, axis=1) for t in split]
            q_aug = jnp.where(aug == 0, hi, jnp.where(aug == 1, mid, jnp.where(aug == 2, lo, q_ones)))
            k_aug = jnp.where(k_bias, -pltpu.roll(q_aug, 3, axis=1), k_ones)
            qa_ref[h, rows, :] = jnp.where(is_head, head_lanes(q), q_aug).astype(BF16)
            ka_ref[h, rows, :] = jnp.where(is_head, head_lanes(k), k_aug).astype(BF16)
            va_ref[h, rows, :] = jnp.where(is_head, head_lanes(v), 1.0).astype(BF16)


def _stage_mixer_weights(wraw_ref, w_ref, *, split, heads, rows):
    lane = lax.broadcasted_iota(jnp.int32, (1, LANES), 1)
    tail = wraw_ref.shape[1] - split - heads
    for r0 in range(0, wraw_ref.shape[0], rows):
        rs = slice(r0, r0 + rows)
        w_ref[rs, 0:split] = wraw_ref[rs, 0:split].astype(BF16)
        gates = wraw_ref[rs, split:split + heads].astype(F32)
        for p in range(heads // HEADS_PER_SLAB):
            even = jnp.broadcast_to(gates[:, HEADS_PER_SLAB * p:HEADS_PER_SLAB * p + 1], (rows, LANES))
            odd = jnp.broadcast_to(gates[:, HEADS_PER_SLAB * p + 1:HEADS_PER_SLAB * p + 2], (rows, LANES))
            w_ref[rs, split + p * LANES:split + (p + 1) * LANES] = (
                jnp.where(lane < HEAD_DIM, even, odd).astype(BF16))
        lo = split + heads * HEAD_DIM
        w_ref[rs, lo:lo + tail] = wraw_ref[rs, split + heads:split + heads + tail].astype(BF16)


def _conv_rows(pad_ref, t0, rows, halo, w_ref, cb_ref, g_ref, b_ref, o_ref):
    ch = pad_ref.shape[1]
    first = t0 + halo - (CONV_K - 1)
    slabs = [slice(c0, c0 + LANES) for c0 in range(0, ch, LANES)]
    accs = []
    sub = F32_SUBLANES
    base = first // sub * sub
    span = rows + (first - base + CONV_K - 1 + sub - 1) // sub * sub
    for cs in slabs:
        win = pad_ref[base:base + span, cs]
        parts = []
        for b in range(sub):
            rolled = win if b == 0 else pltpu.roll(win, span - b, axis=0)
            offs = [o for o in range(b, span - rows + 1, sub) if 0 <= base + o - first < CONV_K]
            parts.append(functools.reduce(jnp.add, [
                w_ref[base + o - first:base + o - first + 1, cs] * rolled[o - b:o - b + rows, :]
                for o in offs]))
        while len(parts) > 1:
            parts = [parts[i] + parts[i + 1] for i in range(0, len(parts), 2)]
        accs.append(parts[0] + cb_ref[:, cs])
    mu = sum(jnp.sum(a, axis=-1, keepdims=True) for a in accs) / ch
    cen = [a - mu for a in accs]
    var = sum(jnp.sum(c * c, axis=-1, keepdims=True) for c in cen) / ch
    inv = lax.rsqrt(var + LN_EPS)
    for cs, c in zip(slabs, cen):
        y = c * inv * g_ref[:, cs] + b_ref[:, cs]
        o_ref[t0:t0 + rows, cs] = (y * jax.nn.sigmoid(y)).astype(o_ref.dtype)


def _proj_kernel(h_ref, wraw_ref, fb_ref, cos_ref, slo_ref, shi_ref, cw_ref, cb_ref, cg_ref, cbeta_ref,
                 ya_ref, qa_ref, ka_ref, va_ref, dq_ref, dk_ref, dv_ref, w_ref, carry_ref, pad_ref,
                 *, cw, fw, dw, nsb, blk, halo, conv_rows):
    tm = h_ref.shape[0]

    @pl.when(pl.program_id(0) == 0)
    def _():
        _stage_mixer_weights(wraw_ref, w_ref, split=2 * cw + 3 * fw, heads=fw // HEAD_DIM, rows=blk)

    @pl.when(pl.program_id(0) % nsb == 0)
    def _():
        carry_ref[...] = jnp.zeros_like(carry_ref)
        pad_ref[0:halo, :] = jnp.zeros((halo, cw), F32)

    @pl.when(pl.program_id(0) % nsb != 0)
    def _():
        pad_ref[0:halo, :] = pad_ref[tm:tm + halo, :]

    hb = h_ref[...].astype(BF16)

    def cols(lo, width):
        return jnp.dot(hb, w_ref[:, lo:lo + width], preferred_element_type=F32)

    scale = SCORE_SCALE
    val = cols(0, cw)
    gate = cols(cw, cw)
    pad_ref[halo:halo + tm, :] = val * jax.nn.sigmoid(gate)
    for t0 in range(0, tm, conv_rows):
        _conv_rows(pad_ref, t0, conv_rows, halo, cw_ref, cb_ref, cg_ref, cbeta_ref, ya_ref)
    base = 2 * cw
    fq = cols(base, fw) * scale
    fk = cols(base + fw, fw)
    fv = cols(base + 2 * fw, fw)
    log_f = jax.nn.log_sigmoid(cols(base + 3 * fw, fw) + fb_ref[...])

    r = lax.broadcasted_iota(jnp.int32, (blk, blk), 0)
    c = lax.broadcasted_iota(jnp.int32, (blk, blk), 1)
    tri = (c <= r).astype(BF16)
    cum = carry_ref[...]
    for r0 in range(0, hb.shape[0], blk):
        rows = slice(r0, r0 + blk)
        cum = cum[-1:, :]
        for part in _split3(log_f[rows]):
            cum = cum + jnp.dot(tri, part.astype(BF16), preferred_element_type=F32)
        _store_fox_operands(cum * LOG2_E, fq[rows], fk[rows], fv[rows], qa_ref, ka_ref, va_ref, rows)
    carry_ref[...] = cum[-1:, :]

    base = base + 4 * fw
    cos, slo, shi = cos_ref[...], slo_ref[...], shi_ref[...]
    q, k = cols(base, dw), cols(base + dw, dw)
    for j in range(dw // LANES):
        sl = slice(j * LANES, (j + 1) * LANES)
        dq_ref[:, sl] = _rope_slab(q[:, sl], cos, slo, shi) * scale
        dk_ref[:, sl] = _rope_slab(k[:, sl], cos, slo, shi)
    dv_ref[...] = cols(base + 2 * dw, dw)


def _mixer_proj(h, w, fb, cos, slo, shi, conv_w, conv_b, conv_g, conv_beta, l, *, bsz, fw, dw,
                tm=512, blk=128, halo=32, conv_rows=64):
    t, d = h.shape
    cw = conv_w.shape[-1]
    seq = t // bsz
    nsb = seq // tm
    heads = fw // HEAD_DIM
    assert seq % tm == 0 and tm % blk == 0 and tm % conv_rows == 0 and heads % HEADS_PER_SLAB == 0
    assert CONV_K - 1 <= halo <= tm and halo % F32_SUBLANES == 0 and w.shape[-1] == 2 * cw + 3 * fw + heads + 3 * dw
    row = lambda i: (i, 0)
    pos = lambda i: (i % nsb, 0)
    tok = lambda n, dt: (pl.BlockSpec((tm, n), row), jax.ShapeDtypeStruct((t, n), dt))
    fox = (pl.BlockSpec((None, heads, tm, LANES), lambda i: (i // nsb, 0, i % nsb, 0)),
           jax.ShapeDtypeStruct((bsz, heads, seq, LANES), BF16))
    outs = [tok(cw, BF16), fox, fox, fox, tok(dw, F32), tok(dw, F32), tok(dw, F32)]
    return pl.pallas_call(
        functools.partial(_proj_kernel, cw=cw, fw=fw, dw=dw, nsb=nsb, blk=blk, halo=halo, conv_rows=conv_rows),
        grid=(t // tm,),
        in_specs=[pl.BlockSpec((tm, d), row), _resident(w, l), _resident(fb, l),
                  pl.BlockSpec((tm, LANES), pos), pl.BlockSpec((tm, LANES), pos),
                  pl.BlockSpec((tm, LANES), pos),
                  _resident(conv_w, l), _resident(conv_b, l), _resident(conv_g, l), _resident(conv_beta, l)],
        out_specs=[spec for spec, _ in outs],
        out_shape=[shape for _, shape in outs],
        scratch_shapes=[pltpu.VMEM((d, 2 * cw + 4 * fw + 3 * dw), BF16), pltpu.VMEM((1, fw), F32),
                        pltpu.VMEM((halo + tm, cw), F32)],
        compiler_params=_params("arbitrary"),
        name="mixer_proj",
    )(h, w, fb, cos, slo, shi, conv_w, conv_b, conv_g, conv_beta)


def _fox_kernel(q_ref, k_ref, v_ref, o_ref, m_ref, acc_ref, *, tq, nq):
    i = pl.program_id(2)
    half = tq // 2
    dn = (((1,), (1,)), ((), ()))

    def update(a, rows, k0, width, mask):
        keys = slice(k0, k0 + width)
        s = lax.dot_general(q_ref[a, rows, :], k_ref[a, keys, :], dn, preferred_element_type=F32)
        if mask is not None:
            s = jnp.where(mask, s, NEG_BIG)
        m_old = m_ref[a, rows, :]
        m_new = jnp.maximum(m_old, jnp.max(s, axis=-1, keepdims=True))
        alpha = jnp.exp2(m_old - m_new)
        p = jnp.exp2(s - jnp.concatenate([m_new] * (width // LANES), axis=1))
        m_ref[a, rows, :] = m_new
        acc_ref[a, rows, :] = alpha * acc_ref[a, rows, :] + jnp.dot(
            p.astype(BF16), v_ref[a, keys, :], preferred_element_type=F32)

    lower = (lax.broadcasted_iota(jnp.int32, (half, half), 1)
             <= lax.broadcasted_iota(jnp.int32, (half, half), 0))
    upper = (lax.broadcasted_iota(jnp.int32, (half, tq), 1)
             <= lax.broadcasted_iota(jnp.int32, (half, tq), 0) + half)
    lane = lax.broadcasted_iota(jnp.int32, (1, LANES), 1)

    def q_block(n_before):
        m_ref[...] = jnp.full(m_ref.shape, NEG_BIG, F32)
        acc_ref[...] = jnp.zeros(acc_ref.shape, F32)
        for a in range(HEADS_PER_SLAB):
            update(a, slice(0, half), n_before * tq, half, lower)
            update(a, slice(half, tq), n_before * tq, tq, upper)
        for j in range(n_before):
            for a in range(HEADS_PER_SLAB):
                update(a, slice(0, tq), j * tq, tq, None)
        acc0, acc1 = acc_ref[0], acc_ref[1]
        out = jnp.where(lane < HEAD_DIM,
                        acc0 / pltpu.roll(acc0, HEAD_DIM, axis=1),
                        pltpu.roll(acc1, HEAD_DIM, axis=1) / acc1)
        o_ref[...] = out.astype(o_ref.dtype)

    for n in range(nq):
        pl.when(i == n)(functools.partial(q_block, n))


def _fox_attention(qa, ka, va, *, tq=1024):
    bsz, heads, seq, _ = qa.shape
    assert seq % tq == 0 and heads % HEADS_PER_SLAB == 0
    qspec = pl.BlockSpec((None, HEADS_PER_SLAB, tq, LANES), lambda b, p, i: (b, p, i, 0))
    kvspec = pl.BlockSpec((None, HEADS_PER_SLAB, seq, LANES), lambda b, p, i: (b, p, 0, 0))
    return pl.pallas_call(
        functools.partial(_fox_kernel, tq=tq, nq=seq // tq),
        grid=(bsz, heads // HEADS_PER_SLAB, seq // tq),
        in_specs=[qspec, kvspec, kvspec],
        out_specs=pl.BlockSpec((None, tq, LANES), lambda b, p, i: (b, i, p)),
        out_shape=jax.ShapeDtypeStruct((bsz, seq, heads * HEAD_DIM), BF16),
        scratch_shapes=[pltpu.VMEM((HEADS_PER_SLAB, tq, LANES), F32),
                        pltpu.VMEM((HEADS_PER_SLAB, tq, LANES), F32)],
        compiler_params=_params("parallel", "parallel", "arbitrary"),
        name="fox_attention",
    )(qa, ka, va)


def _dil_kernel(q_ref, k_ref, v_ref, o_ref, bias_ref, bias2_ref, m_ref, l_ref, acc_ref, *,
                blk, rows_out):
    seq = q_ref.shape[0]
    win = 2 * blk
    lane = lax.broadcasted_iota(jnp.int32, (1, LANES), 1)
    head0 = lane < HEAD_DIM
    dist0 = (lax.broadcasted_iota(jnp.int32, (win, win), 0) % blk
             - lax.broadcasted_iota(jnp.int32, (win, win), 1))
    row = lax.broadcasted_iota(jnp.int32, (2 * win, win), 0)
    col = lax.broadcasted_iota(jnp.int32, (2 * win, win), 1)
    bias_ref[0] = jnp.where(dist0 >= 0, 0.0, NEG_BIG)
    bias_ref[1] = jnp.where(jnp.logical_and(dist0 + blk >= 0, dist0 <= 0), 0.0, NEG_BIG)
    dist_all = (row % win) - col
    bias2_ref[...] = jnp.where(jnp.logical_and(dist_all >= 0, dist_all <= blk), 0.0, NEG_BIG)
    ones = jnp.ones((win, LANES), BF16)
    dn = (((1,), (1,)), ((), ()))

    def attend(bi, rows, wrows, bias):
        nq = rows.size
        qb = q_ref[rows, :].astype(BF16)
        zero = jnp.zeros_like(qb)
        qs = jnp.concatenate([jnp.where(head0, qb, zero), jnp.where(head0, zero, qb)], axis=0)
        kw = k_ref[wrows, :].astype(BF16)
        vw = jnp.concatenate([v_ref[wrows, :].astype(BF16), ones], axis=1)
        s = lax.dot_general(qs, kw, dn, preferred_element_type=F32) + bias
        m = jnp.max(s, axis=-1, keepdims=True)
        p = jnp.exp2(s - m).astype(BF16)
        pv = jnp.dot(p, vw, preferred_element_type=F32)
        mb = jnp.broadcast_to(m, (2 * nq, LANES))
        m_ref[bi, rows, :] = jnp.where(head0, mb[:nq], mb[nq:])
        acc_ref[bi, rows, :] = jnp.where(head0, pv[:nq, :LANES], pv[nq:, :LANES])
        l_ref[bi, rows, :] = jnp.where(head0, pv[:nq, LANES:], pv[nq:, LANES:])

    def merge(r0):
        rows = slice(r0, r0 + rows_out)
        ms = [m_ref[bi, rows, :] for bi in range(len(DIL_CONFIGS))]
        m_all = functools.reduce(jnp.maximum, ms)
        ws = [jnp.exp2(m - m_all) for m in ms]
        num = sum(w * acc_ref[bi, rows, :] for bi, w in enumerate(ws))
        den = sum(w * l_ref[bi, rows, :] for bi, w in enumerate(ws))
        o_ref[rows, :] = (num / den).astype(o_ref.dtype)

    assert DIL_CONFIGS[0][1] == 1 and rows_out % blk == 0
    for bi, (_, dil) in reversed(list(enumerate(DIL_CONFIGS))):
        nb = seq // (dil * blk)
        ds = (lambda start, size, dil=dil: pl.ds(start, size, stride=dil) if dil > 1 else pl.ds(start, size))
        for r in range(dil):
            if nb == 2:
                attend(bi, ds(r, win), ds(r, win), bias2_ref[...])
                continue
            for n in range(nb):
                start = r + n * (dil * blk)
                attend(bi, ds(start, blk), ds(start - (dil * blk if n else 0), win), bias_ref[1 if n else 0])
                if dil == 1 and (start + blk) % rows_out == 0:
                    merge(start + blk - rows_out)


def _dilated_attention(q, k, v, *, blk=128, rows_out=256):
    bsz, seq, width = q.shape
    nbr = len(DIL_CONFIGS)
    assert all(seq % (dil * blk) == 0 and seq // (dil * blk) >= 2 and win == dil * blk for win, dil in DIL_CONFIGS)
    assert width % LANES == 0 and seq % rows_out == 0
    spec = pl.BlockSpec((None, seq, LANES), lambda b, p: (b, 0, p))
    return pl.pallas_call(
        functools.partial(_dil_kernel, blk=blk, rows_out=rows_out),
        grid=(bsz, width // LANES),
        in_specs=[spec, spec, spec],
        out_specs=spec,
        out_shape=jax.ShapeDtypeStruct((bsz, seq, width), BF16),
        scratch_shapes=[pltpu.VMEM((2, 2 * blk, 2 * blk), F32), pltpu.VMEM((4 * blk, 2 * blk), F32)]
        + [pltpu.VMEM((nbr, seq, LANES), F32)] * 3,
        compiler_params=_params("parallel", "parallel"),
        name="dilated_attention",
    )(q, k, v)


def _out_ffn_kernel(h_ref, ya_ref, yb_ref, yc_ref, wraw_ref, g1_ref, b1_ref, win_ref, wout_ref, g2_ref, b2_ref,
                    *rest, n_cast, alpha, cw, fw, d_ff, chunk):
    srcs, (o_ref, *dsts), (w_ref, mid_ref, acc_ref) = rest[:n_cast], rest[n_cast:-3], rest[-3:]
    for src, dst in zip(srcs, dsts):
        dst[...] = src[...].astype(BF16)

    @pl.when(pl.program_id(0) == 0)
    def _():
        w_ref[...] = wraw_ref[...].astype(BF16)

    y = jnp.dot(ya_ref[...], w_ref[0:cw, :], preferred_element_type=F32)
    y = y + jnp.dot(yb_ref[...], w_ref[cw:cw + fw, :], preferred_element_type=F32)
    y = y + jnp.dot(yc_ref[...], w_ref[cw + fw:, :], preferred_element_type=F32)
    mid_ref[...] = _layer_norm(alpha * h_ref[...] + y, g1_ref[...], b1_ref[...])
    _ffn_kernel(mid_ref, win_ref, wout_ref, g2_ref, b2_ref, o_ref, acc_ref, alpha=alpha, d_ff=d_ff, chunk=chunk)


def _out_proj_ffn_ln(h, ya, yb, yc, w, w_in, w_out, g, b, l, next_f32, nxt, *, alpha, tm=512, chunk=256):
    t, d = h.shape
    d_ff = w_out.shape[0]
    assert t % tm == 0 and d_ff % chunk == 0 and ya.shape[1] + yb.shape[1] + yc.shape[1] == d
    row = lambda i: (i, 0)
    srcs, cast_in, cast_out, cast_shapes = _next_weight_cast(next_f32, nxt, t // tm)
    outs = pl.pallas_call(
        functools.partial(_out_ffn_kernel, n_cast=len(srcs), alpha=alpha, cw=ya.shape[1], fw=yb.shape[1],
                          d_ff=d_ff, chunk=chunk),
        grid=(t // tm,),
        in_specs=[pl.BlockSpec((tm, d), row), pl.BlockSpec((tm, ya.shape[1]), row),
                  pl.BlockSpec((tm, yb.shape[1]), row), pl.BlockSpec((tm, yc.shape[1]), row),
                  _resident(w, l), _resident(g, l, 1), _resident(b, l, 1),
                  _resident(w_in), _resident(w_out), _resident(g, l, 2), _resident(b, l, 2)] + cast_in,
        out_specs=[pl.BlockSpec((tm, d), row)] + cast_out,
        out_shape=[jax.ShapeDtypeStruct((t, d), F32)] + cast_shapes,
        scratch_shapes=[pltpu.VMEM((d, d), BF16), pltpu.VMEM((tm, d), F32), pltpu.VMEM((tm, d), F32)],
        compiler_params=_params("arbitrary"),
        name="out_proj_ffn_ln",
    )(h, ya, yb, yc, w, g, b, w_in, w_out, g, b, *srcs)
    return outs[0], outs[1:]


def _rope_tables(seq):
    lane = jnp.arange(LANES)
    inv = 1.0 / (ROPE_THETA ** ((2 * (lane % (HEAD_DIM // 2))).astype(F32) / HEAD_DIM))
    ang = jnp.arange(seq, dtype=F32)[:, None] * inv[None, :]
    cos, sin = jnp.cos(ang), jnp.sin(ang)
    first_half = (lane % HEAD_DIM < HEAD_DIM // 2)[None, :]
    sin_lo = jnp.where(first_half, -sin, 0.0)
    sin_hi = jnp.where(first_half, 0.0, sin)
    return cos, sin_lo, sin_hi


def kernel(x, w_in, w_o, forget_bias, conv_w, conv_b, conv_ln_g, conv_ln_b, ffn_w_in, ffn_w_out, ln_g, ln_b):
    bsz, seq, d = x.shape
    depth = w_in.shape[0]
    cw = conv_w.shape[-1]
    heads_f = forget_bias.shape[-1]
    fw = heads_f * HEAD_DIM
    dw = d - cw - fw
    alpha = (2 * depth) ** 0.25
    t = bsz * seq
    cos, slo, shi = _rope_tables(seq)

    ffn_f32 = (ffn_w_in, ffn_w_out)
    ffn_bf = (ffn_w_in[0, 0].astype(BF16), ffn_w_out[0, 0].astype(BF16))
    w_mix = w_in.astype(BF16)
    fb = jnp.repeat(forget_bias, HEAD_DIM, axis=1)[:, None, :]
    row3 = lambda a: a[:, None, :]
    cb, cg, cbeta = row3(conv_b), row3(conv_ln_g), row3(conv_ln_b)
    g4, b4 = ln_g[:, :, None, :], ln_b[:, :, None, :]
    b3 = lambda a: a.reshape(bsz, seq, a.shape[-1])

    h = x.reshape(t, d)
    for l in range(depth):
        h, ffn_bf = _ffn_ln(h, *ffn_bf, g4, b4, l, 0, ffn_f32, (l, 1), alpha=alpha)
        ya, qa, ka, va, dq, dk, dv = _mixer_proj(h, w_mix, fb, cos, slo, shi, conv_w, cb, cg, cbeta, l,
                                                 bsz=bsz, fw=fw, dw=dw)
        yb = _fox_attention(qa, ka, va)
        yc = _dilated_attention(b3(dq), b3(dk), b3(dv))
        h, ffn_bf = _out_proj_ffn_ln(h, ya, yb.reshape(t, fw), yc.reshape(t, dw), w_o, *ffn_bf, g4, b4, l,
                                     ffn_f32, (l + 1, 0) if l + 1 < depth else None, alpha=alpha)
    return h.reshape(bsz, seq, d)
```
